```python
import math
import jax, jax.numpy as jnp
from jax import lax
import numpy as np

D_MODEL = 1024
BATCH = 16
SEQ = 4096
DEPTH = 4

N_MIXERS = 2
N_HGRN_LAYERS = (DEPTH + 1) // 2
N_ATTN_LAYERS = DEPTH // 2

HGRN_EXPAND = 128
HGRN_HEADS = D_MODEL // HGRN_EXPAND
HGRN_HEAD_K = HGRN_EXPAND
HGRN_HEAD_V = D_MODEL // HGRN_HEADS
HGRN_CHUNK = 32

ATTN_HEAD_DIM = 64
ATTN_HEADS = D_MODEL // (2 * ATTN_HEAD_DIM)
ATTN_BLOCK = 128
ROPE_THETA = 10000.0

MOE_GROUPS = 4
MOE_EXPERTS_PER_GROUP = 8
MOE_EXPERTS = MOE_GROUPS * MOE_EXPERTS_PER_GROUP
MOE_TOP_K = 2
MOE_FF = D_MODEL // 2
MOE_BLOCK = 256

DEEPNORM_ALPHA = (2 * DEPTH) ** 0.25
DEEPNORM_BETA = (8 * DEPTH) ** -0.25
NORM_EPS = 1e-5

kernel_name = "hybrid_hgrn2_diffattn_hmoe_deepnorm"


def layer_norm(x, g, b):
    xf = x.astype(jnp.float32)
    mu = jnp.mean(xf, axis=-1, keepdims=True)
    var = jnp.mean(jnp.square(xf - mu), axis=-1, keepdims=True)
    y = (xf - mu) * lax.rsqrt(var + NORM_EPS) * g.astype(jnp.float32) + b.astype(jnp.float32)
    return y.astype(x.dtype)


def rms_norm(x, w):
    xf = x.astype(jnp.float32)
    return xf * lax.rsqrt(jnp.mean(jnp.square(xf), axis=-1, keepdims=True) + NORM_EPS) * w.astype(jnp.float32)


def rope(t):
    S, d = t.shape[1], t.shape[-1]
    half = d // 2
    inv_freq = ROPE_THETA ** (-jnp.arange(half, dtype=jnp.float32) / half)
    ang = jnp.arange(S, dtype=jnp.float32)[:, None] * inv_freq[None, :]
    cos = jnp.cos(ang)[None, :, None, :]
    sin = jnp.sin(ang)[None, :, None, :]
    tf = t.astype(jnp.float32)
    t1, t2 = tf[..., :half], tf[..., half:]
    return jnp.concatenate([t1 * cos - t2 * sin, t2 * cos + t1 * sin], axis=-1).astype(t.dtype)


def hgrn2_mixer(h, w_in, w_out, lower_bound, norm_w):
    B, S, D = h.shape
    H, K, V, C = HGRN_HEADS, HGRN_HEAD_K, HGRN_HEAD_V, HGRN_CHUNK
    n_chunks = S // C
    q, z, v, gate = jnp.split(h @ w_in, 4, axis=-1)
    z = z.astype(jnp.float32)
    lb = lower_bound.astype(jnp.float32)
    log_f = jnp.logaddexp(jax.nn.log_sigmoid(z), jnp.log(lb) + jax.nn.log_sigmoid(-z))
    k = (1.0 - lb) * jax.nn.sigmoid(-z)

    def to_chunks(t, dh):
        return t.astype(jnp.float32).reshape(B, n_chunks, C, H, dh).transpose(1, 0, 3, 2, 4)

    causal = jnp.tril(jnp.ones((C, C), dtype=bool))[:, :, None]

    def chunk_step(state, inp):
        q_c, k_c, v_c, g_c = inp
        b = jnp.cumsum(g_c, axis=2)
        o_inter = jnp.einsum('bhtk,bhkv->bhtv', q_c * jnp.exp(b), state)
        rel = b[:, :, :, None, :] - b[:, :, None, :, :]
        decay = jnp.where(causal, jnp.exp(jnp.where(causal, rel, 0.0)), 0.0)
        scores = jnp.einsum('bhtk,bhsk,bhtsk->bhts', q_c, k_c, decay)
        o_c = o_inter + jnp.einsum('bhts,bhsv->bhtv', scores, v_c)
        b_last = b[:, :, -1:, :]
        state = (jnp.exp(b_last[:, :, 0, :, None]) * state
                 + jnp.einsum('bhsk,bhsv->bhkv', k_c * jnp.exp(b_last - b), v_c))
        return state, o_c

    state0 = jnp.zeros((B, H, K, V), jnp.float32)
    _, o = lax.scan(chunk_step, state0,
                    (to_chunks(q, K), to_chunks(k, K), to_chunks(v, V), to_chunks(log_f, K)))
    o = o.transpose(1, 0, 3, 2, 4).reshape(B, S, H, V)
    o = rms_norm(o, norm_w) * jax.nn.silu(gate.astype(jnp.float32).reshape(B, S, H, V))
    return o.reshape(B, S, D).astype(h.dtype) @ w_out


def diff_attention(h, w_in, w_out, lam_params, subln_w, lambda_init):
    B, S, D = h.shape
    H, d, Q = ATTN_HEADS, ATTN_HEAD_DIM, ATTN_BLOCK
    q, k, v = jnp.split(h @ w_in, 3, axis=-1)
    q = rope(q.reshape(B, S, 2 * H, d)).reshape(B, S, H, 2, d).transpose(0, 2, 3, 1, 4)
    k = rope(k.reshape(B, S, 2 * H, d)).reshape(B, S, H, 2, d).transpose(0, 2, 3, 1, 4)
    v = v.reshape(B, S, H, 2 * d).transpose(0, 2, 1, 3).astype(jnp.float32)
    lp = lam_params.astype(jnp.float32)
    lam = jnp.exp(jnp.sum(lp[0] * lp[1])) - jnp.exp(jnp.sum(lp[2] * lp[3])) + lambda_init
    scale = d ** -0.5
    outs = []
    for blk in range(S // Q):
        q0, kend = blk * Q, (blk + 1) * Q
        s = jnp.einsum('bhmqd,bhmkd->bhmqk', q[:, :, :, q0:kend], k[:, :, :, :kend]).astype(jnp.float32) * scale
        mask = (q0 + jnp.arange(Q))[:, None] >= jnp.arange(kend)[None, :]
        p = jax.nn.softmax(jnp.where(mask, s, -jnp.inf), axis=-1)
        a = p[:, :, 0] - lam * p[:, :, 1]
        outs.append(jnp.einsum('bhqk,bhkv->bhqv', a, v[:, :, :kend]))
    o = jnp.concatenate(outs, axis=2)
    o = rms_norm(o, subln_w) * (1.0 - lambda_init)
    return o.transpose(0, 2, 1, 3).reshape(B, S, D).astype(h.dtype) @ w_out


def hierarchical_moe(h, wg, bg, we, be, w1, w3, w2):
    B, S, D = h.shape
    N = B * S
    E, G, EPG, T = MOE_EXPERTS, MOE_GROUPS, MOE_EXPERTS_PER_GROUP, MOE_BLOCK
    xf = h.reshape(N, D)
    g_prob = jax.nn.softmax((xf @ wg).astype(jnp.float32) + bg.astype(jnp.float32), axis=-1)
    g_w, g_idx = lax.top_k(g_prob, 1)
    e_logits = (xf @ we).astype(jnp.float32).reshape(N, G, EPG) + be.astype(jnp.float32).reshape(G, EPG)
    e_prob = jax.nn.softmax(jnp.take_along_axis(e_logits, g_idx[:, :, None], axis=1)[:, 0], axis=-1)
    e_w, e_loc = lax.top_k(e_prob, MOE_TOP_K)
    weights = g_w * (e_w / jnp.sum(e_w, axis=-1, keepdims=True))
    expert = g_idx * EPG + e_loc

    A = N * MOE_TOP_K
    eid = expert.reshape(A)
    tok = jnp.repeat(jnp.arange(N, dtype=jnp.int32), MOE_TOP_K)
    wt = weights.reshape(A)
    order = jnp.argsort(eid)
    eid_s, tok_s, wt_s = eid[order], tok[order], wt[order]
    counts = jnp.bincount(eid, length=E)
    start = jnp.cumsum(counts) - counts
    padded = ((counts + T - 1) // T) * T
    pad_end = jnp.cumsum(padded)
    pad_start = pad_end - padded
    dest = pad_start[eid_s] + (jnp.arange(A, dtype=jnp.int32) - start[eid_s])
    L = A + E * T
    NB = L // T
    buf_tok = jnp.full((L,), N, jnp.int32).at[dest].set(tok_s)
    buf_w = jnp.zeros((L,), jnp.float32).at[dest].set(wt_s)
    blk_expert = jnp.minimum(jnp.searchsorted(pad_end, jnp.arange(NB, dtype=jnp.int32) * T, side='right'), E - 1)
    x_pad = jnp.concatenate([xf, jnp.zeros((1, D), xf.dtype)], axis=0)
    x_buf = x_pad[buf_tok].reshape(NB, T, D)

    def expert_block(args):
        xb, e = args
        return (jax.nn.silu(xb @ w1[e]) * (xb @ w3[e])) @ w2[e]

    y_buf = lax.map(expert_block, (x_buf, blk_expert)).reshape(L, D)
    y = jax.ops.segment_sum(y_buf.astype(jnp.float32) * buf_w[:, None], buf_tok, num_segments=N + 1)[:N]
    return y.reshape(B, S, D).astype(h.dtype)


def setup_inputs(seed: int = 0) -> dict:
    key = jax.random.key(seed)
    ks = jax.random.split(key, 24)
    D, F, E = D_MODEL, MOE_FF, MOE_EXPERTS
    nrm = lambda k, shape, s: jax.random.normal(k, shape, jnp.float32) * s
    hgrn_w_in = nrm(ks[6], (N_HGRN_LAYERS, D, 4 * D), D ** -0.5)
    hgrn_w_in = hgrn_w_in.at[:, :, 2 * D:3 * D].multiply(DEEPNORM_BETA)
    attn_w_in = nrm(ks[10], (N_ATTN_LAYERS, D, 3 * D), D ** -0.5)
    attn_w_in = attn_w_in.at[:, :, 2 * D:].multiply(DEEPNORM_BETA)
    return {
        "x": nrm(ks[0], (BATCH, SEQ, D), 1.0),
        "c": nrm(ks[1], (BATCH, D), 1.0),
        "ada_w": nrm(ks[2], (DEPTH, D, 6 * D), 0.5 * D ** -0.5),
        "ada_b": nrm(ks[3], (DEPTH, 6 * D), 0.02),
        "ln_g": 1.0 + nrm(ks[4], (DEPTH, 2, D), 0.02),
        "ln_b": nrm(ks[5], (DEPTH, 2, D), 0.02),
        "hgrn_w_in": hgrn_w_in,
        "hgrn_w_out": nrm(ks[7], (N_HGRN_LAYERS, D, D), DEEPNORM_BETA * D ** -0.5),
        "hgrn_lb": nrm(ks[8], (N_HGRN_LAYERS, HGRN_HEADS * HGRN_HEAD_K), 0.5),
        "hgrn_norm_w": 1.0 + nrm(ks[9], (N_HGRN_LAYERS, HGRN_HEAD_V), 0.02),
        "attn_w_in": attn_w_in,
        "attn_w_out": nrm(ks[11], (N_ATTN_LAYERS, D, D), DEEPNORM_BETA * D ** -0.5),
        "attn_lambda": nrm(ks[12], (N_ATTN_LAYERS, 4, ATTN_HEAD_DIM), 0.1),
        "attn_subln_w": 1.0 + nrm(ks[13], (N_ATTN_LAYERS, 2 * ATTN_HEAD_DIM), 0.02),
        "router_g_w": nrm(ks[14], (DEPTH, D, MOE_GROUPS), D ** -0.5),
        "router_g_b": nrm(ks[15], (DEPTH, MOE_GROUPS), 0.01),
        "router_e_w": nrm(ks[16], (DEPTH, D, E), D ** -0.5),
        "router_e_b": nrm(ks[17], (DEPTH, E), 0.01),
        "moe_w1": nrm(ks[18], (DEPTH, E, D, F), D ** -0.5),
        "moe_w3": nrm(ks[19], (DEPTH, E, D, F), D ** -0.5),
        "moe_w2": nrm(ks[20], (DEPTH, E, F, D), DEEPNORM_BETA * F ** -0.5),
    }


def reference(x, c, ada_w, ada_b, ln_g, ln_b, hgrn_w_in, hgrn_w_out, hgrn_lb, hgrn_norm_w,
              attn_w_in, attn_w_out, attn_lambda, attn_subln_w, router_g_w, router_g_b,
              router_e_w, router_e_b, moe_w1, moe_w3, moe_w2):
    lb_all = jnp.cumsum(jax.nn.softmax(hgrn_lb.astype(jnp.float32), axis=0), axis=0)
    lb_all = lb_all - lb_all[0:1]
    cond = jax.nn.silu(c.astype(jnp.float32))
    for i in range(DEPTH):
        mod = (cond @ ada_w[i].astype(jnp.float32) + ada_b[i].astype(jnp.float32)).astype(x.dtype)
        sh1, sc1, g1, sh2, sc2, g2 = jnp.split(mod[:, None, :], 6, axis=-1)
        h = x * (1 + sc1) + sh1
        j = i // N_MIXERS
        if i % N_MIXERS == 0:
            y = hgrn2_mixer(h, hgrn_w_in[j], hgrn_w_out[j], lb_all[j], hgrn_norm_w[j])
        else:
            lambda_init = 0.8 - 0.6 * math.exp(-0.3 * i)
            y = diff_attention(h, attn_w_in[j], attn_w_out[j], attn_lambda[j], attn_subln_w[j], lambda_init)
        x = layer_norm(DEEPNORM_ALPHA * x + g1 * y, ln_g[i, 0], ln_b[i, 0])
        h = x * (1 + sc2) + sh2
        y = hierarchical_moe(h, router_g_w[i], router_g_b[i], router_e_w[i], router_e_b[i],
                             moe_w1[i], moe_w3[i], moe_w2[i])
        x = layer_norm(DEEPNORM_ALPHA * x + g2 * y, ln_g[i, 1], ln_b[i, 1])
    return x
```

```python
import functools
import math

import jax
import jax.numpy as jnp
from jax import lax
from jax.experimental import pallas as pl
from jax.experimental.pallas import tpu as pltpu

F32 = jnp.float32
BF16 = jnp.bfloat16
I32 = jnp.int32
U32 = jnp.uint32
HIGHEST = lax.Precision.HIGHEST

LANES = 128
HEAD = 128
HGRN_CHUNK = 32
ATTN_D = 64
ROPE_THETA = 10000.0
MOE_GROUPS = 4
MOE_EPG = 8
MOE_EXPERTS = MOE_GROUPS * MOE_EPG
MOE_BLOCK = 256
NORM_EPS = 1e-5
VMEM_LIMIT = 56 * 1024 * 1024

NT_DIMS = (((1,), (1,)), ((), ()))
TN_DIMS = (((0,), (0,)), ((), ()))


def _cparams(sem):
    return pltpu.CompilerParams(dimension_semantics=sem, vmem_limit_bytes=VMEM_LIMIT)


def _const_spec(shape):
    nd = len(shape)
    return pl.BlockSpec(shape, lambda *_: (0,) * nd)


def _layer_norm(r, g, b):
    mu = jnp.mean(r, axis=-1, keepdims=True)
    d = r - mu
    var = jnp.mean(d * d, axis=-1, keepdims=True)
    return d * lax.rsqrt(var + NORM_EPS) * g + b


def _silu(x):
    return x * (1.0 / (1.0 + jnp.exp(-x)))


def _pack_bf16_pair(lo, hi):
    lo_b = lax.bitcast_convert_type(lo.astype(BF16).astype(F32), U32)
    hi_b = lax.bitcast_convert_type(hi.astype(BF16).astype(F32), U32)
    return (hi_b & jnp.uint32(0xFFFF0000)) | (lo_b >> 16)


def _unpack_bf16_pair(u):
    lo = lax.bitcast_convert_type(u << 16, F32)
    hi = lax.bitcast_convert_type(u & jnp.uint32(0xFFFF0000), F32)
    return lo, hi


def _ada_kernel(c_ref, w_ref, b_ref, o_ref):
    c = c_ref[...]
    o_ref[0] = jnp.dot(_silu(c), w_ref[0], precision=HIGHEST, preferred_element_type=F32) + b_ref[0]


def _ada_mod(c, ada_w, ada_b):
    depth, d, n6 = ada_w.shape
    bsz = c.shape[0]
    tn = d
    return pl.pallas_call(
        _ada_kernel,
        grid=(depth, n6 // tn),
        in_specs=[
            pl.BlockSpec((bsz, d), lambda i, j: (0, 0)),
            pl.BlockSpec((1, d, tn), lambda i, j: (i, 0, j)),
            pl.BlockSpec((1, 1, tn), lambda i, j: (i, 0, j)),
        ],
        out_specs=pl.BlockSpec((1, bsz, tn), lambda i, j: (i, 0, j)),
        out_shape=jax.ShapeDtypeStruct((depth, bsz, n6), F32),
        compiler_params=_cparams(("arbitrary", "arbitrary")),
        name="ada_mod",
    )(c, ada_w, ada_b.reshape(depth, 1, n6))


def _hgrn_kernel(alpha, x_ref, sh_ref, sc_ref, g_ref, win_ref, wout_ref, loglb_ref, oml_ref,
                 nw_ref, lng_ref, lnb_ref, o_ref, proj_ref, st_ref, ocat_ref):
    ts, d = x_ref.shape[1], x_ref.shape[2]
    nh = d // HEAD
    c = HGRN_CHUNK
    nc = ts // c

    @pl.when(pl.program_id(1) == 0)
    def _():
        st_ref[...] = jnp.zeros_like(st_ref)

    x = x_ref[0]
    h = x * (1.0 + sc_ref[0]) + sh_ref[0]
    proj_ref[...] = jnp.dot(h.astype(BF16), win_ref[...], preferred_element_type=F32)

    z = proj_ref[:, d:2 * d]
    ls = jnp.minimum(z, 0.0) - jnp.log1p(jnp.exp(-jnp.abs(z)))
    lsn = ls - z
    cc = loglb_ref[...] + lsn
    log_f = jnp.maximum(ls, cc) + jnp.log1p(jnp.exp(-jnp.abs(ls - cc)))
    kk = oml_ref[...] * jnp.exp(lsn)

    pos = lax.broadcasted_iota(I32, (ts, 1), 0) % c
    b = log_f
    step = 1
    while step < c:
        b = b + jnp.where(pos >= step, pltpu.roll(b, step, 0), 0.0)
        step *= 2

    b3 = b.reshape(nc, c, d)
    b_last = b3[:, c - 1:c, :]
    b_mid = b3[:, c // 2 - 1:c // 2, :]
    q3 = proj_ref[:, 0:d].reshape(nc, c, d)
    k3 = kk.reshape(nc, c, d)
    q_inter = (q3 * jnp.exp(b3)).astype(BF16)
    k_state = (k3 * jnp.exp(b_last - b3)).astype(BF16)
    q_intra = (q3 * jnp.exp(b3 - b_mid)).astype(BF16)
    k_intra = (k3 * jnp.exp(b_mid - b3)).astype(BF16)
    dec = jnp.exp(b_last)
    v3 = proj_ref[:, 2 * d:3 * d].astype(BF16).reshape(nc, c, d)

    causal = (lax.broadcasted_iota(I32, (c, c), 0) >= lax.broadcasted_iota(I32, (c, c), 1))
    for ci in range(nc):
        for hd in range(nh):
            sl = slice(hd * HEAD, (hd + 1) * HEAD)
            st = st_ref[hd]
            o_inter = lax.dot_general(q_inter[ci, :, sl], st.astype(BF16), NT_DIMS,
                                      preferred_element_type=F32)
            sc = lax.dot_general(q_intra[ci, :, sl], k_intra[ci, :, sl], NT_DIMS,
                                 preferred_element_type=F32)
            p = jnp.where(causal, sc, 0.0).astype(BF16)
            vc = v3[ci, :, sl]
            ocat_ref[ci * c:(ci + 1) * c, sl] = o_inter + jnp.dot(p, vc, preferred_element_type=F32)
            upd = lax.dot_general(vc, k_state[ci, :, sl], TN_DIMS, preferred_element_type=F32)
            st_ref[hd] = st * dec[ci, :, sl] + upd

    for hd in range(nh):
        sl = slice(hd * HEAD, (hd + 1) * HEAD)
        oh = ocat_ref[:, sl]
        ms = jnp.mean(oh * oh, axis=-1, keepdims=True)
        gate = proj_ref[:, 3 * d + hd * HEAD:3 * d + (hd + 1) * HEAD]
        ocat_ref[:, sl] = oh * lax.rsqrt(ms + NORM_EPS) * nw_ref[...] * _silu(gate)
    y = jnp.dot(ocat_ref[...].astype(BF16), wout_ref[...], preferred_element_type=F32)
    r = alpha * x + g_ref[0] * y
    o_ref[0] = _layer_norm(r, lng_ref[...], lnb_ref[...])


def _hgrn_layer(x, sh, sc, g, w_in, w_out, lb, norm_w, ln_g, ln_b, alpha):
    bsz, s, d = x.shape
    ts = min(256, s)
    nh = d // HEAD
    vec = pl.BlockSpec((1, 1, d), lambda b, i: (b, 0, 0))
    tile = pl.BlockSpec((1, ts, d), lambda b, i: (b, i, 0))
    return pl.pallas_call(
        functools.partial(_hgrn_kernel, alpha),
        grid=(bsz, s // ts),
        in_specs=[tile, vec, vec, vec,
                  _const_spec((d, 4 * d)), _const_spec((d, d)),
                  _const_spec((1, d)), _const_spec((1, d)), _const_spec((1, HEAD)),
                  _const_spec((1, d)), _const_spec((1, d))],
        out_specs=tile,
        out_shape=jax.ShapeDtypeStruct((bsz, s, d), F32),
        scratch_shapes=[pltpu.VMEM((ts, 4 * d), F32),
                        pltpu.VMEM((nh, HEAD, HEAD), F32),
                        pltpu.VMEM((ts, d), F32)],
        compiler_params=_cparams(("arbitrary", "arbitrary")),
        name="hgrn_layer",
    )(x, sh, sc, g, w_in.astype(BF16), w_out.astype(BF16),
      jnp.log(lb).reshape(1, d), (1.0 - lb).reshape(1, d), norm_w.reshape(1, HEAD),
      ln_g.reshape(1, d), ln_b.reshape(1, d))


def _attn_in_kernel(x_ref, sh_ref, sc_ref, w_ref, cos_ref, sin_ref, q_ref, k_ref, v_ref):
    d = x_ref.shape[2]
    x = x_ref[0]
    h = x * (1.0 + sc_ref[0]) + sh_ref[0]
    qkv = jnp.dot(h.astype(BF16), w_ref[...], preferred_element_type=F32)
    cos = cos_ref[...]
    sin = sin_ref[...]
    lane = lax.broadcasted_iota(I32, (1, LANES), 1)
    first_half = (lane % ATTN_D) < (ATTN_D // 2)
    scale = ATTN_D ** -0.5
    for j in range(d // LANES):
        sl = slice(j * LANES, (j + 1) * LANES)
        for src, dst, mul in ((0, q_ref, scale), (d, k_ref, 1.0)):
            t = qkv[:, src + j * LANES:src + (j + 1) * LANES]
            partner = jnp.where(first_half, pltpu.roll(t, LANES - ATTN_D // 2, 1),
                                pltpu.roll(t, ATTN_D // 2, 1))
            dst[0, :, sl] = ((t * cos + partner * sin) * mul).astype(BF16)
    v_ref[0] = qkv[:, 2 * d:].astype(BF16)


def _flash_kernel(out_scale, lam_ref, q_ref, k_ref, v_ref, w_ref, o_ref, m_ref, l_ref, acc_ref):
    tq = q_ref.shape[1]
    qi = pl.program_id(2)
    q = q_ref[0]
    lane = lax.broadcasted_iota(I32, (1, LANES), 1)
    zero = jnp.zeros_like(q)
    qq = jnp.concatenate([jnp.where(lane < ATTN_D, q, zero), jnp.where(lane >= ATTN_D, q, zero)], axis=0)

    m_ref[...] = jnp.full_like(m_ref, -jnp.inf)
    l_ref[...] = jnp.zeros_like(l_ref)
    acc_ref[...] = jnp.zeros_like(acc_ref)

    def step(j, masked):
        kb = k_ref[0, pl.ds(pl.multiple_of(j * tq, tq), tq), :]
        vb = v_ref[0, pl.ds(pl.multiple_of(j * tq, tq), tq), :]
        s = lax.dot_general(qq, kb, NT_DIMS, preferred_element_type=F32)
        if masked:
            row = lax.broadcasted_iota(I32, (2 * tq, tq), 0) % tq
            col = lax.broadcasted_iota(I32, (2 * tq, tq), 1)
            s = jnp.where(row >= col, s, -jnp.inf)
        m_old = m_ref[...]
        m_new = jnp.maximum(m_old, jnp.max(s, axis=-1, keepdims=True))
        a = jnp.exp(m_old - m_new)
        p = jnp.exp(s - m_new)
        l_ref[...] = a * l_ref[...] + jnp.sum(p, axis=-1, keepdims=True)
        acc_ref[...] = a * acc_ref[...] + jnp.dot(p.astype(BF16), vb, preferred_element_type=F32)
        m_ref[...] = m_new

    def body(j, carry):
        step(j, False)
        return carry

    lax.fori_loop(0, qi, body, 0)
    step(qi, True)

    o_all = acc_ref[...] / l_ref[...]
    o = o_all[:tq] - lam_ref[0] * o_all[tq:]
    ms = jnp.mean(o * o, axis=-1, keepdims=True)
    o_ref[0] = (o * lax.rsqrt(ms + NORM_EPS) * w_ref[...] * out_scale).astype(BF16)


def _resid_ln_kernel(alpha, x_ref, g_ref, o_in_ref, w_ref, lng_ref, lnb_ref, o_ref):
    y = jnp.dot(o_in_ref[0], w_ref[...], preferred_element_type=F32)
    r = alpha * x_ref[0] + g_ref[0] * y
    o_ref[0] = _layer_norm(r, lng_ref[...], lnb_ref[...])


def _attn_layer(x, sh, sc, g, w_in, w_out, lam_params, subln_w, lambda_init, ln_g, ln_b, alpha):
    bsz, s, d = x.shape
    nh = d // HEAD
    ts = min(256, s)
    tq = min(512, s)
    half = ATTN_D // 2
    inv_freq = ROPE_THETA ** (-jnp.arange(half, dtype=F32) / half)
    ang = jnp.arange(s, dtype=F32)[:, None] * inv_freq[None, :]
    cos_t = jnp.tile(jnp.cos(ang), (1, LANES // half))
    sin_h = jnp.sin(ang)
    sin_t = jnp.tile(jnp.concatenate([-sin_h, sin_h], axis=1), (1, LANES // ATTN_D))

    vec = pl.BlockSpec((1, 1, d), lambda b, i: (b, 0, 0))
    tile = pl.BlockSpec((1, ts, d), lambda b, i: (b, i, 0))
    rope = pl.BlockSpec((ts, LANES), lambda b, i: (i, 0))
    q, k, v = pl.pallas_call(
        _attn_in_kernel,
        grid=(bsz, s // ts),
        in_specs=[tile, vec, vec, _const_spec((d, 3 * d)), rope, rope],
        out_specs=[tile, tile, tile],
        out_shape=[jax.ShapeDtypeStruct((bsz, s, d), BF16)] * 3,
        compiler_params=_cparams(("arbitrary", "arbitrary")),
        name="attn_in",
    )(x, sh, sc, w_in.astype(BF16), cos_t, sin_t)

    lp = lam_params.astype(F32)
    lam = (jnp.exp(jnp.sum(lp[0] * lp[1])) - jnp.exp(jnp.sum(lp[2] * lp[3])) + lambda_init).reshape(1)
    qspec = pl.BlockSpec((1, tq, HEAD), lambda b, h, i: (b, i, h))
    kvspec = pl.BlockSpec((1, s, HEAD), lambda b, h, i: (b, 0, h))
    o = pl.pallas_call(
        functools.partial(_flash_kernel, 1.0 - lambda_init),
        grid=(bsz, nh, s // tq),
        in_specs=[pl.BlockSpec(memory_space=pltpu.SMEM), qspec, kvspec, kvspec,
                  pl.BlockSpec((1, HEAD), lambda b, h, i: (0, 0))],
        out_specs=qspec,
        out_shape=jax.ShapeDtypeStruct((bsz, s, d), BF16),
        scratch_shapes=[pltpu.VMEM((2 * tq, 1), F32), pltpu.VMEM((2 * tq, 1), F32),
                        pltpu.VMEM((2 * tq, HEAD), F32)],
        compiler_params=_cparams(("arbitrary", "arbitrary", "arbitrary")),
        name="diff_flash",
    )(lam, q, k, v, subln_w.reshape(1, HEAD))

    return pl.pallas_call(
        functools.partial(_resid_ln_kernel, alpha),
        grid=(bsz, s // ts),
        in_specs=[tile, vec, tile, _const_spec((d, d)), _const_spec((1, d)), _const_spec((1, d))],
        out_specs=tile,
        out_shape=jax.ShapeDtypeStruct((bsz, s, d), F32),
        compiler_params=_cparams(("arbitrary", "arbitrary")),
        name="attn_out",
    )(x, g, o, w_out.astype(BF16), ln_g.reshape(1, d), ln_b.reshape(1, d))


def _router_kernel(x_ref, sh_ref, sc_ref, w_ref, bias_ref, hp_ref, mi_ref, mf_ref, cnt_ref, cnt_scr):
    ts, d = x_ref.shape[1], x_ref.shape[2]
    first = (pl.program_id(0) == 0) & (pl.program_id(1) == 0)

    @pl.when(first)
    def _():
        cnt_scr[...] = jnp.zeros_like(cnt_scr)

    x = x_ref[0]
    h = x * (1.0 + sc_ref[0]) + sh_ref[0]
    hp_ref[...] = _pack_bf16_pair(h[:, :d // 2], h[:, d // 2:])

    logits = jnp.dot(h, w_ref[...], precision=HIGHEST, preferred_element_type=F32) + bias_ref[...]
    lane = lax.broadcasted_iota(I32, (ts, LANES), 1)
    neg = -jnp.inf
    big = jnp.int32(LANES)
    is_g = lane < MOE_GROUPS
    gl = jnp.where(is_g, logits, neg)
    gmax = jnp.max(gl, axis=-1, keepdims=True)
    g_idx = jnp.min(jnp.where(gl == gmax, lane, big), axis=-1, keepdims=True)
    g_w = 1.0 / jnp.sum(jnp.exp(gl - gmax), axis=-1, keepdims=True)

    e_lane = lane - MOE_GROUPS
    in_grp = (e_lane >= g_idx * MOE_EPG) & (e_lane < (g_idx + 1) * MOE_EPG)
    el = jnp.where(in_grp, logits, neg)
    l1 = jnp.max(el, axis=-1, keepdims=True)
    i1 = jnp.min(jnp.where(el == l1, lane, big), axis=-1, keepdims=True)
    el2 = jnp.where(lane == i1, neg, el)
    l2 = jnp.max(el2, axis=-1, keepdims=True)
    i2 = jnp.min(jnp.where(el2 == l2, lane, big), axis=-1, keepdims=True)
    t = jnp.exp(l2 - l1)
    w1 = g_w / (1.0 + t)
    w2 = g_w * t / (1.0 + t)

    oh1 = (lane == i1)
    oh2 = (lane == i2)
    both = (oh1 | oh2).astype(BF16)
    tri = (lax.broadcasted_iota(I32, (ts, ts), 0) > lax.broadcasted_iota(I32, (ts, ts), 1)).astype(BF16)
    before = jnp.dot(tri, both, preferred_element_type=F32) + cnt_scr[...]
    r1 = jnp.sum(jnp.where(oh1, before, 0.0), axis=-1, keepdims=True)
    r2 = jnp.sum(jnp.where(oh2, before, 0.0), axis=-1, keepdims=True)
    cnt_scr[...] = cnt_scr[...] + jnp.sum(both.astype(F32), axis=0, keepdims=True)
    cnt_ref[...] = cnt_scr[...]

    e1 = i1 - MOE_GROUPS
    e2 = i2 - MOE_GROUPS
    mi_ref[...] = jnp.where(lane == 0, e1, jnp.where(lane == 1, e2, jnp.where(
        lane == 2, r1.astype(I32), jnp.where(lane == 3, r2.astype(I32), 0))))
    mf_ref[...] = jnp.where(lane == 0, w1, jnp.where(lane == 1, w2, 0.0))


def _dispatch_kernel(d1_hbm, d2_hbm, hp_ref, xb_in, xb_out, d1_s, d2_s, isem, rsem):
    del xb_in
    i = pl.program_id(0)
    ts = hp_ref.shape[0]
    c1 = pltpu.make_async_copy(d1_hbm.at[i], d1_s, isem.at[0])
    c2 = pltpu.make_async_copy(d2_hbm.at[i], d2_s, isem.at[1])
    c1.start()
    c2.start()
    c1.wait()
    c2.wait()

    def row_copy(t, dst):
        return pltpu.make_async_copy(hp_ref.at[pl.ds(t, 1), :], xb_out.at[pl.ds(dst, 1), :], rsem)

    def issue(t, carry):
        row_copy(t, d1_s[t]).start()
        row_copy(t, d2_s[t]).start()
        return carry

    lax.fori_loop(0, ts, issue, 0)

    def drain(t, carry):
        row_copy(t, 0).wait()
        row_copy(t, 0).wait()
        return carry

    lax.fori_loop(0, ts, drain, 0)


def _expert_kernel(be_ref, nu_ref, xb_ref, w1_ref, w3_ref, w2_ref, y_ref):
    del be_ref
    half = xb_ref.shape[1]

    @pl.when(pl.program_id(0) < nu_ref[0])
    def _():
        lo, hi = _unpack_bf16_pair(xb_ref[...])
        lo = lo.astype(BF16)
        hi = hi.astype(BF16)
        a = (jnp.dot(lo, w1_ref[0, :half, :], preferred_element_type=F32)
             + jnp.dot(hi, w1_ref[0, half:, :], preferred_element_type=F32))
        b = (jnp.dot(lo, w3_ref[0, :half, :], preferred_element_type=F32)
             + jnp.dot(hi, w3_ref[0, half:, :], preferred_element_type=F32))
        hid = (_silu(a) * b).astype(BF16)
        y = jnp.dot(hid, w2_ref[0], preferred_element_type=F32)
        y_ref[...] = _pack_bf16_pair(y[:, :half], y[:, half:])

    @pl.when(pl.program_id(0) >= nu_ref[0])
    def _():
        y_ref[...] = jnp.zeros_like(y_ref)


def _combine_kernel(alpha, d1_hbm, d2_hbm, x_ref, g_ref, mf_ref, lng_ref, lnb_ref, yb_hbm, o_ref,
                    d1_s, d2_s, y1_s, y2_s, isem, rsem):
    ts, d = x_ref.shape[1], x_ref.shape[2]
    half = d // 2
    i = pl.program_id(0) * pl.num_programs(1) + pl.program_id(1)
    c1 = pltpu.make_async_copy(d1_hbm.at[i], d1_s, isem.at[0])
    c2 = pltpu.make_async_copy(d2_hbm.at[i], d2_s, isem.at[1])
    c1.start()
    c2.start()
    c1.wait()
    c2.wait()

    def row_copy(src, t, buf):
        return pltpu.make_async_copy(yb_hbm.at[pl.ds(src, 1), :], buf.at[pl.ds(t, 1), :], rsem)

    def issue(t, carry):
        row_copy(d1_s[t], t, y1_s).start()
        row_copy(d2_s[t], t, y2_s).start()
        return carry

    lax.fori_loop(0, ts, issue, 0)

    def drain(t, carry):
        row_copy(0, t, y1_s).wait()
        row_copy(0, t, y2_s).wait()
        return carry

    lax.fori_loop(0, ts, drain, 0)

    w1 = mf_ref[:, 0:1]
    w2 = mf_ref[:, 1:2]
    lo1, hi1 = _unpack_bf16_pair(y1_s[...])
    lo2, hi2 = _unpack_bf16_pair(y2_s[...])
    x = x_ref[0]
    g = g_ref[0]
    r_lo = alpha * x[:, :half] + g[:, :half] * (w1 * lo1 + w2 * lo2)
    r_hi = alpha * x[:, half:] + g[:, half:] * (w1 * hi1 + w2 * hi2)
    mu = (jnp.sum(r_lo, axis=-1, keepdims=True) + jnp.sum(r_hi, axis=-1, keepdims=True)) / d
    d_lo = r_lo - mu
    d_hi = r_hi - mu
    var = (jnp.sum(d_lo * d_lo, axis=-1, keepdims=True) + jnp.sum(d_hi * d_hi, axis=-1, keepdims=True)) / d
    inv = lax.rsqrt(var + NORM_EPS)
    o_ref[0, :, :half] = d_lo * inv * lng_ref[:, :half] + lnb_ref[:, :half]
    o_ref[0, :, half:] = d_hi * inv * lng_ref[:, half:] + lnb_ref[:, half:]


def _moe_layer(x, sh, sc, g, wg, bg, we, be, w1, w3, w2, ln_g, ln_b, alpha):
    bsz, s, d = x.shape
    n = bsz * s
    half = d // 2
    e_num, blk = MOE_EXPERTS, MOE_BLOCK
    ff = w1.shape[-1]
    ts = min(512, s)
    nt_b = s // ts
    nt = n // ts
    n_rows = n * 2 + e_num * blk
    nb = n_rows // blk

    wcat = jnp.zeros((d, LANES), F32).at[:, :MOE_GROUPS].set(wg).at[:, MOE_GROUPS:MOE_GROUPS + e_num].set(we)
    bcat = jnp.zeros((1, LANES), F32).at[0, :MOE_GROUPS].set(bg).at[0, MOE_GROUPS:MOE_GROUPS + e_num].set(be)

    vec = pl.BlockSpec((1, 1, d), lambda b, i: (b, 0, 0))
    tile = pl.BlockSpec((1, ts, d), lambda b, i: (b, i, 0))
    flat = lambda w: pl.BlockSpec((ts, w), lambda b, i: (b * nt_b + i, 0))
    hp, mi, mf, cnt = pl.pallas_call(
        _router_kernel,
        grid=(bsz, nt_b),
        in_specs=[tile, vec, vec, _const_spec((d, LANES)), _const_spec((1, LANES))],
        out_specs=[flat(half), flat(LANES), flat(LANES), _const_spec((1, LANES))],
        out_shape=[jax.ShapeDtypeStruct((n, half), U32), jax.ShapeDtypeStruct((n, LANES), I32),
                   jax.ShapeDtypeStruct((n, LANES), F32), jax.ShapeDtypeStruct((1, LANES), F32)],
        scratch_shapes=[pltpu.VMEM((1, LANES), F32)],
        compiler_params=_cparams(("arbitrary", "arbitrary")),
        name="moe_router",
    )(x, sh, sc, wcat, bcat)

    counts = cnt[0, MOE_GROUPS:MOE_GROUPS + e_num].astype(I32)
    padded = ((counts + blk - 1) // blk) * blk
    pad_end = jnp.cumsum(padded)
    pad_start = pad_end - padded
    dest1 = (pad_start[mi[:, 0]] + mi[:, 2]).reshape(nt, ts)
    dest2 = (pad_start[mi[:, 1]] + mi[:, 3]).reshape(nt, ts)
    blk_expert = jnp.minimum(
        jnp.searchsorted(pad_end, jnp.arange(nb, dtype=I32) * blk, side='right'), e_num - 1).astype(I32)
    n_used = (pad_end[-1] // blk).astype(I32).reshape(1)

    any_spec = pl.BlockSpec(memory_space=pl.ANY)
    xbuf = pl.pallas_call(
        _dispatch_kernel,
        grid=(nt,),
        in_specs=[any_spec, any_spec, pl.BlockSpec((ts, half), lambda i: (i, 0)), any_spec],
        out_specs=any_spec,
        out_shape=jax.ShapeDtypeStruct((n_rows, half), U32),
        input_output_aliases={3: 0},
        scratch_shapes=[pltpu.SMEM((ts,), I32), pltpu.SMEM((ts,), I32),
                        pltpu.SemaphoreType.DMA((2,)), pltpu.SemaphoreType.DMA],
        compiler_params=_cparams(("arbitrary",)),
        name="moe_dispatch",
    )(dest1, dest2, hp, jnp.zeros((n_rows, half), U32))

    ybuf = pl.pallas_call(
        _expert_kernel,
        grid_spec=pltpu.PrefetchScalarGridSpec(
            num_scalar_prefetch=2,
            grid=(nb,),
            in_specs=[pl.BlockSpec((blk, half), lambda i, be_r, nu_r: (i, 0)),
                      pl.BlockSpec((1, d, ff), lambda i, be_r, nu_r: (be_r[i], 0, 0)),
                      pl.BlockSpec((1, d, ff), lambda i, be_r, nu_r: (be_r[i], 0, 0)),
                      pl.BlockSpec((1, ff, d), lambda i, be_r, nu_r: (be_r[i], 0, 0))],
            out_specs=pl.BlockSpec((blk, half), lambda i, be_r, nu_r: (i, 0)),
        ),
        out_shape=jax.ShapeDtypeStruct((n_rows, half), U32),
        compiler_params=_cparams(("arbitrary",)),
        name="moe_experts",
    )(blk_expert, n_used, xbuf, w1.astype(BF16), w3.astype(BF16), w2.astype(BF16))

    return pl.pallas_call(
        functools.partial(_combine_kernel, alpha),
        grid=(bsz, nt_b),
        in_specs=[any_spec, any_spec, tile, vec, flat(LANES),
                  _const_spec((1, d)), _const_spec((1, d)), any_spec],
        out_specs=tile,
        out_shape=jax.ShapeDtypeStruct((bsz, s, d), F32),
        scratch_shapes=[pltpu.SMEM((ts,), I32), pltpu.SMEM((ts,), I32),
                        pltpu.VMEM((ts, half), U32), pltpu.VMEM((ts, half), U32),
                        pltpu.SemaphoreType.DMA((2,)), pltpu.SemaphoreType.DMA],
        compiler_params=_cparams(("arbitrary", "arbitrary")),
        name="moe_combine",
    )(dest1, dest2, x, g, mf, ln_g.reshape(1, d), ln_b.reshape(1, d), ybuf)


def kernel(x, c, ada_w, ada_b, ln_g, ln_b, hgrn_w_in, hgrn_w_out, hgrn_lb, hgrn_norm_w, attn_w_in, attn_w_out,
           attn_lambda, attn_subln_w, router_g_w, router_g_b, router_e_w, router_e_b, moe_w1, moe_w3, moe_w2):
    depth = ada_w.shape[0]
    bsz, s, d = x.shape
    alpha = (2 * depth) ** 0.25
    lb_all = jnp.cumsum(jax.nn.softmax(hgrn_lb.astype(F32), axis=0), axis=0)
    lb_all = lb_all - lb_all[0:1]
    mod = _ada_mod(c, ada_w, ada_b).reshape(depth, bsz, 6, 1, d)
    for i in range(depth):
        sh1, sc1, g1, sh2, sc2, g2 = (mod[i, :, m] for m in range(6))
        j = i // 2
        if i % 2 == 0:
            x = _hgrn_layer(x, sh1, sc1, g1, hgrn_w_in[j], hgrn_w_out[j], lb_all[j], hgrn_norm_w[j],
                            ln_g[i, 0], ln_b[i, 0], alpha)
        else:
            lambda_init = 0.8 - 0.6 * math.exp(-0.3 * i)
            x = _attn_layer(x, sh1, sc1, g1, attn_w_in[j], attn_w_out[j], attn_lambda[j], attn_subln_w[j],
                            lambda_init, ln_g[i, 0], ln_b[i, 0], alpha)
        x = _moe_layer(x, sh2, sc2, g2, router_g_w[i], router_g_b[i], router_e_w[i], router_e_b[i],
                       moe_w1[i], moe_w3[i], moe_w2[i], ln_g[i, 1], ln_b[i, 1], alpha)
    return x
```

```python
import functools
import math

import jax
import jax.numpy as jnp
from jax import lax
from jax.experimental import pallas as pl
from jax.experimental.pallas import tpu as pltpu

F32 = jnp.float32
BF16 = jnp.bfloat16
I32 = jnp.int32
U32 = jnp.uint32
HIGHEST = lax.Precision.HIGHEST

LANES = 128
HEAD = 128
HGRN_CHUNK = 32
ATTN_D = 64
FLASH_ROW_CHUNK = 32
ROPE_THETA = 10000.0
MOE_GROUPS = 4
MOE_EPG = 8
MOE_EXPERTS = MOE_GROUPS * MOE_EPG
MOE_BLOCK = 256
NORM_EPS = 1e-5
VMEM_LIMIT = 56 * 1024 * 1024

NT_DIMS = (((1,), (1,)), ((), ()))
TN_DIMS = (((0,), (0,)), ((), ()))


def _cparams(sem):
    return pltpu.CompilerParams(dimension_semantics=sem, vmem_limit_bytes=VMEM_LIMIT)


def _const_spec(shape):
    nd = len(shape)
    return pl.BlockSpec(shape, lambda *_: (0,) * nd)


def _layer_norm(r, g, b):
    mu = jnp.mean(r, axis=-1, keepdims=True)
    d = r - mu
    var = jnp.mean(d * d, axis=-1, keepdims=True)
    return d * lax.rsqrt(var + NORM_EPS) * g + b


def _silu(x):
    return x * (1.0 / (1.0 + jnp.exp(-x)))


def _pack_bf16_pair(lo, hi):
    lo_b = lax.bitcast_convert_type(lo.astype(BF16).astype(F32), U32)
    hi_b = lax.bitcast_convert_type(hi.astype(BF16).astype(F32), U32)
    return (hi_b & jnp.uint32(0xFFFF0000)) | (lo_b >> 16)


def _unpack_bf16_pair(u):
    lo = lax.bitcast_convert_type(u << 16, F32)
    hi = lax.bitcast_convert_type(u & jnp.uint32(0xFFFF0000), F32)
    return lo, hi


def _ada_kernel(c_ref, w_ref, b_ref, o_ref):
    c = c_ref[...]
    o_ref[0] = jnp.dot(_silu(c), w_ref[0], precision=HIGHEST, preferred_element_type=F32) + b_ref[0]


def _ada_mod(c, ada_w, ada_b):
    depth, d, n6 = ada_w.shape
    bsz = c.shape[0]
    tn = d
    return pl.pallas_call(
        _ada_kernel,
        grid=(depth, n6 // tn),
        in_specs=[
            pl.BlockSpec((bsz, d), lambda i, j: (0, 0)),
            pl.BlockSpec((1, d, tn), lambda i, j: (i, 0, j)),
            pl.BlockSpec((1, 1, tn), lambda i, j: (i, 0, j)),
        ],
        out_specs=pl.BlockSpec((1, bsz, tn), lambda i, j: (i, 0, j)),
        out_shape=jax.ShapeDtypeStruct((depth, bsz, n6), F32),
        compiler_params=_cparams(("arbitrary", "arbitrary")),
        name="ada_mod",
    )(c, ada_w, ada_b.reshape(depth, 1, n6))


def _hgrn_kernel(alpha, x_ref, sh_ref, sc_ref, g_ref, win_ref, wout_ref, loglb_ref, oml_ref,
                 nw_ref, lng_ref, lnb_ref, o_ref, proj_ref, st_ref, ocat_ref):
    ts, d = x_ref.shape[1], x_ref.shape[2]
    nh = d // HEAD
    c = HGRN_CHUNK
    nc = ts // c

    @pl.when(pl.program_id(1) == 0)
    def _():
        st_ref[...] = jnp.zeros_like(st_ref)

    x = x_ref[0]
    h = x * (1.0 + sc_ref[0]) + sh_ref[0]
    proj_ref[...] = jnp.dot(h.astype(BF16), win_ref[...], preferred_element_type=F32)

    z = proj_ref[:, d:2 * d]
    ls = jnp.minimum(z, 0.0) - jnp.log(1.0 + jnp.exp(-jnp.abs(z)))
    lsn = ls - z
    cc = loglb_ref[...] + lsn
    log_f = jnp.maximum(ls, cc) + jnp.log(1.0 + jnp.exp(-jnp.abs(ls - cc)))
    kk = oml_ref[...] * jnp.exp(lsn)

    pos = lax.broadcasted_iota(I32, (ts, 1), 0) % c
    b = log_f
    step = 1
    while step < c:
        b = b + jnp.where(pos >= step, pltpu.roll(b, step, 0), 0.0)
        step *= 2

    b3 = b.reshape(nc, c, d)
    b_last = b3[:, c - 1:c, :]
    b_mid = b3[:, c // 2 - 1:c // 2, :]
    q3 = proj_ref[:, 0:d].reshape(nc, c, d)
    k3 = kk.reshape(nc, c, d)
    q_inter = (q3 * jnp.exp(b3)).astype(BF16)
    k_state = (k3 * jnp.exp(b_last - b3)).astype(BF16)
    q_intra = (q3 * jnp.exp(b3 - b_mid)).astype(BF16).reshape(ts, d)
    k_intra = (k3 * jnp.exp(b_mid - b3)).astype(BF16).reshape(ts, d)
    dec = jnp.exp(b_last)
    v2 = proj_ref[:, 2 * d:3 * d].astype(BF16)
    v3 = v2.reshape(nc, c, d)

    row = lax.broadcasted_iota(I32, (ts, ts), 0)
    col = lax.broadcasted_iota(I32, (ts, ts), 1)
    keep = (row >= col) & (row // c == col // c)
    for hd in range(nh):
        sl = slice(hd * HEAD, (hd + 1) * HEAD)
        sc = lax.dot_general(q_intra[:, sl], k_intra[:, sl], NT_DIMS, preferred_element_type=F32)
        p = jnp.where(keep, sc, 0.0).astype(BF16)
        ocat_ref[:, sl] = jnp.dot(p, v2[:, sl], preferred_element_type=F32)

    for ci in range(nc):
        for hd in range(nh):
            sl = slice(hd * HEAD, (hd + 1) * HEAD)
            st = st_ref[hd]
            o_inter = lax.dot_general(q_inter[ci, :, sl], st.astype(BF16), NT_DIMS,
                                      preferred_element_type=F32)
            ocat_ref[ci * c:(ci + 1) * c, sl] += o_inter
            upd = lax.dot_general(v3[ci, :, sl], k_state[ci, :, sl], TN_DIMS,
                                  preferred_element_type=F32)
            st_ref[hd] = st * dec[ci, :, sl] + upd

    for hd in range(nh):
        sl = slice(hd * HEAD, (hd + 1) * HEAD)
        oh = ocat_ref[:, sl]
        ms = jnp.mean(oh * oh, axis=-1, keepdims=True)
        gate = proj_ref[:, 3 * d + hd * HEAD:3 * d + (hd + 1) * HEAD]
        ocat_ref[:, sl] = oh * lax.rsqrt(ms + NORM_EPS) * nw_ref[...] * _silu(gate)
    y = jnp.dot(ocat_ref[...].astype(BF16), wout_ref[...], preferred_element_type=F32)
    r = alpha * x + g_ref[0] * y
    o_ref[0] = _layer_norm(r, lng_ref[...], lnb_ref[...])


def _hgrn_layer(x, sh, sc, g, w_in, w_out, lb, norm_w, ln_g, ln_b, alpha):
    bsz, s, d = x.shape
    ts = min(256, s)
    nh = d // HEAD
    vec = pl.BlockSpec((1, 1, d), lambda b, i: (b, 0, 0))
    tile = pl.BlockSpec((1, ts, d), lambda b, i: (b, i, 0))
    return pl.pallas_call(
        functools.partial(_hgrn_kernel, alpha),
        grid=(bsz, s // ts),
        in_specs=[tile, vec, vec, vec,
                  _const_spec((d, 4 * d)), _const_spec((d, d)),
                  _const_spec((1, d)), _const_spec((1, d)), _const_spec((1, HEAD)),
                  _const_spec((1, d)), _const_spec((1, d))],
        out_specs=tile,
        out_shape=jax.ShapeDtypeStruct((bsz, s, d), F32),
        scratch_shapes=[pltpu.VMEM((ts, 4 * d), F32),
                        pltpu.VMEM((nh, HEAD, HEAD), F32),
                        pltpu.VMEM((ts, d), F32)],
        compiler_params=_cparams(("arbitrary", "arbitrary")),
        name="hgrn_layer",
    )(x, sh, sc, g, w_in.astype(BF16), w_out.astype(BF16),
      jnp.log(lb).reshape(1, d), (1.0 - lb).reshape(1, d), norm_w.reshape(1, HEAD),
      ln_g.reshape(1, d), ln_b.reshape(1, d))


def _attn_in_kernel(x_ref, sh_ref, sc_ref, w_ref, cos_ref, sin_ref, q_ref, k_ref, v_ref):
    d = x_ref.shape[2]
    x = x_ref[0]
    h = x * (1.0 + sc_ref[0]) + sh_ref[0]
    qkv = jnp.dot(h.astype(BF16), w_ref[...], preferred_element_type=F32)
    cos = cos_ref[...]
    sin = sin_ref[...]
    lane = lax.broadcasted_iota(I32, (1, LANES), 1)
    first_half = (lane % ATTN_D) < (ATTN_D // 2)
    scale = ATTN_D ** -0.5 * math.log2(math.e)
    for j in range(d // LANES):
        sl = slice(j * LANES, (j + 1) * LANES)
        for src, dst, mul in ((0, q_ref, scale), (d, k_ref, 1.0)):
            t = qkv[:, src + j * LANES:src + (j + 1) * LANES]
            partner = jnp.where(first_half, pltpu.roll(t, LANES - ATTN_D // 2, 1),
                                pltpu.roll(t, ATTN_D // 2, 1))
            dst[0, :, sl] = ((t * cos + partner * sin) * mul).astype(BF16)
    v_ref[0] = qkv[:, 2 * d:].astype(BF16)


def _flash_kernel(out_scale, lam_ref, q_ref, k_ref, v_ref, w_ref, o_ref,
                  m_ref, l_ref, a_ref, acc_ref, sa_ref, sb_ref, p_ref, qq_ref):
    tq = q_ref.shape[1]
    rows = 2 * tq
    qi = pl.program_id(2)
    q = q_ref[0]
    lane = lax.broadcasted_iota(I32, (1, LANES), 1)
    zero = jnp.zeros_like(q)
    qq_ref[0:tq, :] = jnp.where(lane < ATTN_D, q, zero)
    qq_ref[tq:rows, :] = jnp.where(lane >= ATTN_D, q, zero)

    m_ref[...] = jnp.full_like(m_ref, -jnp.inf)
    l_ref[...] = jnp.zeros_like(l_ref)
    acc_ref[...] = jnp.zeros_like(acc_ref)
    nrep = tq // LANES
    rc = FLASH_ROW_CHUNK

    def scores(j, s_ref):
        kb = k_ref[0, pl.ds(pl.multiple_of(j * tq, tq), tq), :]
        s_ref[...] = lax.dot_general(qq_ref[...], kb, NT_DIMS, preferred_element_type=F32)

    def step(j, s_ref, masked):
        for r0 in range(0, rows, rc):
            rs = slice(r0, r0 + rc)
            s = s_ref[rs, :]
            if masked:
                row = lax.broadcasted_iota(I32, (rc, tq), 0) + (r0 % tq)
                col = lax.broadcasted_iota(I32, (rc, tq), 1)
                s = jnp.where(row >= col, s, -jnp.inf)
            m_old = m_ref[rs, :]
            m_new = jnp.maximum(m_old, jnp.max(s, axis=-1, keepdims=True))
            a = jnp.exp2(m_old - m_new)
            p = jnp.exp2(s - jnp.concatenate([m_new] * nrep, axis=1))
            psum = p[:, 0:LANES]
            for r in range(1, nrep):
                psum = psum + p[:, r * LANES:(r + 1) * LANES]
            l_ref[rs, :] = a * l_ref[rs, :] + psum
            m_ref[rs, :] = m_new
            a_ref[rs, :] = a
            p_ref[rs, :] = p.astype(BF16)
        vb = v_ref[0, pl.ds(pl.multiple_of(j * tq, tq), tq), :]
        acc_ref[...] = a_ref[...] * acc_ref[...] + jnp.dot(p_ref[...], vb, preferred_element_type=F32)

    scores(0, sa_ref)

    def body(t, carry):
        scores(2 * t + 1, sb_ref)
        step(2 * t, sa_ref, False)
        scores(2 * t + 2, sa_ref)
        step(2 * t + 1, sb_ref, False)
        return carry

    lax.fori_loop(0, qi // 2, body, 0)

    @pl.when(qi % 2 == 0)
    def _():
        step(qi, sa_ref, True)

    @pl.when(qi % 2 == 1)
    def _():
        scores(qi, sb_ref)
        step(qi - 1, sa_ref, False)
        step(qi, sb_ref, True)

    o_all = acc_ref[...] / jnp.sum(l_ref[...], axis=-1, keepdims=True)
    o = o_all[:tq] - lam_ref[0] * o_all[tq:]
    ms = jnp.mean(o * o, axis=-1, keepdims=True)
    o_ref[0] = (o * lax.rsqrt(ms + NORM_EPS) * w_ref[...] * out_scale).astype(BF16)


def _resid_ln_kernel(alpha, x_ref, g_ref, o_in_ref, w_ref, lng_ref, lnb_ref, o_ref):
    y = jnp.dot(o_in_ref[0], w_ref[...], preferred_element_type=F32)
    r = alpha * x_ref[0] + g_ref[0] * y
    o_ref[0] = _layer_norm(r, lng_ref[...], lnb_ref[...])


def _attn_layer(x, sh, sc, g, w_in, w_out, lam_params, subln_w, lambda_init, ln_g, ln_b, alpha):
    bsz, s, d = x.shape
    nh = d // HEAD
    ts = min(256, s)
    tq = min(512, s)
    half = ATTN_D // 2
    inv_freq = ROPE_THETA ** (-jnp.arange(half, dtype=F32) / half)
    ang = jnp.arange(s, dtype=F32)[:, None] * inv_freq[None, :]
    cos_t = jnp.tile(jnp.cos(ang), (1, LANES // half))
    sin_h = jnp.sin(ang)
    sin_t = jnp.tile(jnp.concatenate([-sin_h, sin_h], axis=1), (1, LANES // ATTN_D))

    vec = pl.BlockSpec((1, 1, d), lambda b, i: (b, 0, 0))
    tile = pl.BlockSpec((1, ts, d), lambda b, i: (b, i, 0))
    rope = pl.BlockSpec((ts, LANES), lambda b, i: (i, 0))
    q, k, v = pl.pallas_call(
        _attn_in_kernel,
        grid=(bsz, s // ts),
        in_specs=[tile, vec, vec, _const_spec((d, 3 * d)), rope, rope],
        out_specs=[tile, tile, tile],
        out_shape=[jax.ShapeDtypeStruct((bsz, s, d), BF16)] * 3,
        compiler_params=_cparams(("arbitrary", "arbitrary")),
        name="attn_in",
    )(x, sh, sc, w_in.astype(BF16), cos_t, sin_t)

    lp = lam_params.astype(F32)
    lam = (jnp.exp(jnp.sum(lp[0] * lp[1])) - jnp.exp(jnp.sum(lp[2] * lp[3])) + lambda_init).reshape(1)
    qspec = pl.BlockSpec((1, tq, HEAD), lambda b, h, i: (b, i, h))
    kvspec = pl.BlockSpec((1, s, HEAD), lambda b, h, i: (b, 0, h))
    o = pl.pallas_call(
        functools.partial(_flash_kernel, 1.0 - lambda_init),
        grid=(bsz, nh, s // tq),
        in_specs=[pl.BlockSpec(memory_space=pltpu.SMEM), qspec, kvspec, kvspec,
                  pl.BlockSpec((1, HEAD), lambda b, h, i: (0, 0))],
        out_specs=qspec,
        out_shape=jax.ShapeDtypeStruct((bsz, s, d), BF16),
        scratch_shapes=[pltpu.VMEM((2 * tq, LANES), F32), pltpu.VMEM((2 * tq, LANES), F32),
                        pltpu.VMEM((2 * tq, LANES), F32), pltpu.VMEM((2 * tq, HEAD), F32),
                        pltpu.VMEM((2 * tq, tq), F32), pltpu.VMEM((2 * tq, tq), F32),
                        pltpu.VMEM((2 * tq, tq), BF16), pltpu.VMEM((2 * tq, HEAD), BF16)],
        compiler_params=_cparams(("arbitrary", "arbitrary", "arbitrary")),
        name="diff_flash",
    )(lam, q, k, v, subln_w.reshape(1, HEAD))

    return pl.pallas_call(
        functools.partial(_resid_ln_kernel, alpha),
        grid=(bsz, s // ts),
        in_specs=[tile, vec, tile, _const_spec((d, d)), _const_spec((1, d)), _const_spec((1, d))],
        out_specs=tile,
        out_shape=jax.ShapeDtypeStruct((bsz, s, d), F32),
        compiler_params=_cparams(("arbitrary", "arbitrary")),
        name="attn_out",
    )(x, g, o, w_out.astype(BF16), ln_g.reshape(1, d), ln_b.reshape(1, d))


def _router_kernel(x_ref, sh_ref, sc_ref, w_ref, bias_ref, hp_ref, mi_ref, mf_ref, cnt_ref, cnt_scr):
    ts, d = x_ref.shape[1], x_ref.shape[2]
    first = (pl.program_id(0) == 0) & (pl.program_id(1) == 0)

    @pl.when(first)
    def _():
        cnt_scr[...] = jnp.zeros_like(cnt_scr)

    x = x_ref[0]
    h = x * (1.0 + sc_ref[0]) + sh_ref[0]
    hp_ref[...] = _pack_bf16_pair(h[:, :d // 2], h[:, d // 2:])

    logits = jnp.dot(h, w_ref[...], precision=HIGHEST, preferred_element_type=F32) + bias_ref[...]
    lane = lax.broadcasted_iota(I32, (ts, LANES), 1)
    neg = -jnp.inf
    big = jnp.int32(LANES)
    is_g = lane < MOE_GROUPS
    gl = jnp.where(is_g, logits, neg)
    gmax = jnp.max(gl, axis=-1, keepdims=True)
    g_idx = jnp.min(jnp.where(gl == gmax, lane, big), axis=-1, keepdims=True)
    g_w = 1.0 / jnp.sum(jnp.exp(gl - gmax), axis=-1, keepdims=True)

    e_lane = lane - MOE_GROUPS
    in_grp = (e_lane >= g_idx * MOE_EPG) & (e_lane < (g_idx + 1) * MOE_EPG)
    el = jnp.where(in_grp, logits, neg)
    l1 = jnp.max(el, axis=-1, keepdims=True)
    i1 = jnp.min(jnp.where(el == l1, lane, big), axis=-1, keepdims=True)
    el2 = jnp.where(lane == i1, neg, el)
    l2 = jnp.max(el2, axis=-1, keepdims=True)
    i2 = jnp.min(jnp.where(el2 == l2, lane, big), axis=-1, keepdims=True)
    t = jnp.exp(l2 - l1)
    w1 = g_w / (1.0 + t)
    w2 = g_w * t / (1.0 + t)

    oh1 = (lane == i1)
    oh2 = (lane == i2)
    both = (oh1 | oh2).astype(BF16)
    tri = (lax.broadcasted_iota(I32, (ts, ts), 0) > lax.broadcasted_iota(I32, (ts, ts), 1)).astype(BF16)
    before = jnp.dot(tri, both, preferred_element_type=F32) + cnt_scr[...]
    r1 = jnp.sum(jnp.where(oh1, before, 0.0), axis=-1, keepdims=True)
    r2 = jnp.sum(jnp.where(oh2, before, 0.0), axis=-1, keepdims=True)
    cnt_scr[...] = cnt_scr[...] + jnp.sum(both.astype(F32), axis=0, keepdims=True)
    cnt_ref[...] = cnt_scr[...]

    e1 = i1 - MOE_GROUPS
    e2 = i2 - MOE_GROUPS
    mi_ref[...] = jnp.where(lane == 0, e1, jnp.where(lane == 1, e2, jnp.where(
        lane == 2, r1.astype(I32), jnp.where(lane == 3, r2.astype(I32), 0))))
    mf_ref[...] = jnp.where(lane == 0, w1, jnp.where(lane == 1, w2, 0.0))


def _dispatch_kernel(d1_hbm, d2_hbm, hp_ref, xb_in, xb_out, d1_s, d2_s, isem, rsem):
    del xb_in
    i = pl.program_id(0)
    ts = hp_ref.shape[0]
    c1 = pltpu.make_async_copy(d1_hbm.at[i], d1_s, isem.at[0])
    c2 = pltpu.make_async_copy(d2_hbm.at[i], d2_s, isem.at[1])
    c1.start()
    c2.start()
    c1.wait()
    c2.wait()

    def row_copy(t, dst):
        return pltpu.make_async_copy(hp_ref.at[pl.ds(t, 1), :], xb_out.at[pl.ds(dst, 1), :], rsem)

    def issue(t, carry):
        row_copy(t, d1_s[t]).start()
        row_copy(t, d2_s[t]).start()
        return carry

    lax.fori_loop(0, ts, issue, 0)

    def drain(t, carry):
        row_copy(t, 0).wait()
        row_copy(t, 0).wait()
        return carry

    lax.fori_loop(0, ts, drain, 0)


def _expert_kernel(be_ref, nu_ref, xb_ref, w1_ref, w3_ref, w2_ref, y_ref):
    del be_ref
    half = xb_ref.shape[1]

    @pl.when(pl.program_id(0) < nu_ref[0])
    def _():
        lo, hi = _unpack_bf16_pair(xb_ref[...])
        lo = lo.astype(BF16)
        hi = hi.astype(BF16)
        a = (jnp.dot(lo, w1_ref[0, :half, :], preferred_element_type=F32)
             + jnp.dot(hi, w1_ref[0, half:, :], preferred_element_type=F32))
        b = (jnp.dot(lo, w3_ref[0, :half, :], preferred_element_type=F32)
             + jnp.dot(hi, w3_ref[0, half:, :], preferred_element_type=F32))
        hid = (_silu(a) * b).astype(BF16)
        y = jnp.dot(hid, w2_ref[0], preferred_element_type=F32)
        y_ref[...] = _pack_bf16_pair(y[:, :half], y[:, half:])

    @pl.when(pl.program_id(0) >= nu_ref[0])
    def _():
        y_ref[...] = jnp.zeros_like(y_ref)


def _combine_kernel(alpha, d1_hbm, d2_hbm, x_ref, g_ref, mf_ref, lng_ref, lnb_ref, yb_hbm, o_ref,
                    d1_s, d2_s, y1_s, y2_s, isem, rsem):
    ts, d = x_ref.shape[1], x_ref.shape[2]
    half = d // 2
    i = pl.program_id(0) * pl.num_programs(1) + pl.program_id(1)
    c1 = pltpu.make_async_copy(d1_hbm.at[i], d1_s, isem.at[0])
    c2 = pltpu.make_async_copy(d2_hbm.at[i], d2_s, isem.at[1])
    c1.start()
    c2.start()
    c1.wait()
    c2.wait()

    def row_copy(src, t, buf):
        return pltpu.make_async_copy(yb_hbm.at[pl.ds(src, 1), :], buf.at[pl.ds(t, 1), :], rsem)

    def issue(t, carry):
        row_copy(d1_s[t], t, y1_s).start()
        row_copy(d2_s[t], t, y2_s).start()
        return carry

    lax.fori_loop(0, ts, issue, 0)

    def drain(t, carry):
        row_copy(0, t, y1_s).wait()
        row_copy(0, t, y2_s).wait()
        return carry

    lax.fori_loop(0, ts, drain, 0)

    w1 = mf_ref[:, 0:1]
    w2 = mf_ref[:, 1:2]
    lo1, hi1 = _unpack_bf16_pair(y1_s[...])
    lo2, hi2 = _unpack_bf16_pair(y2_s[...])
    x = x_ref[0]
    g = g_ref[0]
    r_lo = alpha * x[:, :half] + g[:, :half] * (w1 * lo1 + w2 * lo2)
    r_hi = alpha * x[:, half:] + g[:, half:] * (w1 * hi1 + w2 * hi2)
    mu = (jnp.sum(r_lo, axis=-1, keepdims=True) + jnp.sum(r_hi, axis=-1, keepdims=True)) / d
    d_lo = r_lo - mu
    d_hi = r_hi - mu
    var = (jnp.sum(d_lo * d_lo, axis=-1, keepdims=True) + jnp.sum(d_hi * d_hi, axis=-1, keepdims=True)) / d
    inv = lax.rsqrt(var + NORM_EPS)
    o_ref[0, :, :half] = d_lo * inv * lng_ref[:, :half] + lnb_ref[:, :half]
    o_ref[0, :, half:] = d_hi * inv * lng_ref[:, half:] + lnb_ref[:, half:]


def _moe_layer(x, sh, sc, g, wg, bg, we, be, w1, w3, w2, ln_g, ln_b, alpha):
    bsz, s, d = x.shape
    n = bsz * s
    half = d // 2
    e_num, blk = MOE_EXPERTS, MOE_BLOCK
    ff = w1.shape[-1]
    ts = min(512, s)
    nt_b = s // ts
    nt = n // ts
    n_rows = n * 2 + e_num * blk
    nb = n_rows // blk

    wcat = jnp.zeros((d, LANES), F32).at[:, :MOE_GROUPS].set(wg).at[:, MOE_GROUPS:MOE_GROUPS + e_num].set(we)
    bcat = jnp.zeros((1, LANES), F32).at[0, :MOE_GROUPS].set(bg).at[0, MOE_GROUPS:MOE_GROUPS + e_num].set(be)

    vec = pl.BlockSpec((1, 1, d), lambda b, i: (b, 0, 0))
    tile = pl.BlockSpec((1, ts, d), lambda b, i: (b, i, 0))
    flat = lambda w: pl.BlockSpec((ts, w), lambda b, i: (b * nt_b + i, 0))
    hp, mi, mf, cnt = pl.pallas_call(
        _router_kernel,
        grid=(bsz, nt_b),
        in_specs=[tile, vec, vec, _const_spec((d, LANES)), _const_spec((1, LANES))],
        out_specs=[flat(half), flat(LANES), flat(LANES), _const_spec((1, LANES))],
        out_shape=[jax.ShapeDtypeStruct((n, half), U32), jax.ShapeDtypeStruct((n, LANES), I32),
                   jax.ShapeDtypeStruct((n, LANES), F32), jax.ShapeDtypeStruct((1, LANES), F32)],
        scratch_shapes=[pltpu.VMEM((1, LANES), F32)],
        compiler_params=_cparams(("arbitrary", "arbitrary")),
        name="moe_router",
    )(x, sh, sc, wcat, bcat)

    counts = cnt[0, MOE_GROUPS:MOE_GROUPS + e_num].astype(I32)
    padded = ((counts + blk - 1) // blk) * blk
    pad_end = jnp.cumsum(padded)
    pad_start = pad_end - padded
    dest1 = (pad_start[mi[:, 0]] + mi[:, 2]).reshape(nt, ts)
    dest2 = (pad_start[mi[:, 1]] + mi[:, 3]).reshape(nt, ts)
    blk_start = jnp.arange(nb, dtype=I32) * blk
    blk_expert = jnp.minimum(jnp.sum((pad_end[None, :] <= blk_start[:, None]).astype(I32), axis=1), e_num - 1)
    n_used = (pad_end[-1] // blk).astype(I32).reshape(1)

    any_spec = pl.BlockSpec(memory_space=pl.ANY)
    xbuf = pl.pallas_call(
        _dispatch_kernel,
        grid=(nt,),
        in_specs=[any_spec, any_spec, pl.BlockSpec((ts, half), lambda i: (i, 0)), any_spec],
        out_specs=any_spec,
        out_shape=jax.ShapeDtypeStruct((n_rows, half), U32),
        input_output_aliases={3: 0},
        scratch_shapes=[pltpu.SMEM((ts,), I32), pltpu.SMEM((ts,), I32),
                        pltpu.SemaphoreType.DMA((2,)), pltpu.SemaphoreType.DMA],
        compiler_params=_cparams(("arbitrary",)),
        name="moe_dispatch",
    )(dest1, dest2, hp, jnp.zeros((n_rows, half), U32))

    ybuf = pl.pallas_call(
        _expert_kernel,
        grid_spec=pltpu.PrefetchScalarGridSpec(
            num_scalar_prefetch=2,
            grid=(nb,),
            in_specs=[pl.BlockSpec((blk, half), lambda i, be_r, nu_r: (i, 0)),
                      pl.BlockSpec((1, d, ff), lambda i, be_r, nu_r: (be_r[i], 0, 0)),
                      pl.BlockSpec((1, d, ff), lambda i, be_r, nu_r: (be_r[i], 0, 0)),
                      pl.BlockSpec((1, ff, d), lambda i, be_r, nu_r: (be_r[i], 0, 0))],
            out_specs=pl.BlockSpec((blk, half), lambda i, be_r, nu_r: (i, 0)),
        ),
        out_shape=jax.ShapeDtypeStruct((n_rows, half), U32),
        compiler_params=_cparams(("arbitrary",)),
        name="moe_experts",
    )(blk_expert, n_used, xbuf, w1.astype(BF16), w3.astype(BF16), w2.astype(BF16))

    return pl.pallas_call(
        functools.partial(_combine_kernel, alpha),
        grid=(bsz, nt_b),
        in_specs=[any_spec, any_spec, tile, vec, flat(LANES),
                  _const_spec((1, d)), _const_spec((1, d)), any_spec],
        out_specs=tile,
        out_shape=jax.ShapeDtypeStruct((bsz, s, d), F32),
        scratch_shapes=[pltpu.SMEM((ts,), I32), pltpu.SMEM((ts,), I32),
                        pltpu.VMEM((ts, half), U32), pltpu.VMEM((ts, half), U32),
                        pltpu.SemaphoreType.DMA((2,)), pltpu.SemaphoreType.DMA],
        compiler_params=_cparams(("arbitrary", "arbitrary")),
        name="moe_combine",
    )(dest1, dest2, x, g, mf, ln_g.reshape(1, d), ln_b.reshape(1, d), ybuf)


def kernel(x, c, ada_w, ada_b, ln_g, ln_b, hgrn_w_in, hgrn_w_out, hgrn_lb, hgrn_norm_w, attn_w_in, attn_w_out,
           attn_lambda, attn_subln_w, router_g_w, router_g_b, router_e_w, router_e_b, moe_w1, moe_w3, moe_w2):
    depth = ada_w.shape[0]
    bsz, s, d = x.shape
    alpha = (2 * depth) ** 0.25
    lb_all = jnp.cumsum(jax.nn.softmax(hgrn_lb.astype(F32), axis=0), axis=0)
    lb_all = lb_all - lb_all[0:1]
    mod = _ada_mod(c, ada_w, ada_b).reshape(depth, bsz, 6, 1, d)
    for i in range(depth):
        sh1, sc1, g1, sh2, sc2, g2 = (mod[i, :, m] for m in range(6))
        j = i // 2
        if i % 2 == 0:
            x = _hgrn_layer(x, sh1, sc1, g1, hgrn_w_in[j], hgrn_w_out[j], lb_all[j], hgrn_norm_w[j],
                            ln_g[i, 0], ln_b[i, 0], alpha)
        else:
            lambda_init = 0.8 - 0.6 * math.exp(-0.3 * i)
            x = _attn_layer(x, sh1, sc1, g1, attn_w_in[j], attn_w_out[j], attn_lambda[j], attn_subln_w[j],
                            lambda_init, ln_g[i, 0], ln_b[i, 0], alpha)
        x = _moe_layer(x, sh2, sc2, g2, router_g_w[i], router_g_b[i], router_e_w[i], router_e_b[i],
                       moe_w1[i], moe_w3[i], moe_w2[i], ln_g[i, 1], ln_b[i, 1], alpha)
    return x
```

```python
import functools
import math

import jax
import jax.numpy as jnp
from jax import lax
from jax.experimental import pallas as pl
from jax.experimental.pallas import tpu as pltpu
from jax.experimental.pallas import tpu_sc as plsc

F32 = jnp.float32
BF16 = jnp.bfloat16
I32 = jnp.int32
U32 = jnp.uint32
HIGHEST = lax.Precision.HIGHEST

LANES = 128
HEAD = 128
HGRN_CHUNK = 32
ATTN_D = 64
FLASH_ROW_CHUNK = 32
ROPE_THETA = 10000.0
MOE_GROUPS = 4
MOE_EPG = 8
MOE_EXPERTS = MOE_GROUPS * MOE_EPG
MOE_BLOCK = 256
SC_WINDOW = 128
SC_SPLIT = 2
NORM_EPS = 1e-5
VMEM_LIMIT = 56 * 1024 * 1024

NT_DIMS = (((1,), (1,)), ((), ()))
TN_DIMS = (((0,), (0,)), ((), ()))


def _cparams(sem):
    return pltpu.CompilerParams(dimension_semantics=sem, vmem_limit_bytes=VMEM_LIMIT)


def _const_spec(shape):
    nd = len(shape)
    return pl.BlockSpec(shape, lambda *_: (0,) * nd)


def _layer_norm(r, g, b):
    mu = jnp.mean(r, axis=-1, keepdims=True)
    d = r - mu
    var = jnp.mean(d * d, axis=-1, keepdims=True)
    return d * lax.rsqrt(var + NORM_EPS) * g + b


def _silu(x):
    return x * (1.0 / (1.0 + jnp.exp(-x)))


def _pack_bf16_pair(lo, hi):
    lo_b = lax.bitcast_convert_type(lo.astype(BF16).astype(F32), U32)
    hi_b = lax.bitcast_convert_type(hi.astype(BF16).astype(F32), U32)
    return (hi_b & jnp.uint32(0xFFFF0000)) | (lo_b >> 16)


def _unpack_bf16_pair(u):
    lo = lax.bitcast_convert_type(u << 16, F32)
    hi = lax.bitcast_convert_type(u & jnp.uint32(0xFFFF0000), F32)
    return lo, hi


def _ada_kernel(c_ref, w_ref, b_ref, o_ref):
    c = c_ref[...]
    o_ref[0] = jnp.dot(_silu(c), w_ref[0], precision=HIGHEST, preferred_element_type=F32) + b_ref[0]


def _ada_mod(c, ada_w, ada_b):
    depth, d, n6 = ada_w.shape
    bsz = c.shape[0]
    tn = d
    return pl.pallas_call(
        _ada_kernel,
        grid=(depth, n6 // tn),
        in_specs=[
            pl.BlockSpec((bsz, d), lambda i, j: (0, 0)),
            pl.BlockSpec((1, d, tn), lambda i, j: (i, 0, j)),
            pl.BlockSpec((1, 1, tn), lambda i, j: (i, 0, j)),
        ],
        out_specs=pl.BlockSpec((1, bsz, tn), lambda i, j: (i, 0, j)),
        out_shape=jax.ShapeDtypeStruct((depth, bsz, n6), F32),
        compiler_params=_cparams(("arbitrary", "arbitrary")),
        name="ada_mod",
    )(c, ada_w, ada_b.reshape(depth, 1, n6))


def _hgrn_kernel(alpha, x_ref, sh_ref, sc_ref, g_ref, win_ref, wout_ref, loglb_ref, oml_ref,
                 nw_ref, lng_ref, lnb_ref, o_ref, proj_ref, st_ref, ocat_ref):
    ts, d = x_ref.shape[1], x_ref.shape[2]
    nh = d // HEAD
    c = HGRN_CHUNK
    nc = ts // c

    @pl.when(pl.program_id(1) == 0)
    def _():
        st_ref[...] = jnp.zeros_like(st_ref)

    x = x_ref[0]
    h = x * (1.0 + sc_ref[0]) + sh_ref[0]
    proj_ref[...] = jnp.dot(h.astype(BF16), win_ref[...], preferred_element_type=F32)

    z = proj_ref[:, d:2 * d]
    ls = jnp.minimum(z, 0.0) - jnp.log(1.0 + jnp.exp(-jnp.abs(z)))
    lsn = ls - z
    cc = loglb_ref[...] + lsn
    log_f = jnp.maximum(ls, cc) + jnp.log(1.0 + jnp.exp(-jnp.abs(ls - cc)))
    kk = oml_ref[...] * jnp.exp(lsn)

    pos = lax.broadcasted_iota(I32, (ts, 1), 0) % c
    b = log_f
    step = 1
    while step < c:
        b = b + jnp.where(pos >= step, pltpu.roll(b, step, 0), 0.0)
        step *= 2

    b3 = b.reshape(nc, c, d)
    b_last = b3[:, c - 1:c, :]
    b_mid = b3[:, c // 2 - 1:c // 2, :]
    q3 = proj_ref[:, 0:d].reshape(nc, c, d)
    k3 = kk.reshape(nc, c, d)
    q_inter = (q3 * jnp.exp(b3)).astype(BF16)
    k_state = (k3 * jnp.exp(b_last - b3)).astype(BF16)
    q_intra = (q3 * jnp.exp(b3 - b_mid)).astype(BF16).reshape(ts, d)
    k_intra = (k3 * jnp.exp(b_mid - b3)).astype(BF16).reshape(ts, d)
    dec = jnp.exp(b_last)
    v2 = proj_ref[:, 2 * d:3 * d].astype(BF16)
    v3 = v2.reshape(nc, c, d)

    row = lax.broadcasted_iota(I32, (ts, ts), 0)
    col = lax.broadcasted_iota(I32, (ts, ts), 1)
    keep = (row >= col) & (row // c == col // c)
    for hd in range(nh):
        sl = slice(hd * HEAD, (hd + 1) * HEAD)
        sc = lax.dot_general(q_intra[:, sl], k_intra[:, sl], NT_DIMS, preferred_element_type=F32)
        p = jnp.where(keep, sc, 0.0).astype(BF16)
        ocat_ref[:, sl] = jnp.dot(p, v2[:, sl], preferred_element_type=F32)

    for ci in range(nc):
        for hd in range(nh):
            sl = slice(hd * HEAD, (hd + 1) * HEAD)
            st = st_ref[hd]
            o_inter = lax.dot_general(q_inter[ci, :, sl], st.astype(BF16), NT_DIMS,
                                      preferred_element_type=F32)
            ocat_ref[ci * c:(ci + 1) * c, sl] += o_inter
            upd = lax.dot_general(v3[ci, :, sl], k_state[ci, :, sl], TN_DIMS,
                                  preferred_element_type=F32)
            st_ref[hd] = st * dec[ci, :, sl] + upd

    for hd in range(nh):
        sl = slice(hd * HEAD, (hd + 1) * HEAD)
        oh = ocat_ref[:, sl]
        ms = jnp.mean(oh * oh, axis=-1, keepdims=True)
        gate = proj_ref[:, 3 * d + hd * HEAD:3 * d + (hd + 1) * HEAD]
        ocat_ref[:, sl] = oh * lax.rsqrt(ms + NORM_EPS) * nw_ref[...] * _silu(gate)
    y = jnp.dot(ocat_ref[...].astype(BF16), wout_ref[...], preferred_element_type=F32)
    r = alpha * x + g_ref[0] * y
    o_ref[0] = _layer_norm(r, lng_ref[...], lnb_ref[...])


def _hgrn_layer(x, sh, sc, g, w_in, w_out, lb, norm_w, ln_g, ln_b, alpha):
    bsz, s, d = x.shape
    ts = min(256, s)
    nh = d // HEAD
    vec = pl.BlockSpec((1, 1, d), lambda b, i: (b, 0, 0))
    tile = pl.BlockSpec((1, ts, d), lambda b, i: (b, i, 0))
    return pl.pallas_call(
        functools.partial(_hgrn_kernel, alpha),
        grid=(bsz, s // ts),
        in_specs=[tile, vec, vec, vec,
                  _const_spec((d, 4 * d)), _const_spec((d, d)),
                  _const_spec((1, d)), _const_spec((1, d)), _const_spec((1, HEAD)),
                  _const_spec((1, d)), _const_spec((1, d))],
        out_specs=tile,
        out_shape=jax.ShapeDtypeStruct((bsz, s, d), F32),
        scratch_shapes=[pltpu.VMEM((ts, 4 * d), F32),
                        pltpu.VMEM((nh, HEAD, HEAD), F32),
                        pltpu.VMEM((ts, d), F32)],
        compiler_params=_cparams(("arbitrary", "arbitrary")),
        name="hgrn_layer",
    )(x, sh, sc, g, w_in.astype(BF16), w_out.astype(BF16),
      jnp.log(lb).reshape(1, d), (1.0 - lb).reshape(1, d), norm_w.reshape(1, HEAD),
      ln_g.reshape(1, d), ln_b.reshape(1, d))


def _attn_in_kernel(x_ref, sh_ref, sc_ref, w_ref, cos_ref, sin_ref, q_ref, k_ref, v_ref):
    d = x_ref.shape[2]
    x = x_ref[0]
    h = x * (1.0 + sc_ref[0]) + sh_ref[0]
    qkv = jnp.dot(h.astype(BF16), w_ref[...], preferred_element_type=F32)
    cos = cos_ref[...]
    sin = sin_ref[...]
    lane = lax.broadcasted_iota(I32, (1, LANES), 1)
    first_half = (lane % ATTN_D) < (ATTN_D // 2)
    scale = ATTN_D ** -0.5 * math.log2(math.e)
    for j in range(d // LANES):
        sl = slice(j * LANES, (j + 1) * LANES)
        for src, dst, mul in ((0, q_ref, scale), (d, k_ref, 1.0)):
            t = qkv[:, src + j * LANES:src + (j + 1) * LANES]
            partner = jnp.where(first_half, pltpu.roll(t, LANES - ATTN_D // 2, 1),
                                pltpu.roll(t, ATTN_D // 2, 1))
            dst[0, :, sl] = ((t * cos + partner * sin) * mul).astype(BF16)
    v_ref[0] = qkv[:, 2 * d:].astype(BF16)


def _flash_kernel(out_scale, lam_ref, q_ref, k_ref, v_ref, w_ref, o_ref,
                  m_ref, l_ref, a_ref, acc_ref, sa_ref, sb_ref, p_ref, qq_ref):
    tq = q_ref.shape[1]
    rows = 2 * tq
    qi = pl.program_id(2)
    q = q_ref[0]
    lane = lax.broadcasted_iota(I32, (1, LANES), 1)
    zero = jnp.zeros_like(q)
    qq_ref[0:tq, :] = jnp.where(lane < ATTN_D, q, zero)
    qq_ref[tq:rows, :] = jnp.where(lane >= ATTN_D, q, zero)

    m_ref[...] = jnp.full_like(m_ref, -jnp.inf)
    l_ref[...] = jnp.zeros_like(l_ref)
    acc_ref[...] = jnp.zeros_like(acc_ref)
    nrep = tq // LANES
    rc = FLASH_ROW_CHUNK

    def scores(j, s_ref):
        kb = k_ref[0, pl.ds(pl.multiple_of(j * tq, tq), tq), :]
        s_ref[...] = lax.dot_general(qq_ref[...], kb, NT_DIMS, preferred_element_type=F32)

    def step(j, s_ref, masked):
        for r0 in range(0, rows, rc):
            rs = slice(r0, r0 + rc)
            s = s_ref[rs, :]
            if masked:
                row = lax.broadcasted_iota(I32, (rc, tq), 0) + (r0 % tq)
                col = lax.broadcasted_iota(I32, (rc, tq), 1)
                s = jnp.where(row >= col, s, -jnp.inf)
            m_old = m_ref[rs, :]
            m_new = jnp.maximum(m_old, jnp.max(s, axis=-1, keepdims=True))
            a = jnp.exp2(m_old - m_new)
            p = jnp.exp2(s - jnp.concatenate([m_new] * nrep, axis=1))
            psum = p[:, 0:LANES]
            for r in range(1, nrep):
                psum = psum + p[:, r * LANES:(r + 1) * LANES]
            l_ref[rs, :] = a * l_ref[rs, :] + psum
            m_ref[rs, :] = m_new
            a_ref[rs, :] = a
            p_ref[rs, :] = p.astype(BF16)
        vb = v_ref[0, pl.ds(pl.multiple_of(j * tq, tq), tq), :]
        acc_ref[...] = a_ref[...] * acc_ref[...] + jnp.dot(p_ref[...], vb, preferred_element_type=F32)

    scores(0, sa_ref)

    def body(t, carry):
        scores(2 * t + 1, sb_ref)
        step(2 * t, sa_ref, False)
        scores(2 * t + 2, sa_ref)
        step(2 * t + 1, sb_ref, False)
        return carry

    lax.fori_loop(0, qi // 2, body, 0)

    @pl.when(qi % 2 == 0)
    def _():
        step(qi, sa_ref, True)

    @pl.when(qi % 2 == 1)
    def _():
        scores(qi, sb_ref)
        step(qi - 1, sa_ref, False)
        step(qi, sb_ref, True)

    o_all = acc_ref[...] / jnp.sum(l_ref[...], axis=-1, keepdims=True)
    o = o_all[:tq] - lam_ref[0] * o_all[tq:]
    ms = jnp.mean(o * o, axis=-1, keepdims=True)
    o_ref[0] = (o * lax.rsqrt(ms + NORM_EPS) * w_ref[...] * out_scale).astype(BF16)


def _resid_ln_kernel(alpha, x_ref, g_ref, o_in_ref, w_ref, lng_ref, lnb_ref, o_ref):
    y = jnp.dot(o_in_ref[0], w_ref[...], preferred_element_type=F32)
    r = alpha * x_ref[0] + g_ref[0] * y
    o_ref[0] = _layer_norm(r, lng_ref[...], lnb_ref[...])


def _attn_layer(x, sh, sc, g, w_in, w_out, lam_params, subln_w, lambda_init, ln_g, ln_b, alpha):
    bsz, s, d = x.shape
    nh = d // HEAD
    ts = min(256, s)
    tq = min(512, s)
    half = ATTN_D // 2
    inv_freq = ROPE_THETA ** (-jnp.arange(half, dtype=F32) / half)
    ang = jnp.arange(s, dtype=F32)[:, None] * inv_freq[None, :]
    cos_t = jnp.tile(jnp.cos(ang), (1, LANES // half))
    sin_h = jnp.sin(ang)
    sin_t = jnp.tile(jnp.concatenate([-sin_h, sin_h], axis=1), (1, LANES // ATTN_D))

    vec = pl.BlockSpec((1, 1, d), lambda b, i: (b, 0, 0))
    tile = pl.BlockSpec((1, ts, d), lambda b, i: (b, i, 0))
    rope = pl.BlockSpec((ts, LANES), lambda b, i: (i, 0))
    q, k, v = pl.pallas_call(
        _attn_in_kernel,
        grid=(bsz, s // ts),
        in_specs=[tile, vec, vec, _const_spec((d, 3 * d)), rope, rope],
        out_specs=[tile, tile, tile],
        out_shape=[jax.ShapeDtypeStruct((bsz, s, d), BF16)] * 3,
        compiler_params=_cparams(("arbitrary", "arbitrary")),
        name="attn_in",
    )(x, sh, sc, w_in.astype(BF16), cos_t, sin_t)

    lp = lam_params.astype(F32)
    lam = (jnp.exp(jnp.sum(lp[0] * lp[1])) - jnp.exp(jnp.sum(lp[2] * lp[3])) + lambda_init).reshape(1)
    qspec = pl.BlockSpec((1, tq, HEAD), lambda b, h, i: (b, i, h))
    kvspec = pl.BlockSpec((1, s, HEAD), lambda b, h, i: (b, 0, h))
    o = pl.pallas_call(
        functools.partial(_flash_kernel, 1.0 - lambda_init),
        grid=(bsz, nh, s // tq),
        in_specs=[pl.BlockSpec(memory_space=pltpu.SMEM), qspec, kvspec, kvspec,
                  pl.BlockSpec((1, HEAD), lambda b, h, i: (0, 0))],
        out_specs=qspec,
        out_shape=jax.ShapeDtypeStruct((bsz, s, d), BF16),
        scratch_shapes=[pltpu.VMEM((2 * tq, LANES), F32), pltpu.VMEM((2 * tq, LANES), F32),
                        pltpu.VMEM((2 * tq, LANES), F32), pltpu.VMEM((2 * tq, HEAD), F32),
                        pltpu.VMEM((2 * tq, tq), F32), pltpu.VMEM((2 * tq, tq), F32),
                        pltpu.VMEM((2 * tq, tq), BF16), pltpu.VMEM((2 * tq, HEAD), BF16)],
        compiler_params=_cparams(("arbitrary", "arbitrary", "arbitrary")),
        name="diff_flash",
    )(lam, q, k, v, subln_w.reshape(1, HEAD))

    return pl.pallas_call(
        functools.partial(_resid_ln_kernel, alpha),
        grid=(bsz, s // ts),
        in_specs=[tile, vec, tile, _const_spec((d, d)), _const_spec((1, d)), _const_spec((1, d))],
        out_specs=tile,
        out_shape=jax.ShapeDtypeStruct((bsz, s, d), F32),
        compiler_params=_cparams(("arbitrary", "arbitrary")),
        name="attn_out",
    )(x, g, o, w_out.astype(BF16), ln_g.reshape(1, d), ln_b.reshape(1, d))


def _router_kernel(x_ref, sh_ref, sc_ref, w_ref, bias_ref, hp_ref, mi_ref, mf_ref, cnt_ref, cnt_scr):
    ts, d = x_ref.shape[1], x_ref.shape[2]
    first = (pl.program_id(0) == 0) & (pl.program_id(1) == 0)

    @pl.when(first)
    def _():
        cnt_scr[...] = jnp.zeros_like(cnt_scr)

    x = x_ref[0]
    h = x * (1.0 + sc_ref[0]) + sh_ref[0]
    hp_ref[...] = lax.bitcast_convert_type(_pack_bf16_pair(h[:, :d // 2], h[:, d // 2:]), I32)

    logits = jnp.dot(h, w_ref[...], precision=HIGHEST, preferred_element_type=F32) + bias_ref[...]
    lane = lax.broadcasted_iota(I32, (ts, LANES), 1)
    neg = -jnp.inf
    big = jnp.int32(LANES)
    is_g = lane < MOE_GROUPS
    gl = jnp.where(is_g, logits, neg)
    gmax = jnp.max(gl, axis=-1, keepdims=True)
    g_idx = jnp.min(jnp.where(gl == gmax, lane, big), axis=-1, keepdims=True)
    g_w = 1.0 / jnp.sum(jnp.exp(gl - gmax), axis=-1, keepdims=True)

    e_lane = lane - MOE_GROUPS
    in_grp = (e_lane >= g_idx * MOE_EPG) & (e_lane < (g_idx + 1) * MOE_EPG)
    el = jnp.where(in_grp, logits, neg)
    l1 = jnp.max(el, axis=-1, keepdims=True)
    i1 = jnp.min(jnp.where(el == l1, lane, big), axis=-1, keepdims=True)
    el2 = jnp.where(lane == i1, neg, el)
    l2 = jnp.max(el2, axis=-1, keepdims=True)
    i2 = jnp.min(jnp.where(el2 == l2, lane, big), axis=-1, keepdims=True)
    t = jnp.exp(l2 - l1)
    w1 = g_w / (1.0 + t)
    w2 = g_w * t / (1.0 + t)

    oh1 = (lane == i1)
    oh2 = (lane == i2)
    both = (oh1 | oh2).astype(BF16)
    tri = (lax.broadcasted_iota(I32, (ts, ts), 0) > lax.broadcasted_iota(I32, (ts, ts), 1)).astype(BF16)
    before = jnp.dot(tri, both, preferred_element_type=F32) + cnt_scr[...]
    r1 = jnp.sum(jnp.where(oh1, before, 0.0), axis=-1, keepdims=True)
    r2 = jnp.sum(jnp.where(oh2, before, 0.0), axis=-1, keepdims=True)
    cnt_scr[...] = cnt_scr[...] + jnp.sum(both.astype(F32), axis=0, keepdims=True)
    cnt_ref[...] = cnt_scr[...]

    e1 = i1 - MOE_GROUPS
    e2 = i2 - MOE_GROUPS
    mi_ref[...] = jnp.where(lane == 0, e1, jnp.where(lane == 1, e2, jnp.where(
        lane == 2, r1.astype(I32), jnp.where(lane == 3, r2.astype(I32), 0))))
    mf_ref[...] = jnp.where(lane == 0, w1, jnp.where(lane == 1, w2, 0.0))


def _sc_mesh():
    return plsc.VectorSubcoreMesh(core_axis_name="c", subcore_axis_name="s")


def _sc_split_idx(idx):
    return (idx[:, None] * SC_SPLIT + jnp.arange(SC_SPLIT, dtype=I32)[None, :]).reshape(-1)


def _sc_scatter_rows(src, idx_a, idx_b, n_rows):
    m, w = src.shape
    out = _sc_scatter_parts(src.reshape(m * SC_SPLIT, w // SC_SPLIT), _sc_split_idx(idx_a),
                            _sc_split_idx(idx_b), n_rows * SC_SPLIT)
    return out.reshape(n_rows, w)


def _sc_gather_rows(table, idx):
    r, w = table.shape
    out = _sc_gather_parts(table.reshape(r * SC_SPLIT, w // SC_SPLIT), _sc_split_idx(idx))
    return out.reshape(idx.shape[0], w)


def _sc_scatter_parts(src, idx_a, idx_b, n_rows):
    m, w = src.shape

    @functools.partial(pl.kernel, out_type=jax.ShapeDtypeStruct((n_rows, w), src.dtype),
                       mesh=_sc_mesh(), scratch_types=[], name="moe_sc_scatter")
    def scatter(x_hbm, ia_hbm, ib_hbm, o_hbm):
        def body(x_vmem, ia_vmem, ib_vmem):
            pltpu.sync_copy(x_vmem, o_hbm.at[ia_vmem.at[0]])
            pltpu.sync_copy(x_vmem, o_hbm.at[ib_vmem.at[0]])

        pltpu.emit_pipeline(
            body,
            grid=(m // SC_WINDOW,),
            in_specs=[pl.BlockSpec((SC_WINDOW, w), lambda i: (i, 0)),
                      pl.BlockSpec((1, SC_WINDOW), lambda i: (0, i)),
                      pl.BlockSpec((1, SC_WINDOW), lambda i: (0, i))],
            out_specs=[],
            core_axis_name=("c", "s"),
            dimension_semantics=(pltpu.PARALLEL,),
        )(x_hbm, ia_hbm, ib_hbm)

    return scatter(src, idx_a.reshape(1, m), idx_b.reshape(1, m))


def _sc_gather_parts(table, idx):
    m = idx.shape[0]
    w = table.shape[1]

    @functools.partial(pl.kernel, out_type=jax.ShapeDtypeStruct((m, w), table.dtype),
                       mesh=_sc_mesh(), scratch_types=[], name="moe_sc_gather")
    def gather(t_hbm, i_hbm, o_hbm):
        def body(i_vmem, o_vmem):
            pltpu.sync_copy(t_hbm.at[i_vmem.at[0]], o_vmem)

        pltpu.emit_pipeline(
            body,
            grid=(m // SC_WINDOW,),
            in_specs=[pl.BlockSpec((1, SC_WINDOW), lambda i: (0, i))],
            out_specs=[pl.BlockSpec((SC_WINDOW, w), lambda i: (i, 0))],
            core_axis_name=("c", "s"),
            dimension_semantics=(pltpu.PARALLEL,),
        )(i_hbm, o_hbm)

    return gather(table, idx.reshape(1, m))


def _expert_kernel(be_ref, nv_ref, xb_ref, w1_ref, w3_ref, w2_ref, y_ref):
    del be_ref
    blk, half = xb_ref.shape
    n_valid = nv_ref[pl.program_id(0)]

    @pl.when(n_valid > 0)
    def _():
        valid = lax.broadcasted_iota(I32, (blk, 1), 0) < n_valid
        lo, hi = _unpack_bf16_pair(lax.bitcast_convert_type(xb_ref[...], U32))
        lo = jnp.where(valid, lo, 0.0).astype(BF16)
        hi = jnp.where(valid, hi, 0.0).astype(BF16)
        a = (jnp.dot(lo, w1_ref[0, :half, :], preferred_element_type=F32)
             + jnp.dot(hi, w1_ref[0, half:, :], preferred_element_type=F32))
        b = (jnp.dot(lo, w3_ref[0, :half, :], preferred_element_type=F32)
             + jnp.dot(hi, w3_ref[0, half:, :], preferred_element_type=F32))
        hid = (_silu(a) * b).astype(BF16)
        y = jnp.dot(hid, w2_ref[0], preferred_element_type=F32)
        y_ref[...] = lax.bitcast_convert_type(_pack_bf16_pair(y[:, :half], y[:, half:]), I32)

    @pl.when(n_valid <= 0)
    def _():
        y_ref[...] = jnp.zeros_like(y_ref)


def _combine_kernel(alpha, x_ref, g_ref, mf_ref, lng_ref, lnb_ref, y1_ref, y2_ref, o_ref):
    d = x_ref.shape[2]
    half = d // 2
    w1 = mf_ref[:, 0:1]
    w2 = mf_ref[:, 1:2]
    lo1, hi1 = _unpack_bf16_pair(lax.bitcast_convert_type(y1_ref[...], U32))
    lo2, hi2 = _unpack_bf16_pair(lax.bitcast_convert_type(y2_ref[...], U32))
    x = x_ref[0]
    g = g_ref[0]
    r_lo = alpha * x[:, :half] + g[:, :half] * (w1 * lo1 + w2 * lo2)
    r_hi = alpha * x[:, half:] + g[:, half:] * (w1 * hi1 + w2 * hi2)
    mu = (jnp.sum(r_lo, axis=-1, keepdims=True) + jnp.sum(r_hi, axis=-1, keepdims=True)) / d
    d_lo = r_lo - mu
    d_hi = r_hi - mu
    var = (jnp.sum(d_lo * d_lo, axis=-1, keepdims=True) + jnp.sum(d_hi * d_hi, axis=-1, keepdims=True)) / d
    inv = lax.rsqrt(var + NORM_EPS)
    o_ref[0, :, :half] = d_lo * inv * lng_ref[:, :half] + lnb_ref[:, :half]
    o_ref[0, :, half:] = d_hi * inv * lng_ref[:, half:] + lnb_ref[:, half:]


def _moe_layer(x, sh, sc, g, wg, bg, we, be, w1, w3, w2, ln_g, ln_b, alpha):
    bsz, s, d = x.shape
    n = bsz * s
    half = d // 2
    e_num, blk = MOE_EXPERTS, MOE_BLOCK
    ff = w1.shape[-1]
    ts = min(512, s)
    nt_b = s // ts
    nt = n // ts
    n_rows = n * 2 + e_num * blk
    nb = n_rows // blk

    wcat = jnp.zeros((d, LANES), F32).at[:, :MOE_GROUPS].set(wg).at[:, MOE_GROUPS:MOE_GROUPS + e_num].set(we)
    bcat = jnp.zeros((1, LANES), F32).at[0, :MOE_GROUPS].set(bg).at[0, MOE_GROUPS:MOE_GROUPS + e_num].set(be)

    vec = pl.BlockSpec((1, 1, d), lambda b, i: (b, 0, 0))
    tile = pl.BlockSpec((1, ts, d), lambda b, i: (b, i, 0))
    flat = lambda w: pl.BlockSpec((ts, w), lambda b, i: (b * nt_b + i, 0))
    hp, mi, mf, cnt = pl.pallas_call(
        _router_kernel,
        grid=(bsz, nt_b),
        in_specs=[tile, vec, vec, _const_spec((d, LANES)), _const_spec((1, LANES))],
        out_specs=[flat(half), flat(LANES), flat(LANES), _const_spec((1, LANES))],
        out_shape=[jax.ShapeDtypeStruct((n, half), I32), jax.ShapeDtypeStruct((n, LANES), I32),
                   jax.ShapeDtypeStruct((n, LANES), F32), jax.ShapeDtypeStruct((1, LANES), F32)],
        scratch_shapes=[pltpu.VMEM((1, LANES), F32)],
        compiler_params=_cparams(("arbitrary", "arbitrary")),
        name="moe_router",
    )(x, sh, sc, wcat, bcat)

    counts = cnt[0, MOE_GROUPS:MOE_GROUPS + e_num].astype(I32)
    padded = ((counts + blk - 1) // blk) * blk
    pad_end = jnp.cumsum(padded)
    pad_start = pad_end - padded
    dest1 = pad_start[mi[:, 0]] + mi[:, 2]
    dest2 = pad_start[mi[:, 1]] + mi[:, 3]
    blk_start = jnp.arange(nb, dtype=I32) * blk
    blk_expert = jnp.minimum(jnp.sum((pad_end[None, :] <= blk_start[:, None]).astype(I32), axis=1), e_num - 1)
    blk_valid = jnp.clip(pad_start[blk_expert] + counts[blk_expert] - blk_start, 0, blk).astype(I32)

    xbuf = _sc_scatter_rows(hp, dest1, dest2, n_rows)

    ybuf = pl.pallas_call(
        _expert_kernel,
        grid_spec=pltpu.PrefetchScalarGridSpec(
            num_scalar_prefetch=2,
            grid=(nb,),
            in_specs=[pl.BlockSpec((blk, half), lambda i, be_r, nv_r: (i, 0)),
                      pl.BlockSpec((1, d, ff), lambda i, be_r, nv_r: (be_r[i], 0, 0)),
                      pl.BlockSpec((1, d, ff), lambda i, be_r, nv_r: (be_r[i], 0, 0)),
                      pl.BlockSpec((1, ff, d), lambda i, be_r, nv_r: (be_r[i], 0, 0))],
            out_specs=pl.BlockSpec((blk, half), lambda i, be_r, nv_r: (i, 0)),
        ),
        out_shape=jax.ShapeDtypeStruct((n_rows, half), I32),
        compiler_params=_cparams(("arbitrary",)),
        name="moe_experts",
    )(blk_expert, blk_valid, xbuf, w1.astype(BF16), w3.astype(BF16), w2.astype(BF16))

    y12 = _sc_gather_rows(ybuf, jnp.concatenate([dest1, dest2]))

    return pl.pallas_call(
        functools.partial(_combine_kernel, alpha),
        grid=(bsz, nt_b),
        in_specs=[tile, vec, flat(LANES), _const_spec((1, d)), _const_spec((1, d)),
                  pl.BlockSpec((ts, half), lambda b, i: (b * nt_b + i, 0)),
                  pl.BlockSpec((ts, half), lambda b, i: (nt + b * nt_b + i, 0))],
        out_specs=tile,
        out_shape=jax.ShapeDtypeStruct((bsz, s, d), F32),
        compiler_params=_cparams(("arbitrary", "arbitrary")),
        name="moe_combine",
    )(x, g, mf, ln_g.reshape(1, d), ln_b.reshape(1, d), y12, y12)


def kernel(x, c, ada_w, ada_b, ln_g, ln_b, hgrn_w_in, hgrn_w_out, hgrn_lb, hgrn_norm_w, attn_w_in, attn_w_out,
           attn_lambda, attn_subln_w, router_g_w, router_g_b, router_e_w, router_e_b, moe_w1, moe_w3, moe_w2):
    depth = ada_w.shape[0]
    bsz, s, d = x.shape
    alpha = (2 * depth) ** 0.25
    lb_all = jnp.cumsum(jax.nn.softmax(hgrn_lb.astype(F32), axis=0), axis=0)
    lb_all = lb_all - lb_all[0:1]
    mod = _ada_mod(c, ada_w, ada_b).reshape(depth, bsz, 6, 1, d)
    for i in range(depth):
        sh1, sc1, g1, sh2, sc2, g2 = (mod[i, :, m] for m in range(6))
        j = i // 2
        if i % 2 == 0:
            x = _hgrn_layer(x, sh1, sc1, g1, hgrn_w_in[j], hgrn_w_out[j], lb_all[j], hgrn_norm_w[j],
                            ln_g[i, 0], ln_b[i, 0], alpha)
        else:
            lambda_init = 0.8 - 0.6 * math.exp(-0.3 * i)
            x = _attn_layer(x, sh1, sc1, g1, attn_w_in[j], attn_w_out[j], attn_lambda[j], attn_subln_w[j],
                            lambda_init, ln_g[i, 0], ln_b[i, 0], alpha)
        x = _moe_layer(x, sh2, sc2, g2, router_g_w[i], router_g_b[i], router_e_w[i], router_e_b[i],
                       moe_w1[i], moe_w3[i], moe_w2[i], ln_g[i, 1], ln_b[i, 1], alpha)
    return x
```

```python
import functools
import math

import jax
import jax.numpy as jnp
from jax import lax
from jax.experimental import pallas as pl
from jax.experimental.pallas import tpu as pltpu
from jax.experimental.pallas import tpu_sc as plsc

F32 = jnp.float32
BF16 = jnp.bfloat16
I32 = jnp.int32
U32 = jnp.uint32
HIGHEST = lax.Precision.HIGHEST

LANES = 128
HEAD = 128
HGRN_CHUNK = 32
ATTN_D = 64
FLASH_ROW_CHUNK = 32
ROPE_THETA = 10000.0
MOE_GROUPS = 4
MOE_EPG = 8
MOE_EXPERTS = MOE_GROUPS * MOE_EPG
MOE_BLOCK = 256
SC_WINDOW = 128
NORM_EPS = 1e-5
VMEM_LIMIT = 56 * 1024 * 1024

NT_DIMS = (((1,), (1,)), ((), ()))
TN_DIMS = (((0,), (0,)), ((), ()))


def _cparams(sem):
    return pltpu.CompilerParams(dimension_semantics=sem, vmem_limit_bytes=VMEM_LIMIT)


def _const_spec(shape):
    nd = len(shape)
    return pl.BlockSpec(shape, lambda *_: (0,) * nd)


def _layer_norm(r, g, b):
    mu = jnp.mean(r, axis=-1, keepdims=True)
    d = r - mu
    var = jnp.mean(d * d, axis=-1, keepdims=True)
    return d * lax.rsqrt(var + NORM_EPS) * g + b


def _silu(x):
    return x * (1.0 / (1.0 + jnp.exp(-x)))


def _pack_bf16_pair(lo, hi):
    lo_b = lax.bitcast_convert_type(lo.astype(BF16).astype(F32), U32)
    hi_b = lax.bitcast_convert_type(hi.astype(BF16).astype(F32), U32)
    return (hi_b & jnp.uint32(0xFFFF0000)) | (lo_b >> 16)


def _unpack_bf16_pair(u):
    lo = lax.bitcast_convert_type(u << 16, F32)
    hi = lax.bitcast_convert_type(u & jnp.uint32(0xFFFF0000), F32)
    return lo, hi


def _ada_kernel(c_ref, w_ref, b_ref, o_ref):
    c = c_ref[...]
    o_ref[0] = jnp.dot(_silu(c), w_ref[0], precision=HIGHEST, preferred_element_type=F32) + b_ref[0]


def _ada_mod(c, ada_w, ada_b):
    depth, d, n6 = ada_w.shape
    bsz = c.shape[0]
    tn = d
    return pl.pallas_call(
        _ada_kernel,
        grid=(depth, n6 // tn),
        in_specs=[
            pl.BlockSpec((bsz, d), lambda i, j: (0, 0)),
            pl.BlockSpec((1, d, tn), lambda i, j: (i, 0, j)),
            pl.BlockSpec((1, 1, tn), lambda i, j: (i, 0, j)),
        ],
        out_specs=pl.BlockSpec((1, bsz, tn), lambda i, j: (i, 0, j)),
        out_shape=jax.ShapeDtypeStruct((depth, bsz, n6), F32),
        compiler_params=_cparams(("arbitrary", "arbitrary")),
        name="ada_mod",
    )(c, ada_w, ada_b.reshape(depth, 1, n6))


def _hgrn_kernel(alpha, x_ref, sh_ref, sc_ref, g_ref, win_ref, wout_ref, loglb_ref, oml_ref,
                 nw_ref, lng_ref, lnb_ref, o_ref, proj_ref, st_ref, ocat_ref):
    ts, d = x_ref.shape[1], x_ref.shape[2]
    nh = d // HEAD
    c = HGRN_CHUNK
    nc = ts // c

    @pl.when(pl.program_id(1) == 0)
    def _():
        st_ref[...] = jnp.zeros_like(st_ref)

    x = x_ref[0]
    h = x * (1.0 + sc_ref[0]) + sh_ref[0]
    proj_ref[...] = jnp.dot(h.astype(BF16), win_ref[...], preferred_element_type=F32)

    z = proj_ref[:, d:2 * d]
    ls = jnp.minimum(z, 0.0) - jnp.log(1.0 + jnp.exp(-jnp.abs(z)))
    lsn = ls - z
    cc = loglb_ref[...] + lsn
    log_f = jnp.maximum(ls, cc) + jnp.log(1.0 + jnp.exp(-jnp.abs(ls - cc)))
    kk = oml_ref[...] * jnp.exp(lsn)

    pos = lax.broadcasted_iota(I32, (ts, 1), 0) % c
    b = log_f
    step = 1
    while step < c:
        b = b + jnp.where(pos >= step, pltpu.roll(b, step, 0), 0.0)
        step *= 2

    b3 = b.reshape(nc, c, d)
    b_last = b3[:, c - 1:c, :]
    b_mid = b3[:, c // 2 - 1:c // 2, :]
    q3 = proj_ref[:, 0:d].reshape(nc, c, d)
    k3 = kk.reshape(nc, c, d)
    q_inter = (q3 * jnp.exp(b3)).astype(BF16)
    k_state = (k3 * jnp.exp(b_last - b3)).astype(BF16)
    q_intra = (q3 * jnp.exp(b3 - b_mid)).astype(BF16).reshape(ts, d)
    k_intra = (k3 * jnp.exp(b_mid - b3)).astype(BF16).reshape(ts, d)
    dec = jnp.exp(b_last)
    v2 = proj_ref[:, 2 * d:3 * d].astype(BF16)
    v3 = v2.reshape(nc, c, d)

    row = lax.broadcasted_iota(I32, (ts, ts), 0)
    col = lax.broadcasted_iota(I32, (ts, ts), 1)
    keep = (row >= col) & (row // c == col // c)
    for hd in range(nh):
        sl = slice(hd * HEAD, (hd + 1) * HEAD)
        sc = lax.dot_general(q_intra[:, sl], k_intra[:, sl], NT_DIMS, preferred_element_type=F32)
        p = jnp.where(keep, sc, 0.0).astype(BF16)
        ocat_ref[:, sl] = jnp.dot(p, v2[:, sl], preferred_element_type=F32)

    for ci in range(nc):
        for hd in range(nh):
            sl = slice(hd * HEAD, (hd + 1) * HEAD)
            st = st_ref[hd]
            o_inter = lax.dot_general(q_inter[ci, :, sl], st.astype(BF16), NT_DIMS,
                                      preferred_element_type=F32)
            ocat_ref[ci * c:(ci + 1) * c, sl] += o_inter
            upd = lax.dot_general(v3[ci, :, sl], k_state[ci, :, sl], TN_DIMS,
                                  preferred_element_type=F32)
            st_ref[hd] = st * dec[ci, :, sl] + upd

    for hd in range(nh):
        sl = slice(hd * HEAD, (hd + 1) * HEAD)
        oh = ocat_ref[:, sl]
        ms = jnp.mean(oh * oh, axis=-1, keepdims=True)
        gate = proj_ref[:, 3 * d + hd * HEAD:3 * d + (hd + 1) * HEAD]
        ocat_ref[:, sl] = oh * lax.rsqrt(ms + NORM_EPS) * nw_ref[...] * _silu(gate)
    y = jnp.dot(ocat_ref[...].astype(BF16), wout_ref[...], preferred_element_type=F32)
    r = alpha * x + g_ref[0] * y
    o_ref[0] = _layer_norm(r, lng_ref[...], lnb_ref[...])


def _hgrn_layer(x, sh, sc, g, w_in, w_out, lb, norm_w, ln_g, ln_b, alpha):
    bsz, s, d = x.shape
    ts = min(256, s)
    nh = d // HEAD
    vec = pl.BlockSpec((1, 1, d), lambda b, i: (b, 0, 0))
    tile = pl.BlockSpec((1, ts, d), lambda b, i: (b, i, 0))
    return pl.pallas_call(
        functools.partial(_hgrn_kernel, alpha),
        grid=(bsz, s // ts),
        in_specs=[tile, vec, vec, vec,
                  _const_spec((d, 4 * d)), _const_spec((d, d)),
                  _const_spec((1, d)), _const_spec((1, d)), _const_spec((1, HEAD)),
                  _const_spec((1, d)), _const_spec((1, d))],
        out_specs=tile,
        out_shape=jax.ShapeDtypeStruct((bsz, s, d), F32),
        scratch_shapes=[pltpu.VMEM((ts, 4 * d), F32),
                        pltpu.VMEM((nh, HEAD, HEAD), F32),
                        pltpu.VMEM((ts, d), F32)],
        compiler_params=_cparams(("arbitrary", "arbitrary")),
        name="hgrn_layer",
    )(x, sh, sc, g, w_in.astype(BF16), w_out.astype(BF16),
      jnp.log(lb).reshape(1, d), (1.0 - lb).reshape(1, d), norm_w.reshape(1, HEAD),
      ln_g.reshape(1, d), ln_b.reshape(1, d))


def _attn_in_kernel(x_ref, sh_ref, sc_ref, w_ref, cos_ref, sin_ref, q_ref, k_ref, v_ref):
    d = x_ref.shape[2]
    x = x_ref[0]
    h = x * (1.0 + sc_ref[0]) + sh_ref[0]
    qkv = jnp.dot(h.astype(BF16), w_ref[...], preferred_element_type=F32)
    cos = cos_ref[...]
    sin = sin_ref[...]
    lane = lax.broadcasted_iota(I32, (1, LANES), 1)
    first_half = (lane % ATTN_D) < (ATTN_D // 2)
    scale = ATTN_D ** -0.5 * math.log2(math.e)
    for j in range(d // LANES):
        sl = slice(j * LANES, (j + 1) * LANES)
        for src, dst, mul in ((0, q_ref, scale), (d, k_ref, 1.0)):
            t = qkv[:, src + j * LANES:src + (j + 1) * LANES]
            partner = jnp.where(first_half, pltpu.roll(t, LANES - ATTN_D // 2, 1),
                                pltpu.roll(t, ATTN_D // 2, 1))
            dst[0, :, sl] = ((t * cos + partner * sin) * mul).astype(BF16)
    v_ref[0] = qkv[:, 2 * d:].astype(BF16)


def _flash_kernel(out_scale, lam_ref, q_ref, k_ref, v_ref, w_ref, o_ref,
                  m_ref, l_ref, a_ref, acc_ref, sa_ref, sb_ref, p_ref, qq_ref):
    tq = q_ref.shape[1]
    rows = 2 * tq
    qi = pl.program_id(2)
    q = q_ref[0]
    lane = lax.broadcasted_iota(I32, (1, LANES), 1)
    zero = jnp.zeros_like(q)
    qq_ref[0:tq, :] = jnp.where(lane < ATTN_D, q, zero)
    qq_ref[tq:rows, :] = jnp.where(lane >= ATTN_D, q, zero)

    m_ref[...] = jnp.full_like(m_ref, -jnp.inf)
    l_ref[...] = jnp.zeros_like(l_ref)
    acc_ref[...] = jnp.zeros_like(acc_ref)
    nrep = tq // LANES
    rc = FLASH_ROW_CHUNK

    def scores(j, s_ref):
        kb = k_ref[0, pl.ds(pl.multiple_of(j * tq, tq), tq), :]
        s_ref[...] = lax.dot_general(qq_ref[...], kb, NT_DIMS, preferred_element_type=F32)

    def step(j, s_ref, masked):
        for r0 in range(0, rows, rc):
            rs = slice(r0, r0 + rc)
            s = s_ref[rs, :]
            if masked:
                row = lax.broadcasted_iota(I32, (rc, tq), 0) + (r0 % tq)
                col = lax.broadcasted_iota(I32, (rc, tq), 1)
                s = jnp.where(row >= col, s, -jnp.inf)
            m_old = m_ref[rs, :]
            m_new = jnp.maximum(m_old, jnp.max(s, axis=-1, keepdims=True))
            a = jnp.exp2(m_old - m_new)
            p = jnp.exp2(s - jnp.concatenate([m_new] * nrep, axis=1))
            psum = p[:, 0:LANES]
            for r in range(1, nrep):
                psum = psum + p[:, r * LANES:(r + 1) * LANES]
            l_ref[rs, :] = a * l_ref[rs, :] + psum
            m_ref[rs, :] = m_new
            a_ref[rs, :] = a
            p_ref[rs, :] = p.astype(BF16)
        vb = v_ref[0, pl.ds(pl.multiple_of(j * tq, tq), tq), :]
        acc_ref[...] = a_ref[...] * acc_ref[...] + jnp.dot(p_ref[...], vb, preferred_element_type=F32)

    scores(0, sa_ref)

    def body(t, carry):
        scores(2 * t + 1, sb_ref)
        step(2 * t, sa_ref, False)
        scores(2 * t + 2, sa_ref)
        step(2 * t + 1, sb_ref, False)
        return carry

    lax.fori_loop(0, qi // 2, body, 0)

    @pl.when(qi % 2 == 0)
    def _():
        step(qi, sa_ref, True)

    @pl.when(qi % 2 == 1)
    def _():
        scores(qi, sb_ref)
        step(qi - 1, sa_ref, False)
        step(qi, sb_ref, True)

    o_all = acc_ref[...] / jnp.sum(l_ref[...], axis=-1, keepdims=True)
    o = o_all[:tq] - lam_ref[0] * o_all[tq:]
    ms = jnp.mean(o * o, axis=-1, keepdims=True)
    o_ref[0] = (o * lax.rsqrt(ms + NORM_EPS) * w_ref[...] * out_scale).astype(BF16)


def _resid_ln_kernel(alpha, x_ref, g_ref, o_in_ref, w_ref, lng_ref, lnb_ref, o_ref):
    y = jnp.dot(o_in_ref[0], w_ref[...], preferred_element_type=F32)
    r = alpha * x_ref[0] + g_ref[0] * y
    o_ref[0] = _layer_norm(r, lng_ref[...], lnb_ref[...])


def _attn_layer(x, sh, sc, g, w_in, w_out, lam_params, subln_w, lambda_init, ln_g, ln_b, alpha):
    bsz, s, d = x.shape
    nh = d // HEAD
    ts = min(256, s)
    tq = min(512, s)
    half = ATTN_D // 2
    inv_freq = ROPE_THETA ** (-jnp.arange(half, dtype=F32) / half)
    ang = jnp.arange(s, dtype=F32)[:, None] * inv_freq[None, :]
    cos_t = jnp.tile(jnp.cos(ang), (1, LANES // half))
    sin_h = jnp.sin(ang)
    sin_t = jnp.tile(jnp.concatenate([-sin_h, sin_h], axis=1), (1, LANES // ATTN_D))

    vec = pl.BlockSpec((1, 1, d), lambda b, i: (b, 0, 0))
    tile = pl.BlockSpec((1, ts, d), lambda b, i: (b, i, 0))
    rope = pl.BlockSpec((ts, LANES), lambda b, i: (i, 0))
    q, k, v = pl.pallas_call(
        _attn_in_kernel,
        grid=(bsz, s // ts),
        in_specs=[tile, vec, vec, _const_spec((d, 3 * d)), rope, rope],
        out_specs=[tile, tile, tile],
        out_shape=[jax.ShapeDtypeStruct((bsz, s, d), BF16)] * 3,
        compiler_params=_cparams(("arbitrary", "arbitrary")),
        name="attn_in",
    )(x, sh, sc, w_in.astype(BF16), cos_t, sin_t)

    lp = lam_params.astype(F32)
    lam = (jnp.exp(jnp.sum(lp[0] * lp[1])) - jnp.exp(jnp.sum(lp[2] * lp[3])) + lambda_init).reshape(1)
    qspec = pl.BlockSpec((1, tq, HEAD), lambda b, h, i: (b, i, h))
    kvspec = pl.BlockSpec((1, s, HEAD), lambda b, h, i: (b, 0, h))
    o = pl.pallas_call(
        functools.partial(_flash_kernel, 1.0 - lambda_init),
        grid=(bsz, nh, s // tq),
        in_specs=[pl.BlockSpec(memory_space=pltpu.SMEM), qspec, kvspec, kvspec,
                  pl.BlockSpec((1, HEAD), lambda b, h, i: (0, 0))],
        out_specs=qspec,
        out_shape=jax.ShapeDtypeStruct((bsz, s, d), BF16),
        scratch_shapes=[pltpu.VMEM((2 * tq, LANES), F32), pltpu.VMEM((2 * tq, LANES), F32),
                        pltpu.VMEM((2 * tq, LANES), F32), pltpu.VMEM((2 * tq, HEAD), F32),
                        pltpu.VMEM((2 * tq, tq), F32), pltpu.VMEM((2 * tq, tq), F32),
                        pltpu.VMEM((2 * tq, tq), BF16), pltpu.VMEM((2 * tq, HEAD), BF16)],
        compiler_params=_cparams(("arbitrary", "arbitrary", "arbitrary")),
        name="diff_flash",
    )(lam, q, k, v, subln_w.reshape(1, HEAD))

    return pl.pallas_call(
        functools.partial(_resid_ln_kernel, alpha),
        grid=(bsz, s // ts),
        in_specs=[tile, vec, tile, _const_spec((d, d)), _const_spec((1, d)), _const_spec((1, d))],
        out_specs=tile,
        out_shape=jax.ShapeDtypeStruct((bsz, s, d), F32),
        compiler_params=_cparams(("arbitrary", "arbitrary")),
        name="attn_out",
    )(x, g, o, w_out.astype(BF16), ln_g.reshape(1, d), ln_b.reshape(1, d))


def _pack_row_parts(v):
    q = v.shape[1] // 4
    return tuple(lax.bitcast_convert_type(_pack_bf16_pair(v[:, p * q:(p + 1) * q], v[:, (2 + p) * q:(3 + p) * q]), I32)
                 for p in range(2))


def _unpack_row_parts(part_a, part_b):
    lo_a, hi_a = _unpack_bf16_pair(lax.bitcast_convert_type(part_a, U32))
    lo_b, hi_b = _unpack_bf16_pair(lax.bitcast_convert_type(part_b, U32))
    return lo_a, lo_b, hi_a, hi_b


def _router_kernel(x_ref, sh_ref, sc_ref, w_ref, bias_ref, hpa_ref, hpb_ref, mt_ref, mf_ref, cnt_ref, cnt_scr):
    ts, d = x_ref.shape[1], x_ref.shape[2]
    first = (pl.program_id(0) == 0) & (pl.program_id(1) == 0)

    @pl.when(first)
    def _():
        cnt_scr[...] = jnp.zeros_like(cnt_scr)

    x = x_ref[0]
    h = x * (1.0 + sc_ref[0]) + sh_ref[0]
    hpa_ref[...], hpb_ref[...] = _pack_row_parts(h)

    logits = jnp.dot(h, w_ref[...], precision=HIGHEST, preferred_element_type=F32) + bias_ref[...]
    lane = lax.broadcasted_iota(I32, (ts, LANES), 1)
    neg = -jnp.inf
    big = jnp.int32(LANES)
    is_g = lane < MOE_GROUPS
    gl = jnp.where(is_g, logits, neg)
    gmax = jnp.max(gl, axis=-1, keepdims=True)
    g_idx = jnp.min(jnp.where(gl == gmax, lane, big), axis=-1, keepdims=True)
    g_w = 1.0 / jnp.sum(jnp.exp(gl - gmax), axis=-1, keepdims=True)

    e_lane = lane - MOE_GROUPS
    in_grp = (e_lane >= g_idx * MOE_EPG) & (e_lane < (g_idx + 1) * MOE_EPG)
    el = jnp.where(in_grp, logits, neg)
    l1 = jnp.max(el, axis=-1, keepdims=True)
    i1 = jnp.min(jnp.where(el == l1, lane, big), axis=-1, keepdims=True)
    el2 = jnp.where(lane == i1, neg, el)
    l2 = jnp.max(el2, axis=-1, keepdims=True)
    i2 = jnp.min(jnp.where(el2 == l2, lane, big), axis=-1, keepdims=True)
    t = jnp.exp(l2 - l1)
    w1 = g_w / (1.0 + t)
    w2 = g_w * t / (1.0 + t)

    oh1 = (lane == i1)
    oh2 = (lane == i2)
    both = (oh1 | oh2).astype(BF16)
    tri = (lax.broadcasted_iota(I32, (ts, ts), 0) > lax.broadcasted_iota(I32, (ts, ts), 1)).astype(BF16)
    before = jnp.dot(tri, both, preferred_element_type=F32) + cnt_scr[...]
    r1 = jnp.sum(jnp.where(oh1, before, 0.0), axis=-1, keepdims=True)
    r2 = jnp.sum(jnp.where(oh2, before, 0.0), axis=-1, keepdims=True)
    cnt_scr[...] = cnt_scr[...] + jnp.sum(both.astype(F32), axis=0, keepdims=True)
    cnt_ref[...] = cnt_scr[...]

    e1 = i1 - MOE_GROUPS
    e2 = i2 - MOE_GROUPS
    meta = jnp.where(lane == 0, e1, jnp.where(lane == 1, e2, jnp.where(
        lane == 2, r1.astype(I32), jnp.where(lane == 3, r2.astype(I32), 0))))
    mt_ref[...] = jnp.transpose(meta)[0:8, :]
    mf_ref[...] = jnp.where(lane == 0, w1, jnp.where(lane == 1, w2, 0.0))


def _dest_kernel(ps_ref, mt_ref, d1_ref, d2_ref):
    e1 = mt_ref[0:1, :]
    e2 = mt_ref[1:2, :]
    p1 = jnp.zeros_like(e1)
    p2 = jnp.zeros_like(e2)
    for e in range(MOE_EXPERTS):
        p1 = jnp.where(e1 == e, ps_ref[e], p1)
        p2 = jnp.where(e2 == e, ps_ref[e], p2)
    d1_ref[...] = p1 + mt_ref[2:3, :]
    d2_ref[...] = p2 + mt_ref[3:4, :]


def _sc_mesh():
    return plsc.VectorSubcoreMesh(core_axis_name="c", subcore_axis_name="s")


def _sc_scatter_rows(src, idx_a, idx_b, n_rows):
    m, w = src.shape

    @functools.partial(pl.kernel, out_type=jax.ShapeDtypeStruct((n_rows, w), src.dtype),
                       mesh=_sc_mesh(), scratch_types=[], name="moe_sc_scatter")
    def scatter(x_hbm, ia_hbm, ib_hbm, o_hbm):
        def body(x_vmem, ia_vmem, ib_vmem):
            pltpu.sync_copy(x_vmem, o_hbm.at[ia_vmem.at[0]])
            pltpu.sync_copy(x_vmem, o_hbm.at[ib_vmem.at[0]])

        pltpu.emit_pipeline(
            body,
            grid=(m // SC_WINDOW,),
            in_specs=[pl.BlockSpec((SC_WINDOW, w), lambda i: (i, 0)),
                      pl.BlockSpec((1, SC_WINDOW), lambda i: (0, i)),
                      pl.BlockSpec((1, SC_WINDOW), lambda i: (0, i))],
            out_specs=[],
            core_axis_name=("c", "s"),
            dimension_semantics=(pltpu.PARALLEL,),
        )(x_hbm, ia_hbm, ib_hbm)

    return scatter(src, idx_a, idx_b)


def _sc_gather_rows(table, idx):
    m = idx.shape[1]
    w = table.shape[1]

    @functools.partial(pl.kernel, out_type=jax.ShapeDtypeStruct((m, w), table.dtype),
                       mesh=_sc_mesh(), scratch_types=[], name="moe_sc_gather")
    def gather(t_hbm, i_hbm, o_hbm):
        def body(i_vmem, o_vmem):
            pltpu.sync_copy(t_hbm.at[i_vmem.at[0]], o_vmem)

        pltpu.emit_pipeline(
            body,
            grid=(m // SC_WINDOW,),
            in_specs=[pl.BlockSpec((1, SC_WINDOW), lambda i: (0, i))],
            out_specs=[pl.BlockSpec((SC_WINDOW, w), lambda i: (i, 0))],
            core_axis_name=("c", "s"),
            dimension_semantics=(pltpu.PARALLEL,),
        )(i_hbm, o_hbm)

    return gather(table, idx)


def _expert_kernel(be_ref, nv_ref, xa_ref, xb_ref, w1_ref, w3_ref, w2_ref, ya_ref, yb_ref,
                   w1_scr, w3_scr, w2_scr):
    i = pl.program_id(0)
    blk = xa_ref.shape[0]
    n_valid = nv_ref[i]
    new_expert = (i == 0) | (be_ref[i] != be_ref[jnp.maximum(i - 1, 0)])

    @pl.when((n_valid > 0) & new_expert)
    def _():
        w1_scr[...] = w1_ref[0].astype(BF16)
        w3_scr[...] = w3_ref[0].astype(BF16)
        w2_scr[...] = w2_ref[0].astype(BF16)

    @pl.when(n_valid > 0)
    def _():
        valid = lax.broadcasted_iota(I32, (blk, 1), 0) < n_valid
        xq = _unpack_row_parts(xa_ref[...], xb_ref[...])
        xin = jnp.concatenate([jnp.where(valid, q, 0.0).astype(BF16) for q in xq], axis=1)
        a = jnp.dot(xin, w1_scr[...], preferred_element_type=F32)
        b = jnp.dot(xin, w3_scr[...], preferred_element_type=F32)
        hid = (_silu(a) * b).astype(BF16)
        y = jnp.dot(hid, w2_scr[...], preferred_element_type=F32)
        ya_ref[...], yb_ref[...] = _pack_row_parts(y)

    @pl.when(n_valid <= 0)
    def _():
        ya_ref[...] = jnp.zeros_like(ya_ref)
        yb_ref[...] = jnp.zeros_like(yb_ref)


def _combine_kernel(alpha, x_ref, g_ref, mf_ref, lng_ref, lnb_ref, y1a_ref, y1b_ref, y2a_ref, y2b_ref, o_ref):
    d = x_ref.shape[2]
    q = d // 4
    w1 = mf_ref[:, 0:1]
    w2 = mf_ref[:, 1:2]
    y1 = _unpack_row_parts(y1a_ref[...], y1b_ref[...])
    y2 = _unpack_row_parts(y2a_ref[...], y2b_ref[...])
    r = [alpha * x_ref[0, :, p * q:(p + 1) * q] + g_ref[0, :, p * q:(p + 1) * q] * (w1 * y1[p] + w2 * y2[p])
         for p in range(4)]
    mu = sum(jnp.sum(rp, axis=-1, keepdims=True) for rp in r) / d
    dev = [rp - mu for rp in r]
    var = sum(jnp.sum(dp * dp, axis=-1, keepdims=True) for dp in dev) / d
    inv = lax.rsqrt(var + NORM_EPS)
    for p in range(4):
        sl = slice(p * q, (p + 1) * q)
        o_ref[0, :, sl] = dev[p] * inv * lng_ref[:, sl] + lnb_ref[:, sl]


def _moe_layer(x, sh, sc, g, wg, bg, we, be, w1, w3, w2, ln_g, ln_b, alpha):
    bsz, s, d = x.shape
    n = bsz * s
    quarter = d // 4
    e_num, blk = MOE_EXPERTS, MOE_BLOCK
    ff = w1.shape[-1]
    ts = min(512, s)
    nt_b = s // ts
    nt = n // ts
    n_rows = n * 2 + e_num * blk
    nb = n_rows // blk

    wcat = jnp.zeros((d, LANES), F32).at[:, :MOE_GROUPS].set(wg).at[:, MOE_GROUPS:MOE_GROUPS + e_num].set(we)
    bcat = jnp.zeros((1, LANES), F32).at[0, :MOE_GROUPS].set(bg).at[0, MOE_GROUPS:MOE_GROUPS + e_num].set(be)

    vec = pl.BlockSpec((1, 1, d), lambda b, i: (b, 0, 0))
    tile = pl.BlockSpec((1, ts, d), lambda b, i: (b, i, 0))
    flat = lambda w: pl.BlockSpec((ts, w), lambda b, i: (b * nt_b + i, 0))
    hpa, hpb, mt, mf, cnt = pl.pallas_call(
        _router_kernel,
        grid=(bsz, nt_b),
        in_specs=[tile, vec, vec, _const_spec((d, LANES)), _const_spec((1, LANES))],
        out_specs=[flat(quarter), flat(quarter), pl.BlockSpec((8, ts), lambda b, i: (0, b * nt_b + i)),
                   flat(LANES), _const_spec((1, LANES))],
        out_shape=[jax.ShapeDtypeStruct((n, quarter), I32), jax.ShapeDtypeStruct((n, quarter), I32),
                   jax.ShapeDtypeStruct((8, n), I32), jax.ShapeDtypeStruct((n, LANES), F32),
                   jax.ShapeDtypeStruct((1, LANES), F32)],
        scratch_shapes=[pltpu.VMEM((1, LANES), F32)],
        compiler_params=_cparams(("arbitrary", "arbitrary")),
        name="moe_router",
    )(x, sh, sc, wcat, bcat)

    counts = cnt[0, MOE_GROUPS:MOE_GROUPS + e_num].astype(I32)
    padded = ((counts + blk - 1) // blk) * blk
    pad_end = jnp.cumsum(padded)
    pad_start = pad_end - padded
    blk_start = jnp.arange(nb, dtype=I32) * blk
    blk_expert = jnp.minimum(jnp.sum((pad_end[None, :] <= blk_start[:, None]).astype(I32), axis=1), e_num - 1)
    blk_valid = jnp.clip(pad_start[blk_expert] + counts[blk_expert] - blk_start, 0, blk).astype(I32)

    td = min(8192, n)
    dest1, dest2 = pl.pallas_call(
        _dest_kernel,
        grid_spec=pltpu.PrefetchScalarGridSpec(
            num_scalar_prefetch=1,
            grid=(n // td,),
            in_specs=[pl.BlockSpec((8, td), lambda i, ps_r: (0, i))],
            out_specs=[pl.BlockSpec((1, td), lambda i, ps_r: (0, i))] * 2,
        ),
        out_shape=[jax.ShapeDtypeStruct((1, n), I32)] * 2,
        compiler_params=_cparams(("arbitrary",)),
        name="moe_dest",
    )(pad_start.astype(I32), mt)

    xa = _sc_scatter_rows(hpa, dest1, dest2, n_rows)
    xb = _sc_scatter_rows(hpb, dest1, dest2, n_rows)

    rows_spec = pl.BlockSpec((blk, quarter), lambda i, be_r, nv_r: (i, 0))
    ya, yb = pl.pallas_call(
        _expert_kernel,
        grid_spec=pltpu.PrefetchScalarGridSpec(
            num_scalar_prefetch=2,
            grid=(nb,),
            in_specs=[rows_spec, rows_spec,
                      pl.BlockSpec((1, d, ff), lambda i, be_r, nv_r: (be_r[i], 0, 0)),
                      pl.BlockSpec((1, d, ff), lambda i, be_r, nv_r: (be_r[i], 0, 0)),
                      pl.BlockSpec((1, ff, d), lambda i, be_r, nv_r: (be_r[i], 0, 0))],
            out_specs=[rows_spec, rows_spec],
            scratch_shapes=[pltpu.VMEM((d, ff), BF16), pltpu.VMEM((d, ff), BF16), pltpu.VMEM((ff, d), BF16)],
        ),
        out_shape=[jax.ShapeDtypeStruct((n_rows, quarter), I32)] * 2,
        compiler_params=_cparams(("arbitrary",)),
        name="moe_experts",
    )(blk_expert, blk_valid, xa, xb, w1, w3, w2)

    dest12 = jnp.concatenate([dest1, dest2], axis=1)
    ga = _sc_gather_rows(ya, dest12)
    gb = _sc_gather_rows(yb, dest12)

    first = pl.BlockSpec((ts, quarter), lambda b, i: (b * nt_b + i, 0))
    second = pl.BlockSpec((ts, quarter), lambda b, i: (nt + b * nt_b + i, 0))
    return pl.pallas_call(
        functools.partial(_combine_kernel, alpha),
        grid=(bsz, nt_b),
        in_specs=[tile, vec, flat(LANES), _const_spec((1, d)), _const_spec((1, d)),
                  first, first, second, second],
        out_specs=tile,
        out_shape=jax.ShapeDtypeStruct((bsz, s, d), F32),
        compiler_params=_cparams(("arbitrary", "arbitrary")),
        name="moe_combine",
    )(x, g, mf, ln_g.reshape(1, d), ln_b.reshape(1, d), ga, gb, ga, gb)


def kernel(x, c, ada_w, ada_b, ln_g, ln_b, hgrn_w_in, hgrn_w_out, hgrn_lb, hgrn_norm_w, attn_w_in, attn_w_out,
           attn_lambda, attn_subln_w, router_g_w, router_g_b, router_e_w, router_e_b, moe_w1, moe_w3, moe_w2):
    depth = ada_w.shape[0]
    bsz, s, d = x.shape
    alpha = (2 * depth) ** 0.25
    lb_all = jnp.cumsum(jax.nn.softmax(hgrn_lb.astype(F32), axis=0), axis=0)
    lb_all = lb_all - lb_all[0:1]
    mod = _ada_mod(c, ada_w, ada_b).reshape(depth, bsz, 6, 1, d)
    for i in range(depth):
        sh1, sc1, g1, sh2, sc2, g2 = (mod[i, :, m] for m in range(6))
        j = i // 2
        if i % 2 == 0:
            x = _hgrn_layer(x, sh1, sc1, g1, hgrn_w_in[j], hgrn_w_out[j], lb_all[j], hgrn_norm_w[j],
                            ln_g[i, 0], ln_b[i, 0], alpha)
        else:
            lambda_init = 0.8 - 0.6 * math.exp(-0.3 * i)
            x = _attn_layer(x, sh1, sc1, g1, attn_w_in[j], attn_w_out[j], attn_lambda[j], attn_subln_w[j],
                            lambda_init, ln_g[i, 0], ln_b[i, 0], alpha)
        x = _moe_layer(x, sh2, sc2, g2, router_g_w[i], router_g_b[i], router_e_w[i], router_e_b[i],
                       moe_w1[i], moe_w3[i], moe_w2[i], ln_g[i, 1], ln_b[i, 1], alpha)
    return x
```

```python
import functools
import math

import jax
import jax.numpy as jnp
from jax import lax
from jax.experimental import pallas as pl
from jax.experimental.pallas import tpu as pltpu
from jax.experimental.pallas import tpu_sc as plsc

F32 = jnp.float32
BF16 = jnp.bfloat16
I32 = jnp.int32
U32 = jnp.uint32
HIGHEST = lax.Precision.HIGHEST

LANES = 128
HEAD = 128
HGRN_CHUNK = 32
ATTN_D = 64
FLASH_ROW_CHUNK = 32
ROPE_THETA = 10000.0
MOE_GROUPS = 4
MOE_EPG = 8
MOE_EXPERTS = MOE_GROUPS * MOE_EPG
MOE_BLOCK = 256
SC_WINDOW = 128
NORM_EPS = 1e-5
VMEM_LIMIT = 56 * 1024 * 1024

NT_DIMS = (((1,), (1,)), ((), ()))
TN_DIMS = (((0,), (0,)), ((), ()))


def _cparams(sem):
    return pltpu.CompilerParams(dimension_semantics=sem, vmem_limit_bytes=VMEM_LIMIT)


def _const_spec(shape):
    nd = len(shape)
    return pl.BlockSpec(shape, lambda *_: (0,) * nd)


def _layer_norm(r, g, b):
    mu = jnp.mean(r, axis=-1, keepdims=True)
    d = r - mu
    var = jnp.mean(d * d, axis=-1, keepdims=True)
    return d * lax.rsqrt(var + NORM_EPS) * g + b


def _silu(x):
    return x * (1.0 / (1.0 + jnp.exp(-x)))


def _pack_bf16_pair(lo, hi):
    lo_b = lax.bitcast_convert_type(lo.astype(BF16).astype(F32), U32)
    hi_b = lax.bitcast_convert_type(hi.astype(BF16).astype(F32), U32)
    return (hi_b & jnp.uint32(0xFFFF0000)) | (lo_b >> 16)


def _unpack_bf16_pair(u):
    lo = lax.bitcast_convert_type(u << 16, F32)
    hi = lax.bitcast_convert_type(u & jnp.uint32(0xFFFF0000), F32)
    return lo, hi


def _ada_kernel(c_ref, w_ref, b_ref, o_ref):
    c = c_ref[...]
    o_ref[0] = jnp.dot(_silu(c), w_ref[0], precision=HIGHEST, preferred_element_type=F32) + b_ref[0]


def _ada_mod(c, ada_w, ada_b):
    depth, d, n6 = ada_w.shape
    bsz = c.shape[0]
    tn = d
    return pl.pallas_call(
        _ada_kernel,
        grid=(depth, n6 // tn),
        in_specs=[
            pl.BlockSpec((bsz, d), lambda i, j: (0, 0)),
            pl.BlockSpec((1, d, tn), lambda i, j: (i, 0, j)),
            pl.BlockSpec((1, 1, tn), lambda i, j: (i, 0, j)),
        ],
        out_specs=pl.BlockSpec((1, bsz, tn), lambda i, j: (i, 0, j)),
        out_shape=jax.ShapeDtypeStruct((depth, bsz, n6), F32),
        compiler_params=_cparams(("arbitrary", "arbitrary")),
        name="ada_mod",
    )(c, ada_w, ada_b.reshape(depth, 1, n6))


def _hgrn_kernel(alpha, x_ref, sh_ref, sc_ref, g_ref, win_ref, wout_ref, loglb_ref, oml_ref,
                 nw_ref, lng_ref, lnb_ref, o_ref, proj_ref, st_ref, ocat_ref):
    ts, d = x_ref.shape[1], x_ref.shape[2]
    nh = d // HEAD
    c = HGRN_CHUNK
    nc = ts // c

    @pl.when(pl.program_id(1) == 0)
    def _():
        st_ref[...] = jnp.zeros_like(st_ref)

    x = x_ref[0]
    h = x * (1.0 + sc_ref[0]) + sh_ref[0]
    proj_ref[...] = jnp.dot(h.astype(BF16), win_ref[...], preferred_element_type=F32)

    z = proj_ref[:, d:2 * d]
    ls = jnp.minimum(z, 0.0) - jnp.log(1.0 + jnp.exp(-jnp.abs(z)))
    lsn = ls - z
    cc = loglb_ref[...] + lsn
    log_f = jnp.maximum(ls, cc) + jnp.log(1.0 + jnp.exp(-jnp.abs(ls - cc)))
    kk = oml_ref[...] * jnp.exp(lsn)

    pos = lax.broadcasted_iota(I32, (ts, 1), 0) % c
    b = log_f
    step = 1
    while step < c:
        b = b + jnp.where(pos >= step, pltpu.roll(b, step, 0), 0.0)
        step *= 2

    b3 = b.reshape(nc, c, d)
    b_last = b3[:, c - 1:c, :]
    b_mid = b3[:, c // 2 - 1:c // 2, :]
    q3 = proj_ref[:, 0:d].reshape(nc, c, d)
    k3 = kk.reshape(nc, c, d)
    q_inter = (q3 * jnp.exp(b3)).astype(BF16)
    k_state = (k3 * jnp.exp(b_last - b3)).astype(BF16)
    q_intra = (q3 * jnp.exp(b3 - b_mid)).astype(BF16).reshape(ts, d)
    k_intra = (k3 * jnp.exp(b_mid - b3)).astype(BF16).reshape(ts, d)
    dec = jnp.exp(b_last)
    v2 = proj_ref[:, 2 * d:3 * d].astype(BF16)
    v3 = v2.reshape(nc, c, d)

    row = lax.broadcasted_iota(I32, (ts, ts), 0)
    col = lax.broadcasted_iota(I32, (ts, ts), 1)
    keep = (row >= col) & (row // c == col // c)
    for hd in range(nh):
        sl = slice(hd * HEAD, (hd + 1) * HEAD)
        sc = lax.dot_general(q_intra[:, sl], k_intra[:, sl], NT_DIMS, preferred_element_type=F32)
        p = jnp.where(keep, sc, 0.0).astype(BF16)
        ocat_ref[:, sl] = jnp.dot(p, v2[:, sl], preferred_element_type=F32)

    for ci in range(nc):
        for hd in range(nh):
            sl = slice(hd * HEAD, (hd + 1) * HEAD)
            st = st_ref[hd]
            o_inter = lax.dot_general(q_inter[ci, :, sl], st.astype(BF16), NT_DIMS,
                                      preferred_element_type=F32)
            ocat_ref[ci * c:(ci + 1) * c, sl] += o_inter
            upd = lax.dot_general(v3[ci, :, sl], k_state[ci, :, sl], TN_DIMS,
                                  preferred_element_type=F32)
            st_ref[hd] = st * dec[ci, :, sl] + upd

    for hd in range(nh):
        sl = slice(hd * HEAD, (hd + 1) * HEAD)
        oh = ocat_ref[:, sl]
        ms = jnp.mean(oh * oh, axis=-1, keepdims=True)
        gate = proj_ref[:, 3 * d + hd * HEAD:3 * d + (hd + 1) * HEAD]
        ocat_ref[:, sl] = oh * lax.rsqrt(ms + NORM_EPS) * nw_ref[...] * _silu(gate)
    y = jnp.dot(ocat_ref[...].astype(BF16), wout_ref[...], preferred_element_type=F32)
    r = alpha * x + g_ref[0] * y
    o_ref[0] = _layer_norm(r, lng_ref[...], lnb_ref[...])


def _hgrn_layer(x, sh, sc, g, w_in, w_out, lb, norm_w, ln_g, ln_b, alpha):
    bsz, s, d = x.shape
    ts = min(256, s)
    nh = d // HEAD
    vec = pl.BlockSpec((1, 1, d), lambda b, i: (b, 0, 0))
    tile = pl.BlockSpec((1, ts, d), lambda b, i: (b, i, 0))
    return pl.pallas_call(
        functools.partial(_hgrn_kernel, alpha),
        grid=(bsz, s // ts),
        in_specs=[tile, vec, vec, vec,
                  _const_spec((d, 4 * d)), _const_spec((d, d)),
                  _const_spec((1, d)), _const_spec((1, d)), _const_spec((1, HEAD)),
                  _const_spec((1, d)), _const_spec((1, d))],
        out_specs=tile,
        out_shape=jax.ShapeDtypeStruct((bsz, s, d), F32),
        scratch_shapes=[pltpu.VMEM((ts, 4 * d), F32),
                        pltpu.VMEM((nh, HEAD, HEAD), F32),
                        pltpu.VMEM((ts, d), F32)],
        compiler_params=_cparams(("arbitrary", "arbitrary")),
        name="hgrn_layer",
    )(x, sh, sc, g, w_in.astype(BF16), w_out.astype(BF16),
      jnp.log(lb).reshape(1, d), (1.0 - lb).reshape(1, d), norm_w.reshape(1, HEAD),
      ln_g.reshape(1, d), ln_b.reshape(1, d))


def _attn_in_kernel(x_ref, sh_ref, sc_ref, w_ref, cos_ref, sin_ref, q_ref, k_ref, v_ref):
    d = x_ref.shape[2]
    x = x_ref[0]
    h = x * (1.0 + sc_ref[0]) + sh_ref[0]
    qkv = jnp.dot(h.astype(BF16), w_ref[...], preferred_element_type=F32)
    cos = cos_ref[...]
    sin = sin_ref[...]
    lane = lax.broadcasted_iota(I32, (1, LANES), 1)
    first_half = (lane % ATTN_D) < (ATTN_D // 2)
    scale = ATTN_D ** -0.5 * math.log2(math.e)
    for j in range(d // LANES):
        sl = slice(j * LANES, (j + 1) * LANES)
        for src, dst, mul in ((0, q_ref, scale), (d, k_ref, 1.0)):
            t = qkv[:, src + j * LANES:src + (j + 1) * LANES]
            partner = jnp.where(first_half, pltpu.roll(t, LANES - ATTN_D // 2, 1),
                                pltpu.roll(t, ATTN_D // 2, 1))
            dst[0, :, sl] = ((t * cos + partner * sin) * mul).astype(BF16)
    v_ref[0] = qkv[:, 2 * d:].astype(BF16)


def _flash_kernel(out_scale, lam_ref, q_ref, k_ref, v_ref, w_ref, o_ref,
                  m_ref, l_ref, acc_ref, aa_ref, ab_ref, sa_ref, sb_ref, pa_ref, pb_ref, qq_ref):
    tq = q_ref.shape[1]
    rows = 2 * tq
    qi = pl.program_id(2)
    q = q_ref[0]
    lane = lax.broadcasted_iota(I32, (1, LANES), 1)
    zero = jnp.zeros_like(q)
    qq_ref[0:tq, :] = jnp.where(lane < ATTN_D, q, zero)
    qq_ref[tq:rows, :] = jnp.where(lane >= ATTN_D, q, zero)

    m_ref[...] = jnp.full_like(m_ref, -jnp.inf)
    l_ref[...] = jnp.zeros_like(l_ref)
    acc_ref[...] = jnp.zeros_like(acc_ref)
    nrep = tq // LANES
    rc = FLASH_ROW_CHUNK

    def scores(j, s_ref):
        kb = k_ref[0, pl.ds(pl.multiple_of(j * tq, tq), tq), :]
        s_ref[...] = lax.dot_general(qq_ref[...], kb, NT_DIMS, preferred_element_type=F32)

    def softmax(s_ref, p_ref, a_ref, masked):
        for r0 in range(0, rows, rc):
            rs = slice(r0, r0 + rc)
            s = s_ref[rs, :]
            if masked:
                row = lax.broadcasted_iota(I32, (rc, tq), 0) + (r0 % tq)
                col = lax.broadcasted_iota(I32, (rc, tq), 1)
                s = jnp.where(row >= col, s, -jnp.inf)
            m_old = m_ref[rs, :]
            m_new = jnp.maximum(m_old, jnp.max(s, axis=-1, keepdims=True))
            a = jnp.exp2(m_old - m_new)
            p = jnp.exp2(s - jnp.concatenate([m_new] * nrep, axis=1))
            psum = p[:, 0:LANES]
            for r in range(1, nrep):
                psum = psum + p[:, r * LANES:(r + 1) * LANES]
            l_ref[rs, :] = a * l_ref[rs, :] + psum
            m_ref[rs, :] = m_new
            a_ref[rs, :] = a
            p_ref[rs, :] = p.astype(BF16)

    def values(j, p_ref, a_ref):
        vb = v_ref[0, pl.ds(pl.multiple_of(j * tq, tq), tq), :]
        acc_ref[...] = a_ref[...] * acc_ref[...] + jnp.dot(p_ref[...], vb, preferred_element_type=F32)

    n_blk = qi + 1
    scores(0, sa_ref)

    @pl.when(n_blk == 1)
    def _():
        softmax(sa_ref, pa_ref, aa_ref, True)
        values(0, pa_ref, aa_ref)

    @pl.when(n_blk >= 2)
    def _():
        scores(1, sb_ref)
        softmax(sa_ref, pa_ref, aa_ref, False)

        def body(t, carry):
            scores(2 * t + 2, sa_ref)
            softmax(sb_ref, pb_ref, ab_ref, False)
            values(2 * t, pa_ref, aa_ref)
            scores(2 * t + 3, sb_ref)
            softmax(sa_ref, pa_ref, aa_ref, False)
            values(2 * t + 1, pb_ref, ab_ref)
            return carry

        n_pairs = (n_blk - 2) // 2
        lax.fori_loop(0, n_pairs, body, 0)
        done = 2 * n_pairs

        @pl.when(n_blk % 2 == 0)
        def _():
            softmax(sb_ref, pb_ref, ab_ref, True)
            values(done, pa_ref, aa_ref)
            values(done + 1, pb_ref, ab_ref)

        @pl.when(n_blk % 2 == 1)
        def _():
            scores(done + 2, sa_ref)
            softmax(sb_ref, pb_ref, ab_ref, False)
            values(done, pa_ref, aa_ref)
            softmax(sa_ref, pa_ref, aa_ref, True)
            values(done + 1, pb_ref, ab_ref)
            values(done + 2, pa_ref, aa_ref)

    o_all = acc_ref[...] / jnp.sum(l_ref[...], axis=-1, keepdims=True)
    o = o_all[:tq] - lam_ref[0] * o_all[tq:]
    ms = jnp.mean(o * o, axis=-1, keepdims=True)
    o_ref[0] = (o * lax.rsqrt(ms + NORM_EPS) * w_ref[...] * out_scale).astype(BF16)


def _resid_ln_kernel(alpha, x_ref, g_ref, o_in_ref, w_ref, lng_ref, lnb_ref, o_ref):
    y = jnp.dot(o_in_ref[0], w_ref[...], preferred_element_type=F32)
    r = alpha * x_ref[0] + g_ref[0] * y
    o_ref[0] = _layer_norm(r, lng_ref[...], lnb_ref[...])


def _attn_layer(x, sh, sc, g, w_in, w_out, lam_params, subln_w, lambda_init, ln_g, ln_b, alpha):
    bsz, s, d = x.shape
    nh = d // HEAD
    ts = min(512, s)
    tq = min(512, s)
    half = ATTN_D // 2
    inv_freq = ROPE_THETA ** (-jnp.arange(half, dtype=F32) / half)
    ang = jnp.arange(s, dtype=F32)[:, None] * inv_freq[None, :]
    cos_t = jnp.tile(jnp.cos(ang), (1, LANES // half))
    sin_h = jnp.sin(ang)
    sin_t = jnp.tile(jnp.concatenate([-sin_h, sin_h], axis=1), (1, LANES // ATTN_D))

    vec = pl.BlockSpec((1, 1, d), lambda b, i: (b, 0, 0))
    tile = pl.BlockSpec((1, ts, d), lambda b, i: (b, i, 0))
    rope = pl.BlockSpec((ts, LANES), lambda b, i: (i, 0))
    q, k, v = pl.pallas_call(
        _attn_in_kernel,
        grid=(bsz, s // ts),
        in_specs=[tile, vec, vec, _const_spec((d, 3 * d)), rope, rope],
        out_specs=[tile, tile, tile],
        out_shape=[jax.ShapeDtypeStruct((bsz, s, d), BF16)] * 3,
        compiler_params=_cparams(("arbitrary", "arbitrary")),
        name="attn_in",
    )(x, sh, sc, w_in.astype(BF16), cos_t, sin_t)

    lp = lam_params.astype(F32)
    lam = (jnp.exp(jnp.sum(lp[0] * lp[1])) - jnp.exp(jnp.sum(lp[2] * lp[3])) + lambda_init).reshape(1)
    qspec = pl.BlockSpec((1, tq, HEAD), lambda b, h, i: (b, i, h))
    kvspec = pl.BlockSpec((1, s, HEAD), lambda b, h, i: (b, 0, h))
    o = pl.pallas_call(
        functools.partial(_flash_kernel, 1.0 - lambda_init),
        grid=(bsz, nh, s // tq),
        in_specs=[pl.BlockSpec(memory_space=pltpu.SMEM), qspec, kvspec, kvspec,
                  pl.BlockSpec((1, HEAD), lambda b, h, i: (0, 0))],
        out_specs=qspec,
        out_shape=jax.ShapeDtypeStruct((bsz, s, d), BF16),
        scratch_shapes=[pltpu.VMEM((2 * tq, LANES), F32), pltpu.VMEM((2 * tq, LANES), F32),
                        pltpu.VMEM((2 * tq, HEAD), F32),
                        pltpu.VMEM((2 * tq, LANES), F32), pltpu.VMEM((2 * tq, LANES), F32),
                        pltpu.VMEM((2 * tq, tq), F32), pltpu.VMEM((2 * tq, tq), F32),
                        pltpu.VMEM((2 * tq, tq), BF16), pltpu.VMEM((2 * tq, tq), BF16),
                        pltpu.VMEM((2 * tq, HEAD), BF16)],
        compiler_params=_cparams(("arbitrary", "arbitrary", "arbitrary")),
        name="diff_flash",
    )(lam, q, k, v, subln_w.reshape(1, HEAD))

    return pl.pallas_call(
        functools.partial(_resid_ln_kernel, alpha),
        grid=(bsz, s // ts),
        in_specs=[tile, vec, tile, _const_spec((d, d)), _const_spec((1, d)), _const_spec((1, d))],
        out_specs=tile,
        out_shape=jax.ShapeDtypeStruct((bsz, s, d), F32),
        compiler_params=_cparams(("arbitrary", "arbitrary")),
        name="attn_out",
    )(x, g, o, w_out.astype(BF16), ln_g.reshape(1, d), ln_b.reshape(1, d))


def _pack_row_parts(v):
    q = v.shape[1] // 4
    return tuple(lax.bitcast_convert_type(_pack_bf16_pair(v[:, p * q:(p + 1) * q], v[:, (2 + p) * q:(3 + p) * q]), I32)
                 for p in range(2))


def _unpack_row_parts(part_a, part_b):
    lo_a, hi_a = _unpack_bf16_pair(lax.bitcast_convert_type(part_a, U32))
    lo_b, hi_b = _unpack_bf16_pair(lax.bitcast_convert_type(part_b, U32))
    return lo_a, lo_b, hi_a, hi_b


def _router_kernel(x_ref, sh_ref, sc_ref, w_ref, bias_ref, tri_ref, hpa_ref, hpb_ref, mt_ref, mf_ref, cnt_ref,
                   cnt_scr):
    ts, d = x_ref.shape[1], x_ref.shape[2]
    first = (pl.program_id(0) == 0) & (pl.program_id(1) == 0)

    @pl.when(first)
    def _():
        cnt_scr[...] = jnp.zeros_like(cnt_scr)

    x = x_ref[0]
    h = x * (1.0 + sc_ref[0]) + sh_ref[0]
    hpa_ref[...], hpb_ref[...] = _pack_row_parts(h)

    h_hi = h.astype(BF16)
    h_lo = (h - h_hi.astype(F32)).astype(BF16)
    hh = jnp.dot(h_hi, w_ref[...], preferred_element_type=F32)
    lh = jnp.dot(h_lo, w_ref[:, 0:LANES], preferred_element_type=F32)
    logits = hh[:, 0:LANES] + hh[:, LANES:2 * LANES] + lh + bias_ref[...]
    lane = lax.broadcasted_iota(I32, (ts, LANES), 1)
    neg = -jnp.inf
    big = jnp.int32(LANES)
    is_g = lane < MOE_GROUPS
    gl = jnp.where(is_g, logits, neg)
    gmax = jnp.max(gl, axis=-1, keepdims=True)
    g_idx = jnp.min(jnp.where(gl == gmax, lane, big), axis=-1, keepdims=True)
    g_w = 1.0 / jnp.sum(jnp.exp(gl - gmax), axis=-1, keepdims=True)

    e_lane = lane - MOE_GROUPS
    in_grp = (e_lane >= g_idx * MOE_EPG) & (e_lane < (g_idx + 1) * MOE_EPG)
    el = jnp.where(in_grp, logits, neg)
    l1 = jnp.max(el, axis=-1, keepdims=True)
    i1 = jnp.min(jnp.where(el == l1, lane, big), axis=-1, keepdims=True)
    el2 = jnp.where(lane == i1, neg, el)
    l2 = jnp.max(el2, axis=-1, keepdims=True)
    i2 = jnp.min(jnp.where(el2 == l2, lane, big), axis=-1, keepdims=True)
    t = jnp.exp(l2 - l1)
    w1 = g_w / (1.0 + t)
    w2 = g_w * t / (1.0 + t)

    oh1 = (lane == i1)
    oh2 = (lane == i2)
    both = (oh1 | oh2).astype(BF16)
    before = jnp.dot(tri_ref[...], both, preferred_element_type=F32) + cnt_scr[...]
    r1 = jnp.sum(jnp.where(oh1, before, 0.0), axis=-1, keepdims=True)
    r2 = jnp.sum(jnp.where(oh2, before, 0.0), axis=-1, keepdims=True)
    cnt_scr[...] = cnt_scr[...] + jnp.sum(both.astype(F32), axis=0, keepdims=True)
    cnt_ref[...] = cnt_scr[...]

    e1 = i1 - MOE_GROUPS
    e2 = i2 - MOE_GROUPS
    meta = jnp.where(lane == 0, e1, jnp.where(lane == 1, e2, jnp.where(
        lane == 2, r1.astype(I32), jnp.where(lane == 3, r2.astype(I32), 0))))
    mt_ref[...] = jnp.transpose(meta)[0:8, :]
    mf_ref[...] = jnp.where(lane == 0, w1, jnp.where(lane == 1, w2, 0.0))


def _dest_kernel(ps_ref, mt_ref, d1_ref, d2_ref):
    e1 = mt_ref[0:1, :]
    e2 = mt_ref[1:2, :]
    p1 = jnp.zeros_like(e1)
    p2 = jnp.zeros_like(e2)
    for e in range(MOE_EXPERTS):
        p1 = jnp.where(e1 == e, ps_ref[e], p1)
        p2 = jnp.where(e2 == e, ps_ref[e], p2)
    d1_ref[...] = p1 + mt_ref[2:3, :]
    d2_ref[...] = p2 + mt_ref[3:4, :]


def _sc_mesh():
    return plsc.VectorSubcoreMesh(core_axis_name="c", subcore_axis_name="s")


def _sc_scatter_rows(src, idx_a, idx_b, n_rows):
    m, w = src.shape

    @functools.partial(pl.kernel, out_type=jax.ShapeDtypeStruct((n_rows, w), src.dtype),
                       mesh=_sc_mesh(), scratch_types=[], name="moe_sc_scatter")
    def scatter(x_hbm, ia_hbm, ib_hbm, o_hbm):
        def body(x_vmem, ia_vmem, ib_vmem):
            pltpu.sync_copy(x_vmem, o_hbm.at[ia_vmem.at[0]])
            pltpu.sync_copy(x_vmem, o_hbm.at[ib_vmem.at[0]])

        pltpu.emit_pipeline(
            body,
            grid=(m // SC_WINDOW,),
            in_specs=[pl.BlockSpec((SC_WINDOW, w), lambda i: (i, 0)),
                      pl.BlockSpec((1, SC_WINDOW), lambda i: (0, i)),
                      pl.BlockSpec((1, SC_WINDOW), lambda i: (0, i))],
            out_specs=[],
            core_axis_name=("c", "s"),
            dimension_semantics=(pltpu.PARALLEL,),
        )(x_hbm, ia_hbm, ib_hbm)

    return scatter(src, idx_a, idx_b)


def _sc_gather_rows(table, idx):
    m = idx.shape[1]
    w = table.shape[1]

    @functools.partial(pl.kernel, out_type=jax.ShapeDtypeStruct((m, w), table.dtype),
                       mesh=_sc_mesh(), scratch_types=[], name="moe_sc_gather")
    def gather(t_hbm, i_hbm, o_hbm):
        def body(i_vmem, o_vmem):
            pltpu.sync_copy(t_hbm.at[i_vmem.at[0]], o_vmem)

        pltpu.emit_pipeline(
            body,
            grid=(m // SC_WINDOW,),
            in_specs=[pl.BlockSpec((1, SC_WINDOW), lambda i: (0, i))],
            out_specs=[pl.BlockSpec((SC_WINDOW, w), lambda i: (i, 0))],
            core_axis_name=("c", "s"),
            dimension_semantics=(pltpu.PARALLEL,),
        )(i_hbm, o_hbm)

    return gather(table, idx)


def _expert_kernel(be_ref, nv_ref, xa_ref, xb_ref, w1_ref, w3_ref, w2_ref, ya_ref, yb_ref,
                   w1_scr, w3_scr, w2_scr):
    i = pl.program_id(0)
    blk = xa_ref.shape[0]
    n_valid = nv_ref[i]
    new_expert = (i == 0) | (be_ref[i] != be_ref[jnp.maximum(i - 1, 0)])

    @pl.when((n_valid > 0) & new_expert)
    def _():
        w1_scr[...] = w1_ref[0].astype(BF16)
        w3_scr[...] = w3_ref[0].astype(BF16)
        w2_scr[...] = w2_ref[0].astype(BF16)

    @pl.when(n_valid > 0)
    def _():
        valid = lax.broadcasted_iota(I32, (blk, 1), 0) < n_valid
        xq = _unpack_row_parts(xa_ref[...], xb_ref[...])
        xin = jnp.concatenate([jnp.where(valid, q, 0.0).astype(BF16) for q in xq], axis=1)
        a = jnp.dot(xin, w1_scr[...], preferred_element_type=F32)
        b = jnp.dot(xin, w3_scr[...], preferred_element_type=F32)
        hid = (_silu(a) * b).astype(BF16)
        y = jnp.dot(hid, w2_scr[...], preferred_element_type=F32)
        ya_ref[...], yb_ref[...] = _pack_row_parts(y)

    @pl.when(n_valid <= 0)
    def _():
        ya_ref[...] = jnp.zeros_like(ya_ref)
        yb_ref[...] = jnp.zeros_like(yb_ref)


def _combine_kernel(alpha, x_ref, g_ref, mf_ref, lng_ref, lnb_ref, y1a_ref, y1b_ref, y2a_ref, y2b_ref, o_ref):
    d = x_ref.shape[2]
    q = d // 4
    w1 = mf_ref[:, 0:1]
    w2 = mf_ref[:, 1:2]
    y1 = _unpack_row_parts(y1a_ref[...], y1b_ref[...])
    y2 = _unpack_row_parts(y2a_ref[...], y2b_ref[...])
    r = [alpha * x_ref[0, :, p * q:(p + 1) * q] + g_ref[0, :, p * q:(p + 1) * q] * (w1 * y1[p] + w2 * y2[p])
         for p in range(4)]
    mu = sum(jnp.sum(rp, axis=-1, keepdims=True) for rp in r) / d
    dev = [rp - mu for rp in r]
    var = sum(jnp.sum(dp * dp, axis=-1, keepdims=True) for dp in dev) / d
    inv = lax.rsqrt(var + NORM_EPS)
    for p in range(4):
        sl = slice(p * q, (p + 1) * q)
        o_ref[0, :, sl] = dev[p] * inv * lng_ref[:, sl] + lnb_ref[:, sl]


def _moe_layer(x, sh, sc, g, wg, bg, we, be, w1, w3, w2, layer, ln_g, ln_b, alpha):
    bsz, s, d = x.shape
    n = bsz * s
    quarter = d // 4
    e_num, blk = MOE_EXPERTS, MOE_BLOCK
    ff = w1.shape[-1]
    ts = min(512, s)
    nt_b = s // ts
    nt = n // ts
    n_rows = n * 2 + e_num * blk
    nb = n_rows // blk

    wcat = jnp.zeros((d, LANES), F32).at[:, :MOE_GROUPS].set(wg).at[:, MOE_GROUPS:MOE_GROUPS + e_num].set(we)
    bcat = jnp.zeros((1, LANES), F32).at[0, :MOE_GROUPS].set(bg).at[0, MOE_GROUPS:MOE_GROUPS + e_num].set(be)
    wcat_hi = wcat.astype(BF16)
    wcat_hl = jnp.concatenate([wcat_hi, (wcat - wcat_hi.astype(F32)).astype(BF16)], axis=1)
    tri = jnp.tri(ts, k=-1, dtype=BF16)

    vec = pl.BlockSpec((1, 1, d), lambda b, i: (b, 0, 0))
    tile = pl.BlockSpec((1, ts, d), lambda b, i: (b, i, 0))
    flat = lambda w: pl.BlockSpec((ts, w), lambda b, i: (b * nt_b + i, 0))
    hpa, hpb, mt, mf, cnt = pl.pallas_call(
        _router_kernel,
        grid=(bsz, nt_b),
        in_specs=[tile, vec, vec, _const_spec((d, 2 * LANES)), _const_spec((1, LANES)), _const_spec((ts, ts))],
        out_specs=[flat(quarter), flat(quarter), pl.BlockSpec((8, ts), lambda b, i: (0, b * nt_b + i)),
                   flat(LANES), _const_spec((1, LANES))],
        out_shape=[jax.ShapeDtypeStruct((n, quarter), I32), jax.ShapeDtypeStruct((n, quarter), I32),
                   jax.ShapeDtypeStruct((8, n), I32), jax.ShapeDtypeStruct((n, LANES), F32),
                   jax.ShapeDtypeStruct((1, LANES), F32)],
        scratch_shapes=[pltpu.VMEM((1, LANES), F32)],
        compiler_params=_cparams(("arbitrary", "arbitrary")),
        name="moe_router",
    )(x, sh, sc, wcat_hl, bcat, tri)

    counts = cnt[0, MOE_GROUPS:MOE_GROUPS + e_num].astype(I32)
    padded = ((counts + blk - 1) // blk) * blk
    pad_end = jnp.cumsum(padded)
    pad_start = pad_end - padded
    blk_start = jnp.arange(nb, dtype=I32) * blk
    blk_expert = jnp.minimum(jnp.sum((pad_end[None, :] <= blk_start[:, None]).astype(I32), axis=1), e_num - 1)
    blk_valid = jnp.clip(pad_start[blk_expert] + counts[blk_expert] - blk_start, 0, blk).astype(I32)

    td = min(8192, n)
    dest1, dest2 = pl.pallas_call(
        _dest_kernel,
        grid_spec=pltpu.PrefetchScalarGridSpec(
            num_scalar_prefetch=1,
            grid=(n // td,),
            in_specs=[pl.BlockSpec((8, td), lambda i, ps_r: (0, i))],
            out_specs=[pl.BlockSpec((1, td), lambda i, ps_r: (0, i))] * 2,
        ),
        out_shape=[jax.ShapeDtypeStruct((1, n), I32)] * 2,
        compiler_params=_cparams(("arbitrary",)),
        name="moe_dest",
    )(pad_start.astype(I32), mt)

    xa = _sc_scatter_rows(hpa, dest1, dest2, n_rows)
    xb = _sc_scatter_rows(hpb, dest1, dest2, n_rows)

    rows_spec = pl.BlockSpec((blk, quarter), lambda i, be_r, nv_r: (i, 0))
    ya, yb = pl.pallas_call(
        _expert_kernel,
        grid_spec=pltpu.PrefetchScalarGridSpec(
            num_scalar_prefetch=2,
            grid=(nb,),
            in_specs=[rows_spec, rows_spec,
                      pl.BlockSpec((1, d, ff), lambda i, be_r, nv_r: (layer * e_num + be_r[i], 0, 0)),
                      pl.BlockSpec((1, d, ff), lambda i, be_r, nv_r: (layer * e_num + be_r[i], 0, 0)),
                      pl.BlockSpec((1, ff, d), lambda i, be_r, nv_r: (layer * e_num + be_r[i], 0, 0))],
            out_specs=[rows_spec, rows_spec],
            scratch_shapes=[pltpu.VMEM((d, ff), BF16), pltpu.VMEM((d, ff), BF16), pltpu.VMEM((ff, d), BF16)],
        ),
        out_shape=[jax.ShapeDtypeStruct((n_rows, quarter), I32)] * 2,
        compiler_params=_cparams(("arbitrary",)),
        name="moe_experts",
    )(blk_expert, blk_valid, xa, xb, w1, w3, w2)

    dest12 = jnp.concatenate([dest1, dest2], axis=1)
    ga = _sc_gather_rows(ya, dest12)
    gb = _sc_gather_rows(yb, dest12)

    first = pl.BlockSpec((ts, quarter), lambda b, i: (b * nt_b + i, 0))
    second = pl.BlockSpec((ts, quarter), lambda b, i: (nt + b * nt_b + i, 0))
    return pl.pallas_call(
        functools.partial(_combine_kernel, alpha),
        grid=(bsz, nt_b),
        in_specs=[tile, vec, flat(LANES), _const_spec((1, d)), _const_spec((1, d)),
                  first, first, second, second],
        out_specs=tile,
        out_shape=jax.ShapeDtypeStruct((bsz, s, d), F32),
        compiler_params=_cparams(("arbitrary", "arbitrary")),
        name="moe_combine",
    )(x, g, mf, ln_g.reshape(1, d), ln_b.reshape(1, d), ga, gb, ga, gb)


def kernel(x, c, ada_w, ada_b, ln_g, ln_b, hgrn_w_in, hgrn_w_out, hgrn_lb, hgrn_norm_w, attn_w_in, attn_w_out,
           attn_lambda, attn_subln_w, router_g_w, router_g_b, router_e_w, router_e_b, moe_w1, moe_w3, moe_w2):
    depth = ada_w.shape[0]
    bsz, s, d = x.shape
    alpha = (2 * depth) ** 0.25
    lb_all = jnp.cumsum(jax.nn.softmax(hgrn_lb.astype(F32), axis=0), axis=0)
    lb_all = lb_all - lb_all[0:1]
    mod = _ada_mod(c, ada_w, ada_b).reshape(depth, bsz, 6, 1, d)
    w1_all = moe_w1.reshape((-1,) + moe_w1.shape[2:])
    w3_all = moe_w3.reshape((-1,) + moe_w3.shape[2:])
    w2_all = moe_w2.reshape((-1,) + moe_w2.shape[2:])
    for i in range(depth):
        sh1, sc1, g1, sh2, sc2, g2 = (mod[i, :, m] for m in range(6))
        j = i // 2
        if i % 2 == 0:
            x = _hgrn_layer(x, sh1, sc1, g1, hgrn_w_in[j], hgrn_w_out[j], lb_all[j], hgrn_norm_w[j],
                            ln_g[i, 0], ln_b[i, 0], alpha)
        else:
            lambda_init = 0.8 - 0.6 * math.exp(-0.3 * i)
            x = _attn_layer(x, sh1, sc1, g1, attn_w_in[j], attn_w_out[j], attn_lambda[j], attn_subln_w[j],
                            lambda_init, ln_g[i, 0], ln_b[i, 0], alpha)
        x = _moe_layer(x, sh2, sc2, g2, router_g_w[i], router_g_b[i], router_e_w[i], router_e_b[i],
                       w1_all, w3_all, w2_all, i, ln_g[i, 1], ln_b[i, 1], alpha)
    return x
```

```python
import functools
import math

import jax
import jax.numpy as jnp
from jax import lax
from jax.experimental import pallas as pl
from jax.experimental.pallas import tpu as pltpu
from jax.experimental.pallas import tpu_sc as plsc

F32 = jnp.float32
BF16 = jnp.bfloat16
I32 = jnp.int32
U32 = jnp.uint32
HIGHEST = lax.Precision.HIGHEST

LANES = 128
HEAD = 128
HGRN_CHUNK = 32
HGRN_SAFE_SPAN = 80.0
ATTN_D = 64
FLASH_ROW_CHUNK = 32
ROPE_THETA = 10000.0
MOE_GROUPS = 4
MOE_EPG = 8
MOE_EXPERTS = MOE_GROUPS * MOE_EPG
MOE_BLOCK = 512
SC_WINDOW = 128
NORM_EPS = 1e-5
VMEM_LIMIT = 56 * 1024 * 1024

NT_DIMS = (((1,), (1,)), ((), ()))
TN_DIMS = (((0,), (0,)), ((), ()))


def _cparams(sem):
    return pltpu.CompilerParams(dimension_semantics=sem, vmem_limit_bytes=VMEM_LIMIT)


def _const_spec(shape):
    nd = len(shape)
    return pl.BlockSpec(shape, lambda *_: (0,) * nd)


def _layer_norm(r, g, b):
    mu = jnp.mean(r, axis=-1, keepdims=True)
    d = r - mu
    var = jnp.mean(d * d, axis=-1, keepdims=True)
    return d * lax.rsqrt(var + NORM_EPS) * g + b


def _silu(x):
    return x * (1.0 / (1.0 + jnp.exp(-x)))


def _pack_bf16_pair(lo, hi):
    lo_b = lax.bitcast_convert_type(lo.astype(BF16).astype(F32), U32)
    hi_b = lax.bitcast_convert_type(hi.astype(BF16).astype(F32), U32)
    return (hi_b & jnp.uint32(0xFFFF0000)) | (lo_b >> 16)


def _unpack_bf16_pair(u):
    lo = lax.bitcast_convert_type(u << 16, F32)
    hi = lax.bitcast_convert_type(u & jnp.uint32(0xFFFF0000), F32)
    return lo, hi


def _ada_kernel(c_ref, w_ref, b_ref, o_ref):
    c = c_ref[...]
    o_ref[0] = jnp.dot(_silu(c), w_ref[0], precision=HIGHEST, preferred_element_type=F32) + b_ref[0]


def _ada_mod(c, ada_w, ada_b):
    depth, d, n6 = ada_w.shape
    bsz = c.shape[0]
    tn = d
    return pl.pallas_call(
        _ada_kernel,
        grid=(depth, n6 // tn),
        in_specs=[
            pl.BlockSpec((bsz, d), lambda i, j: (0, 0)),
            pl.BlockSpec((1, d, tn), lambda i, j: (i, 0, j)),
            pl.BlockSpec((1, 1, tn), lambda i, j: (i, 0, j)),
        ],
        out_specs=pl.BlockSpec((1, bsz, tn), lambda i, j: (i, 0, j)),
        out_shape=jax.ShapeDtypeStruct((depth, bsz, n6), F32),
        compiler_params=_cparams(("arbitrary", "arbitrary")),
        name="ada_mod",
    )(c, ada_w, ada_b.reshape(depth, 1, n6))


def _hgrn_kernel(alpha, x_ref, sh_ref, sc_ref, g_ref, win_ref, wout_ref, loglb_ref, oml_ref,
                 nw_ref, lng_ref, lnb_ref, o_ref, proj_ref, st_ref, ocat_ref):
    ts, d = x_ref.shape[1], x_ref.shape[2]
    nh = d // HEAD
    c = HGRN_CHUNK
    nc = ts // c

    @pl.when(pl.program_id(1) == 0)
    def _():
        st_ref[...] = jnp.zeros_like(st_ref)

    x = x_ref[0]
    h = x * (1.0 + sc_ref[0]) + sh_ref[0]
    proj_ref[...] = jnp.dot(h.astype(BF16), win_ref[...], preferred_element_type=F32)

    z = proj_ref[:, d:2 * d]
    ls = jnp.minimum(z, 0.0) - jnp.log(1.0 + jnp.exp(-jnp.abs(z)))
    lsn = ls - z
    cc = loglb_ref[...] + lsn
    log_f = jnp.maximum(ls, cc) + jnp.log(1.0 + jnp.exp(-jnp.abs(ls - cc)))
    kk = oml_ref[...] * jnp.exp(lsn)

    pos = lax.broadcasted_iota(I32, (ts, 1), 0) % c
    b = log_f
    step = 1
    while step < c:
        b = b + jnp.where(pos >= step, pltpu.roll(b, step, 0), 0.0)
        step *= 2

    b3 = b.reshape(nc, c, d)
    b_last = b3[:, c - 1:c, :]
    b_mid = b3[:, c // 2 - 1:c // 2, :]
    q3 = proj_ref[:, 0:d].reshape(nc, c, d)
    k3 = kk.reshape(nc, c, d)
    q_inter = (q3 * jnp.exp(b3)).astype(BF16)
    k_state = (k3 * jnp.exp(b_last - b3)).astype(BF16)
    q_intra = (q3 * jnp.exp(b3 - b_mid)).astype(BF16).reshape(ts, d)
    k_intra = (k3 * jnp.exp(b_mid - b3)).astype(BF16).reshape(ts, d)
    dec = jnp.exp(b_last)
    v2 = proj_ref[:, 2 * d:3 * d].astype(BF16)
    v3 = v2.reshape(nc, c, d)

    row = lax.broadcasted_iota(I32, (ts, ts), 0)
    col = lax.broadcasted_iota(I32, (ts, ts), 1)
    keep = (row >= col) & (row // c == col // c)
    for hd in range(nh):
        sl = slice(hd * HEAD, (hd + 1) * HEAD)
        sc = lax.dot_general(q_intra[:, sl], k_intra[:, sl], NT_DIMS, preferred_element_type=F32)
        p = jnp.where(keep, sc, 0.0).astype(BF16)
        ocat_ref[:, sl] = jnp.dot(p, v2[:, sl], preferred_element_type=F32)

    @pl.when(jnp.max(-b_last) > HGRN_SAFE_SPAN)
    def _():
        q2 = proj_ref[:, 0:d]
        diag = q2 * kk
        tpos = lax.broadcasted_iota(I32, (ts, 1), 0)
        scores = [jnp.where(row == col, jnp.sum(diag[:, hd * HEAD:(hd + 1) * HEAD], axis=-1, keepdims=True), 0.0)
                  for hd in range(nh)]
        size = 2
        while size <= c:
            ref_row = (row // size) * size + (size // 2 - 1)
            pick = jnp.where(col == ref_row, 1.0, 0.0)
            b_ref = jnp.dot(pick, b, precision=HIGHEST, preferred_element_type=F32)
            right = (tpos % size) >= (size // 2)
            qa = jnp.where(right, q2 * jnp.exp(b - b_ref), 0.0).astype(BF16)
            ka = jnp.where(right, 0.0, kk * jnp.exp(b_ref - b)).astype(BF16)
            same = (row // size) == (col // size)
            for hd in range(nh):
                sl = slice(hd * HEAD, (hd + 1) * HEAD)
                sc = lax.dot_general(qa[:, sl], ka[:, sl], NT_DIMS, preferred_element_type=F32)
                scores[hd] = scores[hd] + jnp.where(same, sc, 0.0)
            size *= 2
        for hd in range(nh):
            sl = slice(hd * HEAD, (hd + 1) * HEAD)
            ocat_ref[:, sl] = jnp.dot(scores[hd].astype(BF16), v2[:, sl], preferred_element_type=F32)

    for ci in range(nc):
        for hd in range(nh):
            sl = slice(hd * HEAD, (hd + 1) * HEAD)
            st = st_ref[hd]
            o_inter = lax.dot_general(q_inter[ci, :, sl], st.astype(BF16), NT_DIMS,
                                      preferred_element_type=F32)
            ocat_ref[ci * c:(ci + 1) * c, sl] += o_inter
            upd = lax.dot_general(v3[ci, :, sl], k_state[ci, :, sl], TN_DIMS,
                                  preferred_element_type=F32)
            st_ref[hd] = st * dec[ci, :, sl] + upd

    for hd in range(nh):
        sl = slice(hd * HEAD, (hd + 1) * HEAD)
        oh = ocat_ref[:, sl]
        ms = jnp.mean(oh * oh, axis=-1, keepdims=True)
        gate = proj_ref[:, 3 * d + hd * HEAD:3 * d + (hd + 1) * HEAD]
        ocat_ref[:, sl] = oh * lax.rsqrt(ms + NORM_EPS) * nw_ref[...] * _silu(gate)
    y = jnp.dot(ocat_ref[...].astype(BF16), wout_ref[...], preferred_element_type=F32)
    r = alpha * x + g_ref[0] * y
    o_ref[0] = _layer_norm(r, lng_ref[...], lnb_ref[...])


def _hgrn_layer(x, sh, sc, g, w_in, w_out, lb, norm_w, ln_g, ln_b, alpha):
    bsz, s, d = x.shape
    ts = min(256, s)
    nh = d // HEAD
    vec = pl.BlockSpec((1, 1, d), lambda b, i: (b, 0, 0))
    tile = pl.BlockSpec((1, ts, d), lambda b, i: (b, i, 0))
    return pl.pallas_call(
        functools.partial(_hgrn_kernel, alpha),
        grid=(bsz, s // ts),
        in_specs=[tile, vec, vec, vec,
                  _const_spec((d, 4 * d)), _const_spec((d, d)),
                  _const_spec((1, d)), _const_spec((1, d)), _const_spec((1, HEAD)),
                  _const_spec((1, d)), _const_spec((1, d))],
        out_specs=tile,
        out_shape=jax.ShapeDtypeStruct((bsz, s, d), F32),
        scratch_shapes=[pltpu.VMEM((ts, 4 * d), F32),
                        pltpu.VMEM((nh, HEAD, HEAD), F32),
                        pltpu.VMEM((ts, d), F32)],
        compiler_params=_cparams(("arbitrary", "arbitrary")),
        name="hgrn_layer",
    )(x, sh, sc, g, w_in.astype(BF16), w_out.astype(BF16),
      jnp.log(lb).reshape(1, d), (1.0 - lb).reshape(1, d), norm_w.reshape(1, HEAD),
      ln_g.reshape(1, d), ln_b.reshape(1, d))


def _attn_in_kernel(x_ref, sh_ref, sc_ref, w_ref, cos_ref, sin_ref, q_ref, k_ref, v_ref):
    d = x_ref.shape[2]
    x = x_ref[0]
    h = x * (1.0 + sc_ref[0]) + sh_ref[0]
    qkv = jnp.dot(h.astype(BF16), w_ref[...], preferred_element_type=F32)
    cos = cos_ref[...]
    sin = sin_ref[...]
    lane = lax.broadcasted_iota(I32, (1, LANES), 1)
    first_half = (lane % ATTN_D) < (ATTN_D // 2)
    scale = ATTN_D ** -0.5 * math.log2(math.e)
    for j in range(d // LANES):
        sl = slice(j * LANES, (j + 1) * LANES)
        for src, dst, mul in ((0, q_ref, scale), (d, k_ref, 1.0)):
            t = qkv[:, src + j * LANES:src + (j + 1) * LANES]
            partner = jnp.where(first_half, pltpu.roll(t, LANES - ATTN_D // 2, 1),
                                pltpu.roll(t, ATTN_D // 2, 1))
            dst[0, :, sl] = ((t * cos + partner * sin) * mul).astype(BF16)
    v_ref[0] = qkv[:, 2 * d:].astype(BF16)


def _flash_kernel(out_scale, lam_ref, q_ref, k_ref, v_ref, w_ref, o_ref,
                  m_ref, l_ref, acc_ref, aa_ref, ab_ref, sa_ref, sb_ref, pa_ref, pb_ref, qq_ref):
    tq = q_ref.shape[1]
    rows = 2 * tq
    qi = pl.program_id(2)
    q = q_ref[0]
    lane = lax.broadcasted_iota(I32, (1, LANES), 1)
    zero = jnp.zeros_like(q)
    qq_ref[0:tq, :] = jnp.where(lane < ATTN_D, q, zero)
    qq_ref[tq:rows, :] = jnp.where(lane >= ATTN_D, q, zero)

    m_ref[...] = jnp.full_like(m_ref, -jnp.inf)
    l_ref[...] = jnp.zeros_like(l_ref)
    acc_ref[...] = jnp.zeros_like(acc_ref)
    nrep = tq // LANES
    rc = FLASH_ROW_CHUNK

    def scores(j, s_ref):
        kb = k_ref[0, pl.ds(pl.multiple_of(j * tq, tq), tq), :]
        s_ref[...] = lax.dot_general(qq_ref[...], kb, NT_DIMS, preferred_element_type=F32)

    def softmax(s_ref, p_ref, a_ref, masked):
        for r0 in range(0, rows, rc):
            rs = slice(r0, r0 + rc)
            s = s_ref[rs, :]
            if masked:
                row = lax.broadcasted_iota(I32, (rc, tq), 0) + (r0 % tq)
                col = lax.broadcasted_iota(I32, (rc, tq), 1)
                s = jnp.where(row >= col, s, -jnp.inf)
            m_old = m_ref[rs, :]
            m_new = jnp.maximum(m_old, jnp.max(s, axis=-1, keepdims=True))
            a = jnp.exp2(m_old - m_new)
            p = jnp.exp2(s - jnp.concatenate([m_new] * nrep, axis=1))
            psum = p[:, 0:LANES]
            for r in range(1, nrep):
                psum = psum + p[:, r * LANES:(r + 1) * LANES]
            l_ref[rs, :] = a * l_ref[rs, :] + psum
            m_ref[rs, :] = m_new
            a_ref[rs, :] = a
            p_ref[rs, :] = p.astype(BF16)

    def values(j, p_ref, a_ref):
        vb = v_ref[0, pl.ds(pl.multiple_of(j * tq, tq), tq), :]
        acc_ref[...] = a_ref[...] * acc_ref[...] + jnp.dot(p_ref[...], vb, preferred_element_type=F32)

    n_blk = qi + 1
    scores(0, sa_ref)

    @pl.when(n_blk == 1)
    def _():
        softmax(sa_ref, pa_ref, aa_ref, True)
        values(0, pa_ref, aa_ref)

    @pl.when(n_blk >= 2)
    def _():
        scores(1, sb_ref)
        softmax(sa_ref, pa_ref, aa_ref, False)

        def body(t, carry):
            scores(2 * t + 2, sa_ref)
            softmax(sb_ref, pb_ref, ab_ref, False)
            values(2 * t, pa_ref, aa_ref)
            scores(2 * t + 3, sb_ref)
            softmax(sa_ref, pa_ref, aa_ref, False)
            values(2 * t + 1, pb_ref, ab_ref)
            return carry

        n_pairs = (n_blk - 2) // 2
        lax.fori_loop(0, n_pairs, body, 0)
        done = 2 * n_pairs

        @pl.when(n_blk % 2 == 0)
        def _():
            softmax(sb_ref, pb_ref, ab_ref, True)
            values(done, pa_ref, aa_ref)
            values(done + 1, pb_ref, ab_ref)

        @pl.when(n_blk % 2 == 1)
        def _():
            scores(done + 2, sa_ref)
            softmax(sb_ref, pb_ref, ab_ref, False)
            values(done, pa_ref, aa_ref)
            softmax(sa_ref, pa_ref, aa_ref, True)
            values(done + 1, pb_ref, ab_ref)
            values(done + 2, pa_ref, aa_ref)

    o_all = acc_ref[...] / jnp.sum(l_ref[...], axis=-1, keepdims=True)
    o = o_all[:tq] - lam_ref[0] * o_all[tq:]
    ms = jnp.mean(o * o, axis=-1, keepdims=True)
    o_ref[0] = (o * lax.rsqrt(ms + NORM_EPS) * w_ref[...] * out_scale).astype(BF16)


def _resid_ln_kernel(alpha, x_ref, g_ref, o_in_ref, w_ref, lng_ref, lnb_ref, o_ref):
    y = jnp.dot(o_in_ref[0], w_ref[...], preferred_element_type=F32)
    r = alpha * x_ref[0] + g_ref[0] * y
    o_ref[0] = _layer_norm(r, lng_ref[...], lnb_ref[...])


def _attn_layer(x, sh, sc, g, w_in, w_out, lam_params, subln_w, lambda_init, ln_g, ln_b, alpha):
    bsz, s, d = x.shape
    nh = d // HEAD
    ts = min(512, s)
    tq = min(512, s)
    half = ATTN_D // 2
    inv_freq = ROPE_THETA ** (-jnp.arange(half, dtype=F32) / half)
    ang = jnp.arange(s, dtype=F32)[:, None] * inv_freq[None, :]
    cos_t = jnp.tile(jnp.cos(ang), (1, LANES // half))
    sin_h = jnp.sin(ang)
    sin_t = jnp.tile(jnp.concatenate([-sin_h, sin_h], axis=1), (1, LANES // ATTN_D))

    vec = pl.BlockSpec((1, 1, d), lambda b, i: (b, 0, 0))
    tile = pl.BlockSpec((1, ts, d), lambda b, i: (b, i, 0))
    rope = pl.BlockSpec((ts, LANES), lambda b, i: (i, 0))
    q, k, v = pl.pallas_call(
        _attn_in_kernel,
        grid=(bsz, s // ts),
        in_specs=[tile, vec, vec, _const_spec((d, 3 * d)), rope, rope],
        out_specs=[tile, tile, tile],
        out_shape=[jax.ShapeDtypeStruct((bsz, s, d), BF16)] * 3,
        compiler_params=_cparams(("arbitrary", "arbitrary")),
        name="attn_in",
    )(x, sh, sc, w_in.astype(BF16), cos_t, sin_t)

    lp = lam_params.astype(F32)
    lam = (jnp.exp(jnp.sum(lp[0] * lp[1])) - jnp.exp(jnp.sum(lp[2] * lp[3])) + lambda_init).reshape(1)
    qspec = pl.BlockSpec((1, tq, HEAD), lambda b, h, i: (b, i, h))
    kvspec = pl.BlockSpec((1, s, HEAD), lambda b, h, i: (b, 0, h))
    o = pl.pallas_call(
        functools.partial(_flash_kernel, 1.0 - lambda_init),
        grid=(bsz, nh, s // tq),
        in_specs=[pl.BlockSpec(memory_space=pltpu.SMEM), qspec, kvspec, kvspec,
                  pl.BlockSpec((1, HEAD), lambda b, h, i: (0, 0))],
        out_specs=qspec,
        out_shape=jax.ShapeDtypeStruct((bsz, s, d), BF16),
        scratch_shapes=[pltpu.VMEM((2 * tq, LANES), F32), pltpu.VMEM((2 * tq, LANES), F32),
                        pltpu.VMEM((2 * tq, HEAD), F32),
                        pltpu.VMEM((2 * tq, LANES), F32), pltpu.VMEM((2 * tq, LANES), F32),
                        pltpu.VMEM((2 * tq, tq), F32), pltpu.VMEM((2 * tq, tq), F32),
                        pltpu.VMEM((2 * tq, tq), BF16), pltpu.VMEM((2 * tq, tq), BF16),
                        pltpu.VMEM((2 * tq, HEAD), BF16)],
        compiler_params=_cparams(("arbitrary", "arbitrary", "arbitrary")),
        name="diff_flash",
    )(lam, q, k, v, subln_w.reshape(1, HEAD))

    return pl.pallas_call(
        functools.partial(_resid_ln_kernel, alpha),
        grid=(bsz, s // ts),
        in_specs=[tile, vec, tile, _const_spec((d, d)), _const_spec((1, d)), _const_spec((1, d))],
        out_specs=tile,
        out_shape=jax.ShapeDtypeStruct((bsz, s, d), F32),
        compiler_params=_cparams(("arbitrary", "arbitrary")),
        name="attn_out",
    )(x, g, o, w_out.astype(BF16), ln_g.reshape(1, d), ln_b.reshape(1, d))


def _pack_row_parts(v):
    q = v.shape[1] // 4
    return tuple(lax.bitcast_convert_type(_pack_bf16_pair(v[:, p * q:(p + 1) * q], v[:, (2 + p) * q:(3 + p) * q]), I32)
                 for p in range(2))


def _unpack_row_parts(part_a, part_b):
    lo_a, hi_a = _unpack_bf16_pair(lax.bitcast_convert_type(part_a, U32))
    lo_b, hi_b = _unpack_bf16_pair(lax.bitcast_convert_type(part_b, U32))
    return lo_a, lo_b, hi_a, hi_b


def _router_kernel(x_ref, sh_ref, sc_ref, w_ref, bias_ref, tri_ref, hpa_ref, hpb_ref, mt_ref, mf_ref, cnt_ref,
                   cnt_scr):
    ts, d = x_ref.shape[1], x_ref.shape[2]
    first = (pl.program_id(0) == 0) & (pl.program_id(1) == 0)

    @pl.when(first)
    def _():
        cnt_scr[...] = jnp.zeros_like(cnt_scr)

    x = x_ref[0]
    h = x * (1.0 + sc_ref[0]) + sh_ref[0]
    hpa_ref[...], hpb_ref[...] = _pack_row_parts(h)

    h_hi = h.astype(BF16)
    h_lo = (h - h_hi.astype(F32)).astype(BF16)
    hh = jnp.dot(h_hi, w_ref[...], preferred_element_type=F32)
    lh = jnp.dot(h_lo, w_ref[:, 0:LANES], preferred_element_type=F32)
    logits = hh[:, 0:LANES] + hh[:, LANES:2 * LANES] + lh + bias_ref[...]
    lane = lax.broadcasted_iota(I32, (ts, LANES), 1)
    neg = -jnp.inf
    big = jnp.int32(LANES)
    is_g = lane < MOE_GROUPS
    gl = jnp.where(is_g, logits, neg)
    gmax = jnp.max(gl, axis=-1, keepdims=True)
    g_idx = jnp.min(jnp.where(gl == gmax, lane, big), axis=-1, keepdims=True)
    g_w = 1.0 / jnp.sum(jnp.exp(gl - gmax), axis=-1, keepdims=True)

    e_lane = lane - MOE_GROUPS
    in_grp = (e_lane >= g_idx * MOE_EPG) & (e_lane < (g_idx + 1) * MOE_EPG)
    el = jnp.where(in_grp, logits, neg)
    l1 = jnp.max(el, axis=-1, keepdims=True)
    i1 = jnp.min(jnp.where(el == l1, lane, big), axis=-1, keepdims=True)
    el2 = jnp.where(lane == i1, neg, el)
    l2 = jnp.max(el2, axis=-1, keepdims=True)
    i2 = jnp.min(jnp.where(el2 == l2, lane, big), axis=-1, keepdims=True)
    t = jnp.exp(l2 - l1)
    w1 = g_w / (1.0 + t)
    w2 = g_w * t / (1.0 + t)

    oh1 = (lane == i1)
    oh2 = (lane == i2)
    both = (oh1 | oh2).astype(BF16)
    before = jnp.dot(tri_ref[...], both, preferred_element_type=F32) + cnt_scr[...]
    r1 = jnp.sum(jnp.where(oh1, before, 0.0), axis=-1, keepdims=True)
    r2 = jnp.sum(jnp.where(oh2, before, 0.0), axis=-1, keepdims=True)
    cnt_scr[...] = cnt_scr[...] + jnp.sum(both.astype(F32), axis=0, keepdims=True)
    cnt_ref[...] = cnt_scr[...]

    e1 = i1 - MOE_GROUPS
    e2 = i2 - MOE_GROUPS
    meta = jnp.where(lane == 0, e1, jnp.where(lane == 1, e2, jnp.where(
        lane == 2, r1.astype(I32), jnp.where(lane == 3, r2.astype(I32), 0))))
    mt_ref[...] = jnp.transpose(meta)[0:8, :]
    mf_ref[...] = jnp.where(lane == 0, w1, jnp.where(lane == 1, w2, 0.0))


def _dest_kernel(ps_ref, mt_ref, d1_ref, d2_ref):
    e1 = mt_ref[0:1, :]
    e2 = mt_ref[1:2, :]
    p1 = jnp.zeros_like(e1)
    p2 = jnp.zeros_like(e2)
    for e in range(MOE_EXPERTS):
        p1 = jnp.where(e1 == e, ps_ref[e], p1)
        p2 = jnp.where(e2 == e, ps_ref[e], p2)
    d1_ref[...] = p1 + mt_ref[2:3, :]
    d2_ref[...] = p2 + mt_ref[3:4, :]


def _sc_mesh():
    return plsc.VectorSubcoreMesh(core_axis_name="c", subcore_axis_name="s")


def _sc_scatter_rows(src, idx_a, idx_b, n_rows):
    m, w = src.shape

    @functools.partial(pl.kernel, out_type=jax.ShapeDtypeStruct((n_rows, w), src.dtype),
                       mesh=_sc_mesh(), scratch_types=[], name="moe_sc_scatter")
    def scatter(x_hbm, ia_hbm, ib_hbm, o_hbm):
        def body(x_vmem, ia_vmem, ib_vmem):
            pltpu.sync_copy(x_vmem, o_hbm.at[ia_vmem.at[0]])
            pltpu.sync_copy(x_vmem, o_hbm.at[ib_vmem.at[0]])

        pltpu.emit_pipeline(
            body,
            grid=(m // SC_WINDOW,),
            in_specs=[pl.BlockSpec((SC_WINDOW, w), lambda i: (i, 0)),
                      pl.BlockSpec((1, SC_WINDOW), lambda i: (0, i)),
                      pl.BlockSpec((1, SC_WINDOW), lambda i: (0, i))],
            out_specs=[],
            core_axis_name=("c", "s"),
            dimension_semantics=(pltpu.PARALLEL,),
        )(x_hbm, ia_hbm, ib_hbm)

    return scatter(src, idx_a, idx_b)


def _sc_gather_rows(table, idx):
    m = idx.shape[1]
    w = table.shape[1]

    @functools.partial(pl.kernel, out_type=jax.ShapeDtypeStruct((m, w), table.dtype),
                       mesh=_sc_mesh(), scratch_types=[], name="moe_sc_gather")
    def gather(t_hbm, i_hbm, o_hbm):
        def body(i_vmem, o_vmem):
            pltpu.sync_copy(t_hbm.at[i_vmem.at[0]], o_vmem)

        pltpu.emit_pipeline(
            body,
            grid=(m // SC_WINDOW,),
            in_specs=[pl.BlockSpec((1, SC_WINDOW), lambda i: (0, i))],
            out_specs=[pl.BlockSpec((SC_WINDOW, w), lambda i: (i, 0))],
            core_axis_name=("c", "s"),
            dimension_semantics=(pltpu.PARALLEL,),
        )(i_hbm, o_hbm)

    return gather(table, idx)


def _expert_kernel(be_ref, nv_ref, xa_ref, xb_ref, w1_ref, w3_ref, w2_ref, ya_ref, yb_ref,
                   w1_scr, w3_scr, w2_scr):
    i = pl.program_id(0)
    blk = xa_ref.shape[0]
    n_valid = nv_ref[i]
    new_expert = (i == 0) | (be_ref[i] != be_ref[jnp.maximum(i - 1, 0)])

    @pl.when((n_valid > 0) & new_expert)
    def _():
        w1_scr[...] = w1_ref[0].astype(BF16)
        w3_scr[...] = w3_ref[0].astype(BF16)
        w2_scr[...] = w2_ref[0].astype(BF16)

    @pl.when(n_valid > 0)
    def _():
        valid = lax.broadcasted_iota(I32, (blk, 1), 0) < n_valid
        xq = _unpack_row_parts(xa_ref[...], xb_ref[...])
        xin = jnp.concatenate([jnp.where(valid, q, 0.0).astype(BF16) for q in xq], axis=1)
        a = jnp.dot(xin, w1_scr[...], preferred_element_type=F32)
        b = jnp.dot(xin, w3_scr[...], preferred_element_type=F32)
        hid = (_silu(a) * b).astype(BF16)
        y = jnp.dot(hid, w2_scr[...], preferred_element_type=F32)
        ya_ref[...], yb_ref[...] = _pack_row_parts(y)

    @pl.when(n_valid <= 0)
    def _():
        ya_ref[...] = jnp.zeros_like(ya_ref)
        yb_ref[...] = jnp.zeros_like(yb_ref)


def _combine_kernel(alpha, x_ref, g_ref, mf_ref, lng_ref, lnb_ref, y1a_ref, y1b_ref, y2a_ref, y2b_ref, o_ref):
    d = x_ref.shape[2]
    q = d // 4
    w1 = mf_ref[:, 0:1]
    w2 = mf_ref[:, 1:2]
    y1 = _unpack_row_parts(y1a_ref[...], y1b_ref[...])
    y2 = _unpack_row_parts(y2a_ref[...], y2b_ref[...])
    r = [alpha * x_ref[0, :, p * q:(p + 1) * q] + g_ref[0, :, p * q:(p + 1) * q] * (w1 * y1[p] + w2 * y2[p])
         for p in range(4)]
    mu = sum(jnp.sum(rp, axis=-1, keepdims=True) for rp in r) / d
    dev = [rp - mu for rp in r]
    var = sum(jnp.sum(dp * dp, axis=-1, keepdims=True) for dp in dev) / d
    inv = lax.rsqrt(var + NORM_EPS)
    for p in range(4):
        sl = slice(p * q, (p + 1) * q)
        o_ref[0, :, sl] = dev[p] * inv * lng_ref[:, sl] + lnb_ref[:, sl]


def _moe_layer(x, sh, sc, g, wg, bg, we, be, w1, w3, w2, layer, ln_g, ln_b, alpha):
    bsz, s, d = x.shape
    n = bsz * s
    quarter = d // 4
    e_num, blk = MOE_EXPERTS, MOE_BLOCK
    ff = w1.shape[-1]
    ts = min(512, s)
    nt_b = s // ts
    nt = n // ts
    n_rows = n * 2 + e_num * blk
    nb = n_rows // blk

    wcat = jnp.zeros((d, LANES), F32).at[:, :MOE_GROUPS].set(wg).at[:, MOE_GROUPS:MOE_GROUPS + e_num].set(we)
    bcat = jnp.zeros((1, LANES), F32).at[0, :MOE_GROUPS].set(bg).at[0, MOE_GROUPS:MOE_GROUPS + e_num].set(be)
    wcat_hi = wcat.astype(BF16)
    wcat_hl = jnp.concatenate([wcat_hi, (wcat - wcat_hi.astype(F32)).astype(BF16)], axis=1)
    tri = jnp.tri(ts, k=-1, dtype=BF16)

    vec = pl.BlockSpec((1, 1, d), lambda b, i: (b, 0, 0))
    tile = pl.BlockSpec((1, ts, d), lambda b, i: (b, i, 0))
    flat = lambda w: pl.BlockSpec((ts, w), lambda b, i: (b * nt_b + i, 0))
    hpa, hpb, mt, mf, cnt = pl.pallas_call(
        _router_kernel,
        grid=(bsz, nt_b),
        in_specs=[tile, vec, vec, _const_spec((d, 2 * LANES)), _const_spec((1, LANES)), _const_spec((ts, ts))],
        out_specs=[flat(quarter), flat(quarter), pl.BlockSpec((8, ts), lambda b, i: (0, b * nt_b + i)),
                   flat(LANES), _const_spec((1, LANES))],
        out_shape=[jax.ShapeDtypeStruct((n, quarter), I32), jax.ShapeDtypeStruct((n, quarter), I32),
                   jax.ShapeDtypeStruct((8, n), I32), jax.ShapeDtypeStruct((n, LANES), F32),
                   jax.ShapeDtypeStruct((1, LANES), F32)],
        scratch_shapes=[pltpu.VMEM((1, LANES), F32)],
        compiler_params=_cparams(("arbitrary", "arbitrary")),
        name="moe_router",
    )(x, sh, sc, wcat_hl, bcat, tri)

    counts = cnt[0, MOE_GROUPS:MOE_GROUPS + e_num].astype(I32)
    padded = ((counts + blk - 1) // blk) * blk
    pad_end = jnp.cumsum(padded)
    pad_start = pad_end - padded
    blk_start = jnp.arange(nb, dtype=I32) * blk
    blk_expert = jnp.minimum(jnp.sum((pad_end[None, :] <= blk_start[:, None]).astype(I32), axis=1), e_num - 1)
    blk_valid = jnp.clip(pad_start[blk_expert] + counts[blk_expert] - blk_start, 0, blk).astype(I32)

    td = min(8192, n)
    dest1, dest2 = pl.pallas_call(
        _dest_kernel,
        grid_spec=pltpu.PrefetchScalarGridSpec(
            num_scalar_prefetch=1,
            grid=(n // td,),
            in_specs=[pl.BlockSpec((8, td), lambda i, ps_r: (0, i))],
            out_specs=[pl.BlockSpec((1, td), lambda i, ps_r: (0, i))] * 2,
        ),
        out_shape=[jax.ShapeDtypeStruct((1, n), I32)] * 2,
        compiler_params=_cparams(("arbitrary",)),
        name="moe_dest",
    )(pad_start.astype(I32), mt)

    xa = _sc_scatter_rows(hpa, dest1, dest2, n_rows)
    xb = _sc_scatter_rows(hpb, dest1, dest2, n_rows)

    rows_spec = pl.BlockSpec((blk, quarter), lambda i, be_r, nv_r: (i, 0))
    ya, yb = pl.pallas_call(
        _expert_kernel,
        grid_spec=pltpu.PrefetchScalarGridSpec(
            num_scalar_prefetch=2,
            grid=(nb,),
            in_specs=[rows_spec, rows_spec,
                      pl.BlockSpec((1, d, ff), lambda i, be_r, nv_r: (layer * e_num + be_r[i], 0, 0)),
                      pl.BlockSpec((1, d, ff), lambda i, be_r, nv_r: (layer * e_num + be_r[i], 0, 0)),
                      pl.BlockSpec((1, ff, d), lambda i, be_r, nv_r: (layer * e_num + be_r[i], 0, 0))],
            out_specs=[rows_spec, rows_spec],
            scratch_shapes=[pltpu.VMEM((d, ff), BF16), pltpu.VMEM((d, ff), BF16), pltpu.VMEM((ff, d), BF16)],
        ),
        out_shape=[jax.ShapeDtypeStruct((n_rows, quarter), I32)] * 2,
        compiler_params=_cparams(("arbitrary",)),
        name="moe_experts",
    )(blk_expert, blk_valid, xa, xb, w1, w3, w2)

    dest12 = jnp.concatenate([dest1, dest2], axis=1)
    ga = _sc_gather_rows(ya, dest12)
    gb = _sc_gather_rows(yb, dest12)

    first = pl.BlockSpec((ts, quarter), lambda b, i: (b * nt_b + i, 0))
    second = pl.BlockSpec((ts, quarter), lambda b, i: (nt + b * nt_b + i, 0))
    return pl.pallas_call(
        functools.partial(_combine_kernel, alpha),
        grid=(bsz, nt_b),
        in_specs=[tile, vec, flat(LANES), _const_spec((1, d)), _const_spec((1, d)),
                  first, first, second, second],
        out_specs=tile,
        out_shape=jax.ShapeDtypeStruct((bsz, s, d), F32),
        compiler_params=_cparams(("arbitrary", "arbitrary")),
        name="moe_combine",
    )(x, g, mf, ln_g.reshape(1, d), ln_b.reshape(1, d), ga, gb, ga, gb)


def kernel(x, c, ada_w, ada_b, ln_g, ln_b, hgrn_w_in, hgrn_w_out, hgrn_lb, hgrn_norm_w, attn_w_in, attn_w_out,
           attn_lambda, attn_subln_w, router_g_w, router_g_b, router_e_w, router_e_b, moe_w1, moe_w3, moe_w2):
    depth = ada_w.shape[0]
    bsz, s, d = x.shape
    alpha = (2 * depth) ** 0.25
    lb_all = jnp.cumsum(jax.nn.softmax(hgrn_lb.astype(F32), axis=0), axis=0)
    lb_all = lb_all - lb_all[0:1]
    mod = _ada_mod(c, ada_w, ada_b).reshape(depth, bsz, 6, 1, d)
    w1_all = moe_w1.reshape((-1,) + moe_w1.shape[2:])
    w3_all = moe_w3.reshape((-1,) + moe_w3.shape[2:])
    w2_all = moe_w2.reshape((-1,) + moe_w2.shape[2:])
    for i in range(depth):
        sh1, sc1, g1, sh2, sc2, g2 = (mod[i, :, m] for m in range(6))
        j = i // 2
        if i % 2 == 0:
            x = _hgrn_layer(x, sh1, sc1, g1, hgrn_w_in[j], hgrn_w_out[j], lb_all[j], hgrn_norm_w[j],
                            ln_g[i, 0], ln_b[i, 0], alpha)
        else:
            lambda_init = 0.8 - 0.6 * math.exp(-0.3 * i)
            x = _attn_layer(x, sh1, sc1, g1, attn_w_in[j], attn_w_out[j], attn_lambda[j], attn_subln_w[j],
                            lambda_init, ln_g[i, 0], ln_b[i, 0], alpha)
        x = _moe_layer(x, sh2, sc2, g2, router_g_w[i], router_g_b[i], router_e_w[i], router_e_b[i],
                       w1_all, w3_all, w2_all, i, ln_g[i, 1], ln_b[i, 1], alpha)
    return x
```

```python
import functools
import math

import jax
import jax.numpy as jnp
from jax import lax
from jax.experimental import pallas as pl
from jax.experimental.pallas import tpu as pltpu
from jax.experimental.pallas import tpu_sc as plsc

F32 = jnp.float32
BF16 = jnp.bfloat16
I32 = jnp.int32
U32 = jnp.uint32
HIGHEST = lax.Precision.HIGHEST

LANES = 128
HEAD = 128
HGRN_CHUNK = 32
HGRN_GROUP = 2
HGRN_SAFE_SPAN = 80.0
ATTN_D = 64
FLASH_HEADS = 2
FLASH_ROW_CHUNK = 32
ROPE_THETA = 10000.0
MOE_GROUPS = 4
MOE_EPG = 8
MOE_EXPERTS = MOE_GROUPS * MOE_EPG
MOE_BLOCK = 512
SC_WINDOW = 128
NORM_EPS = 1e-5
VMEM_LIMIT = 56 * 1024 * 1024

NT_DIMS = (((1,), (1,)), ((), ()))
TN_DIMS = (((0,), (0,)), ((), ()))


def _cparams(sem):
    return pltpu.CompilerParams(dimension_semantics=sem, vmem_limit_bytes=VMEM_LIMIT)


def _const_spec(shape):
    nd = len(shape)
    return pl.BlockSpec(shape, lambda *_: (0,) * nd)


def _layer_norm(r, g, b):
    mu = jnp.mean(r, axis=-1, keepdims=True)
    d = r - mu
    var = jnp.mean(d * d, axis=-1, keepdims=True)
    return d * lax.rsqrt(var + NORM_EPS) * g + b


def _silu(x):
    return x * (1.0 / (1.0 + jnp.exp(-x)))


def _pack_bf16_pair(lo, hi):
    lo_b = lax.bitcast_convert_type(lo.astype(BF16).astype(F32), U32)
    hi_b = lax.bitcast_convert_type(hi.astype(BF16).astype(F32), U32)
    return (hi_b & jnp.uint32(0xFFFF0000)) | (lo_b >> 16)


def _unpack_bf16_pair(u):
    lo = lax.bitcast_convert_type(u << 16, F32)
    hi = lax.bitcast_convert_type(u & jnp.uint32(0xFFFF0000), F32)
    return lo, hi


def _ada_kernel(c_ref, w_ref, b_ref, o_ref):
    c = c_ref[...]
    o_ref[0] = jnp.dot(_silu(c), w_ref[0], precision=HIGHEST, preferred_element_type=F32) + b_ref[0]


def _ada_mod(c, ada_w, ada_b):
    depth, d, n6 = ada_w.shape
    bsz = c.shape[0]
    tn = d
    return pl.pallas_call(
        _ada_kernel,
        grid=(depth, n6 // tn),
        in_specs=[
            pl.BlockSpec((bsz, d), lambda i, j: (0, 0)),
            pl.BlockSpec((1, d, tn), lambda i, j: (i, 0, j)),
            pl.BlockSpec((1, 1, tn), lambda i, j: (i, 0, j)),
        ],
        out_specs=pl.BlockSpec((1, bsz, tn), lambda i, j: (i, 0, j)),
        out_shape=jax.ShapeDtypeStruct((depth, bsz, n6), F32),
        compiler_params=_cparams(("arbitrary", "arbitrary")),
        name="ada_mod",
    )(c, ada_w, ada_b.reshape(depth, 1, n6))


def _hgrn_kernel(alpha, x_ref, sh_ref, sc_ref, g_ref, win_ref, wout_ref, loglb_ref, oml_ref,
                 nw_ref, lng_ref, lnb_ref, o_ref,
                 proj_ref, st_ref, ocat_ref, b_ref, kk_ref, qi_ref, ks_ref, vb_ref, dec_ref):
    ts, d = x_ref.shape[1], x_ref.shape[2]
    nh = d // HEAD
    c = HGRN_CHUNK
    nc = ts // c
    gw = min(HGRN_GROUP * HEAD, d)
    ng = d // gw
    hpg = gw // HEAD

    @pl.when(pl.program_id(1) == 0)
    def _():
        st_ref[...] = jnp.zeros_like(st_ref)

    x = x_ref[0]
    hb = (x * (1.0 + sc_ref[0]) + sh_ref[0]).astype(BF16)
    pos = lax.broadcasted_iota(I32, (ts, 1), 0) % c
    row = lax.broadcasted_iota(I32, (ts, ts), 0)
    col = lax.broadcasted_iota(I32, (ts, ts), 1)
    keep = (row >= col) & (row // c == col // c)
    span = jnp.float32(0.0)

    def project(g):
        ps = slice(g * 4 * gw, (g + 1) * 4 * gw)
        proj_ref[:, ps] = jnp.dot(hb, win_ref[:, ps], preferred_element_type=F32)

    project(0)
    for g in range(ng):
        if g + 1 < ng:
            project(g + 1)
        cs = slice(g * gw, (g + 1) * gw)
        p0 = g * 4 * gw
        q2 = proj_ref[:, p0:p0 + gw]
        z = proj_ref[:, p0 + gw:p0 + 2 * gw]

        ls = jnp.minimum(z, 0.0) - jnp.log(1.0 + jnp.exp(-jnp.abs(z)))
        lsn = ls - z
        cc = loglb_ref[:, cs] + lsn
        log_f = jnp.maximum(ls, cc) + jnp.log(1.0 + jnp.exp(-jnp.abs(ls - cc)))
        kk = oml_ref[:, cs] * jnp.exp(lsn)

        b = log_f
        step = 1
        while step < c:
            b = b + jnp.where(pos >= step, pltpu.roll(b, step, 0), 0.0)
            step *= 2
        b_ref[:, cs] = b
        kk_ref[:, cs] = kk

        b3 = b.reshape(nc, c, gw)
        b_last = b3[:, c - 1:c, :]
        b_mid = b3[:, c // 2 - 1:c // 2, :]
        q3 = q2.reshape(nc, c, gw)
        k3 = kk.reshape(nc, c, gw)
        qi_ref[:, cs] = (q3 * jnp.exp(b3)).astype(BF16).reshape(ts, gw)
        ks_ref[:, cs] = (k3 * jnp.exp(b_last - b3)).astype(BF16).reshape(ts, gw)
        q_intra = (q3 * jnp.exp(b3 - b_mid)).astype(BF16).reshape(ts, gw)
        k_intra = (k3 * jnp.exp(b_mid - b3)).astype(BF16).reshape(ts, gw)
        dec_ref[:, cs] = jnp.exp(b_last).reshape(nc, gw)
        span = jnp.maximum(span, jnp.max(-b_last))
        v2 = proj_ref[:, p0 + 2 * gw:p0 + 3 * gw].astype(BF16)
        vb_ref[:, cs] = v2

        for hh in range(hpg):
            sl = slice(hh * HEAD, (hh + 1) * HEAD)
            sc = lax.dot_general(q_intra[:, sl], k_intra[:, sl], NT_DIMS, preferred_element_type=F32)
            p = jnp.where(keep, sc, 0.0).astype(BF16)
            ocat_ref[:, g * gw + hh * HEAD:g * gw + (hh + 1) * HEAD] = jnp.dot(
                p, v2[:, sl], preferred_element_type=F32)

    @pl.when(span > HGRN_SAFE_SPAN)
    def _():
        b = b_ref[...]
        kk = kk_ref[...]
        q2 = jnp.concatenate([proj_ref[:, g * 4 * gw:g * 4 * gw + gw] for g in range(ng)], axis=1)
        diag = q2 * kk
        tpos = lax.broadcasted_iota(I32, (ts, 1), 0)
        scores = [jnp.where(row == col, jnp.sum(diag[:, hd * HEAD:(hd + 1) * HEAD], axis=-1, keepdims=True), 0.0)
                  for hd in range(nh)]
        size = 2
        while size <= c:
            ref_row = (row // size) * size + (size // 2 - 1)
            pick = jnp.where(col == ref_row, 1.0, 0.0)
            b_at = jnp.dot(pick, b, precision=HIGHEST, preferred_element_type=F32)
            right = (tpos % size) >= (size // 2)
            qa = jnp.where(right, q2 * jnp.exp(b - b_at), 0.0).astype(BF16)
            ka = jnp.where(right, 0.0, kk * jnp.exp(b_at - b)).astype(BF16)
            same = (row // size) == (col // size)
            for hd in range(nh):
                sl = slice(hd * HEAD, (hd + 1) * HEAD)
                sc = lax.dot_general(qa[:, sl], ka[:, sl], NT_DIMS, preferred_element_type=F32)
                scores[hd] = scores[hd] + jnp.where(same, sc, 0.0)
            size *= 2
        for hd in range(nh):
            sl = slice(hd * HEAD, (hd + 1) * HEAD)
            ocat_ref[:, sl] = jnp.dot(scores[hd].astype(BF16), vb_ref[:, sl], preferred_element_type=F32)

    for ci in range(nc):
        rs = slice(ci * c, (ci + 1) * c)
        for hd in range(nh):
            sl = slice(hd * HEAD, (hd + 1) * HEAD)
            st = st_ref[hd]
            o_inter = lax.dot_general(qi_ref[rs, sl], st.astype(BF16), NT_DIMS,
                                      preferred_element_type=F32)
            ocat_ref[rs, sl] += o_inter
            upd = lax.dot_general(vb_ref[rs, sl], ks_ref[rs, sl], TN_DIMS,
                                  preferred_element_type=F32)
            st_ref[hd] = st * dec_ref[ci:ci + 1, sl] + upd

    for hd in range(nh):
        sl = slice(hd * HEAD, (hd + 1) * HEAD)
        g, hh = divmod(hd, hpg)
        oh = ocat_ref[:, sl]
        ms = jnp.mean(oh * oh, axis=-1, keepdims=True)
        gate = proj_ref[:, g * 4 * gw + 3 * gw + hh * HEAD:g * 4 * gw + 3 * gw + (hh + 1) * HEAD]
        ocat_ref[:, sl] = oh * lax.rsqrt(ms + NORM_EPS) * nw_ref[...] * _silu(gate)
    y = jnp.dot(ocat_ref[...].astype(BF16), wout_ref[...], preferred_element_type=F32)
    r = alpha * x + g_ref[0] * y
    o_ref[0] = _layer_norm(r, lng_ref[...], lnb_ref[...])


def _hgrn_layer(x, sh, sc, g, w_in, w_out, lb, norm_w, ln_g, ln_b, alpha):
    bsz, s, d = x.shape
    ts = min(256, s)
    nh = d // HEAD
    gw = min(HGRN_GROUP * HEAD, d)
    ng = d // gw
    w_grouped = w_in.reshape(d, 4, ng, gw).transpose(0, 2, 1, 3).reshape(d, 4 * d).astype(BF16)
    vec = pl.BlockSpec((1, 1, d), lambda b, i: (b, 0, 0))
    tile = pl.BlockSpec((1, ts, d), lambda b, i: (b, i, 0))
    return pl.pallas_call(
        functools.partial(_hgrn_kernel, alpha),
        grid=(bsz, s // ts),
        in_specs=[tile, vec, vec, vec,
                  _const_spec((d, 4 * d)), _const_spec((d, d)),
                  _const_spec((1, d)), _const_spec((1, d)), _const_spec((1, HEAD)),
                  _const_spec((1, d)), _const_spec((1, d))],
        out_specs=tile,
        out_shape=jax.ShapeDtypeStruct((bsz, s, d), F32),
        scratch_shapes=[pltpu.VMEM((ts, 4 * d), F32),
                        pltpu.VMEM((nh, HEAD, HEAD), F32),
                        pltpu.VMEM((ts, d), F32),
                        pltpu.VMEM((ts, d), F32), pltpu.VMEM((ts, d), F32),
                        pltpu.VMEM((ts, d), BF16), pltpu.VMEM((ts, d), BF16), pltpu.VMEM((ts, d), BF16),
                        pltpu.VMEM((ts // HGRN_CHUNK, d), F32)],
        compiler_params=_cparams(("arbitrary", "arbitrary")),
        name="hgrn_layer",
    )(x, sh, sc, g, w_grouped, w_out.astype(BF16),
      jnp.log(lb).reshape(1, d), (1.0 - lb).reshape(1, d), norm_w.reshape(1, HEAD),
      ln_g.reshape(1, d), ln_b.reshape(1, d))


def _attn_in_kernel(x_ref, sh_ref, sc_ref, w_ref, cos_ref, sin_ref, q_ref, k_ref, v_ref):
    d = x_ref.shape[2]
    x = x_ref[0]
    h = x * (1.0 + sc_ref[0]) + sh_ref[0]
    qkv = jnp.dot(h.astype(BF16), w_ref[...], preferred_element_type=F32)
    cos = cos_ref[...]
    sin = sin_ref[...]
    lane = lax.broadcasted_iota(I32, (1, LANES), 1)
    first_half = (lane % ATTN_D) < (ATTN_D // 2)
    scale = ATTN_D ** -0.5 * math.log2(math.e)
    for j in range(d // LANES):
        sl = slice(j * LANES, (j + 1) * LANES)
        for src, dst, mul in ((0, q_ref, scale), (d, k_ref, 1.0)):
            t = qkv[:, src + j * LANES:src + (j + 1) * LANES]
            partner = jnp.where(first_half, pltpu.roll(t, LANES - ATTN_D // 2, 1),
                                pltpu.roll(t, ATTN_D // 2, 1))
            dst[0, :, sl] = ((t * cos + partner * sin) * mul).astype(BF16)
    v_ref[0] = qkv[:, 2 * d:].astype(BF16)


def _flash_kernel(out_scale, lam_ref, q_ref, k_ref, v_ref, w_ref, o_ref, *scratch):
    tq = q_ref.shape[1]
    rows = 2 * tq
    qi = pl.program_id(2)
    nrep = tq // LANES
    rc = FLASH_ROW_CHUNK
    per_head = len(scratch) // FLASH_HEADS
    heads = [(hd,) + tuple(scratch[hd * per_head:(hd + 1) * per_head]) for hd in range(FLASH_HEADS)]
    lane = lax.broadcasted_iota(I32, (1, LANES), 1)

    for hd, m_ref, l_ref, acc_ref, _, _, _, qq_ref in heads:
        q = q_ref[0, :, hd * HEAD:(hd + 1) * HEAD]
        zero = jnp.zeros_like(q)
        qq_ref[0:tq, :] = jnp.where(lane < ATTN_D, q, zero)
        qq_ref[tq:rows, :] = jnp.where(lane >= ATTN_D, q, zero)
        m_ref[...] = jnp.full_like(m_ref, -jnp.inf)
        l_ref[...] = jnp.zeros_like(l_ref)
        acc_ref[...] = jnp.zeros_like(acc_ref)

    def scores(head, j):
        hd, _, _, _, _, s_ref, _, qq_ref = head
        kb = k_ref[0, pl.ds(pl.multiple_of(j * tq, tq), tq), hd * HEAD:(hd + 1) * HEAD]
        s_ref[...] = lax.dot_general(qq_ref[...], kb, NT_DIMS, preferred_element_type=F32)

    def softmax(head, masked):
        _, m_ref, l_ref, _, a_ref, s_ref, p_ref, _ = head
        for r0 in range(0, rows, rc):
            rs = slice(r0, r0 + rc)
            s = s_ref[rs, :]
            if masked:
                row = lax.broadcasted_iota(I32, (rc, tq), 0) + (r0 % tq)
                col = lax.broadcasted_iota(I32, (rc, tq), 1)
                s = jnp.where(row >= col, s, -jnp.inf)
            m_old = m_ref[rs, :]
            m_new = jnp.maximum(m_old, jnp.max(s, axis=-1, keepdims=True))
            a = jnp.exp2(m_old - m_new)
            p = jnp.exp2(s - jnp.concatenate([m_new] * nrep, axis=1))
            psum = p[:, 0:LANES]
            for r in range(1, nrep):
                psum = psum + p[:, r * LANES:(r + 1) * LANES]
            l_ref[rs, :] = a * l_ref[rs, :] + psum
            m_ref[rs, :] = m_new
            a_ref[rs, :] = a
            p_ref[rs, :] = p.astype(BF16)

    def values(head, j):
        hd, _, _, acc_ref, a_ref, _, p_ref, _ = head
        vb = v_ref[0, pl.ds(pl.multiple_of(j * tq, tq), tq), hd * HEAD:(hd + 1) * HEAD]
        acc_ref[...] = a_ref[...] * acc_ref[...] + jnp.dot(p_ref[...], vb, preferred_element_type=F32)

    head_a, head_b = heads
    n_blk = qi + 1
    scores(head_a, 0)
    scores(head_b, 0)

    @pl.when(n_blk == 1)
    def _():
        softmax(head_a, True)
        softmax(head_b, True)
        values(head_a, 0)
        values(head_b, 0)

    @pl.when(n_blk >= 2)
    def _():
        softmax(head_a, False)

        def body(t, carry):
            scores(head_a, t + 1)
            softmax(head_b, False)
            values(head_a, t)
            scores(head_b, t + 1)
            softmax(head_a, False)
            values(head_b, t)
            return carry

        lax.fori_loop(0, n_blk - 2, body, 0)
        last = n_blk - 1
        scores(head_a, last)
        softmax(head_b, False)
        values(head_a, last - 1)
        scores(head_b, last)
        softmax(head_a, True)
        values(head_b, last - 1)
        softmax(head_b, True)
        values(head_a, last)
        values(head_b, last)

    for hd, _, l_ref, acc_ref, _, _, _, _ in heads:
        o_all = acc_ref[...] / jnp.sum(l_ref[...], axis=-1, keepdims=True)
        o = o_all[:tq] - lam_ref[0] * o_all[tq:]
        ms = jnp.mean(o * o, axis=-1, keepdims=True)
        o_ref[0, :, hd * HEAD:(hd + 1) * HEAD] = (
            o * lax.rsqrt(ms + NORM_EPS) * w_ref[...] * out_scale).astype(BF16)


def _resid_ln_kernel(alpha, x_ref, g_ref, o_in_ref, w_ref, lng_ref, lnb_ref, o_ref):
    y = jnp.dot(o_in_ref[0], w_ref[...], preferred_element_type=F32)
    r = alpha * x_ref[0] + g_ref[0] * y
    o_ref[0] = _layer_norm(r, lng_ref[...], lnb_ref[...])


def _attn_layer(x, sh, sc, g, w_in, w_out, lam_params, subln_w, lambda_init, ln_g, ln_b, alpha):
    bsz, s, d = x.shape
    nh = d // HEAD
    ts = min(512, s)
    tq = min(512, s)
    half = ATTN_D // 2
    inv_freq = ROPE_THETA ** (-jnp.arange(half, dtype=F32) / half)
    ang = jnp.arange(s, dtype=F32)[:, None] * inv_freq[None, :]
    cos_t = jnp.tile(jnp.cos(ang), (1, LANES // half))
    sin_h = jnp.sin(ang)
    sin_t = jnp.tile(jnp.concatenate([-sin_h, sin_h], axis=1), (1, LANES // ATTN_D))

    vec = pl.BlockSpec((1, 1, d), lambda b, i: (b, 0, 0))
    tile = pl.BlockSpec((1, ts, d), lambda b, i: (b, i, 0))
    rope = pl.BlockSpec((ts, LANES), lambda b, i: (i, 0))
    q, k, v = pl.pallas_call(
        _attn_in_kernel,
        grid=(bsz, s // ts),
        in_specs=[tile, vec, vec, _const_spec((d, 3 * d)), rope, rope],
        out_specs=[tile, tile, tile],
        out_shape=[jax.ShapeDtypeStruct((bsz, s, d), BF16)] * 3,
        compiler_params=_cparams(("arbitrary", "arbitrary")),
        name="attn_in",
    )(x, sh, sc, w_in.astype(BF16), cos_t, sin_t)

    lp = lam_params.astype(F32)
    lam = (jnp.exp(jnp.sum(lp[0] * lp[1])) - jnp.exp(jnp.sum(lp[2] * lp[3])) + lambda_init).reshape(1)
    hw = FLASH_HEADS * HEAD
    qspec = pl.BlockSpec((1, tq, hw), lambda b, h, i: (b, i, h))
    kvspec = pl.BlockSpec((1, s, hw), lambda b, h, i: (b, 0, h))
    head_scratch = [pltpu.VMEM((2 * tq, LANES), F32), pltpu.VMEM((2 * tq, LANES), F32),
                    pltpu.VMEM((2 * tq, HEAD), F32), pltpu.VMEM((2 * tq, LANES), F32),
                    pltpu.VMEM((2 * tq, tq), F32), pltpu.VMEM((2 * tq, tq), BF16),
                    pltpu.VMEM((2 * tq, HEAD), BF16)]
    o = pl.pallas_call(
        functools.partial(_flash_kernel, 1.0 - lambda_init),
        grid=(bsz, nh // FLASH_HEADS, s // tq),
        in_specs=[pl.BlockSpec(memory_space=pltpu.SMEM), qspec, kvspec, kvspec,
                  pl.BlockSpec((1, HEAD), lambda b, h, i: (0, 0))],
        out_specs=qspec,
        out_shape=jax.ShapeDtypeStruct((bsz, s, d), BF16),
        scratch_shapes=head_scratch * FLASH_HEADS,
        compiler_params=_cparams(("arbitrary", "arbitrary", "arbitrary")),
        name="diff_flash",
    )(lam, q, k, v, subln_w.reshape(1, HEAD))

    return pl.pallas_call(
        functools.partial(_resid_ln_kernel, alpha),
        grid=(bsz, s // ts),
        in_specs=[tile, vec, tile, _const_spec((d, d)), _const_spec((1, d)), _const_spec((1, d))],
        out_specs=tile,
        out_shape=jax.ShapeDtypeStruct((bsz, s, d), F32),
        compiler_params=_cparams(("arbitrary", "arbitrary")),
        name="attn_out",
    )(x, g, o, w_out.astype(BF16), ln_g.reshape(1, d), ln_b.reshape(1, d))


def _pack_row_parts(v):
    q = v.shape[1] // 4
    return tuple(lax.bitcast_convert_type(_pack_bf16_pair(v[:, p * q:(p + 1) * q], v[:, (2 + p) * q:(3 + p) * q]), I32)
                 for p in range(2))


def _unpack_row_parts(part_a, part_b):
    lo_a, hi_a = _unpack_bf16_pair(lax.bitcast_convert_type(part_a, U32))
    lo_b, hi_b = _unpack_bf16_pair(lax.bitcast_convert_type(part_b, U32))
    return lo_a, lo_b, hi_a, hi_b


def _router_kernel(x_ref, sh_ref, sc_ref, w_ref, bias_ref, tri_ref, hpa_ref, hpb_ref, mt_ref, mf_ref, cnt_ref,
                   cnt_scr):
    ts, d = x_ref.shape[1], x_ref.shape[2]
    first = (pl.program_id(0) == 0) & (pl.program_id(1) == 0)

    @pl.when(first)
    def _():
        cnt_scr[...] = jnp.zeros_like(cnt_scr)

    x = x_ref[0]
    h = x * (1.0 + sc_ref[0]) + sh_ref[0]
    hpa_ref[...], hpb_ref[...] = _pack_row_parts(h)

    h_hi = h.astype(BF16)
    h_lo = (h - h_hi.astype(F32)).astype(BF16)
    hh = jnp.dot(h_hi, w_ref[...], preferred_element_type=F32)
    lh = jnp.dot(h_lo, w_ref[:, 0:LANES], preferred_element_type=F32)
    logits = hh[:, 0:LANES] + hh[:, LANES:2 * LANES] + lh + bias_ref[...]
    lane = lax.broadcasted_iota(I32, (ts, LANES), 1)
    neg = -jnp.inf
    big = jnp.int32(LANES)
    is_g = lane < MOE_GROUPS
    gl = jnp.where(is_g, logits, neg)
    gmax = jnp.max(gl, axis=-1, keepdims=True)
    g_idx = jnp.min(jnp.where(gl == gmax, lane, big), axis=-1, keepdims=True)
    g_w = 1.0 / jnp.sum(jnp.exp(gl - gmax), axis=-1, keepdims=True)

    e_lane = lane - MOE_GROUPS
    in_grp = (e_lane >= g_idx * MOE_EPG) & (e_lane < (g_idx + 1) * MOE_EPG)
    el = jnp.where(in_grp, logits, neg)
    l1 = jnp.max(el, axis=-1, keepdims=True)
    i1 = jnp.min(jnp.where(el == l1, lane, big), axis=-1, keepdims=True)
    el2 = jnp.where(lane == i1, neg, el)
    l2 = jnp.max(el2, axis=-1, keepdims=True)
    i2 = jnp.min(jnp.where(el2 == l2, lane, big), axis=-1, keepdims=True)
    t = jnp.exp(l2 - l1)
    w1 = g_w / (1.0 + t)
    w2 = g_w * t / (1.0 + t)

    oh1 = (lane == i1)
    oh2 = (lane == i2)
    both = (oh1 | oh2).astype(BF16)
    before = jnp.dot(tri_ref[...], both, preferred_element_type=F32) + cnt_scr[...]
    r1 = jnp.sum(jnp.where(oh1, before, 0.0), axis=-1, keepdims=True)
    r2 = jnp.sum(jnp.where(oh2, before, 0.0), axis=-1, keepdims=True)
    cnt_scr[...] = cnt_scr[...] + jnp.sum(both.astype(F32), axis=0, keepdims=True)
    cnt_ref[...] = cnt_scr[...]

    e1 = i1 - MOE_GROUPS
    e2 = i2 - MOE_GROUPS
    meta = jnp.where(lane == 0, e1, jnp.where(lane == 1, e2, jnp.where(
        lane == 2, r1.astype(I32), jnp.where(lane == 3, r2.astype(I32), 0))))
    mt_ref[...] = jnp.transpose(meta)[0:8, :]
    mf_ref[...] = jnp.where(lane == 0, w1, jnp.where(lane == 1, w2, 0.0))


def _dest_kernel(ps_ref, mt_ref, d1_ref, d2_ref):
    e1 = mt_ref[0:1, :]
    e2 = mt_ref[1:2, :]
    p1 = jnp.zeros_like(e1)
    p2 = jnp.zeros_like(e2)
    for e in range(MOE_EXPERTS):
        p1 = jnp.where(e1 == e, ps_ref[e], p1)
        p2 = jnp.where(e2 == e, ps_ref[e], p2)
    d1_ref[...] = p1 + mt_ref[2:3, :]
    d2_ref[...] = p2 + mt_ref[3:4, :]


def _sc_mesh():
    return plsc.VectorSubcoreMesh(core_axis_name="c", subcore_axis_name="s")


def _sc_scatter_rows(src, idx_a, idx_b, n_rows):
    m, w = src.shape

    @functools.partial(pl.kernel, out_type=jax.ShapeDtypeStruct((n_rows, w), src.dtype),
                       mesh=_sc_mesh(), scratch_types=[], name="moe_sc_scatter")
    def scatter(x_hbm, ia_hbm, ib_hbm, o_hbm):
        def body(x_vmem, ia_vmem, ib_vmem):
            pltpu.sync_copy(x_vmem, o_hbm.at[ia_vmem.at[0]])
            pltpu.sync_copy(x_vmem, o_hbm.at[ib_vmem.at[0]])

        pltpu.emit_pipeline(
            body,
            grid=(m // SC_WINDOW,),
            in_specs=[pl.BlockSpec((SC_WINDOW, w), lambda i: (i, 0)),
                      pl.BlockSpec((1, SC_WINDOW), lambda i: (0, i)),
                      pl.BlockSpec((1, SC_WINDOW), lambda i: (0, i))],
            out_specs=[],
            core_axis_name=("c", "s"),
            dimension_semantics=(pltpu.PARALLEL,),
        )(x_hbm, ia_hbm, ib_hbm)

    return scatter(src, idx_a, idx_b)


def _sc_gather_rows(table, idx):
    m = idx.shape[1]
    w = table.shape[1]

    @functools.partial(pl.kernel, out_type=jax.ShapeDtypeStruct((m, w), table.dtype),
                       mesh=_sc_mesh(), scratch_types=[], name="moe_sc_gather")
    def gather(t_hbm, i_hbm, o_hbm):
        def body(i_vmem, o_vmem):
            pltpu.sync_copy(t_hbm.at[i_vmem.at[0]], o_vmem)

        pltpu.emit_pipeline(
            body,
            grid=(m // SC_WINDOW,),
            in_specs=[pl.BlockSpec((1, SC_WINDOW), lambda i: (0, i))],
            out_specs=[pl.BlockSpec((SC_WINDOW, w), lambda i: (i, 0))],
            core_axis_name=("c", "s"),
            dimension_semantics=(pltpu.PARALLEL,),
        )(i_hbm, o_hbm)

    return gather(table, idx)


def _expert_kernel(be_ref, nv_ref, xa_ref, xb_ref, w1_ref, w3_ref, w2_ref, ya_ref, yb_ref,
                   w1_scr, w3_scr, w2_scr):
    i = pl.program_id(0)
    blk = xa_ref.shape[0]
    n_valid = nv_ref[i]
    new_expert = (i == 0) | (be_ref[i] != be_ref[jnp.maximum(i - 1, 0)])

    @pl.when((n_valid > 0) & new_expert)
    def _():
        w1_scr[...] = w1_ref[0].astype(BF16)
        w3_scr[...] = w3_ref[0].astype(BF16)
        w2_scr[...] = w2_ref[0].astype(BF16)

    @pl.when(n_valid > 0)
    def _():
        valid = lax.broadcasted_iota(I32, (blk, 1), 0) < n_valid
        xq = _unpack_row_parts(xa_ref[...], xb_ref[...])
        xin = jnp.concatenate([jnp.where(valid, q, 0.0).astype(BF16) for q in xq], axis=1)
        a = jnp.dot(xin, w1_scr[...], preferred_element_type=F32)
        b = jnp.dot(xin, w3_scr[...], preferred_element_type=F32)
        hid = (_silu(a) * b).astype(BF16)
        y = jnp.dot(hid, w2_scr[...], preferred_element_type=F32)
        ya_ref[...], yb_ref[...] = _pack_row_parts(y)

    @pl.when(n_valid <= 0)
    def _():
        ya_ref[...] = jnp.zeros_like(ya_ref)
        yb_ref[...] = jnp.zeros_like(yb_ref)


def _combine_kernel(alpha, x_ref, g_ref, mf_ref, lng_ref, lnb_ref, y1a_ref, y1b_ref, y2a_ref, y2b_ref, o_ref):
    d = x_ref.shape[2]
    q = d // 4
    w1 = mf_ref[:, 0:1]
    w2 = mf_ref[:, 1:2]
    y1 = _unpack_row_parts(y1a_ref[...], y1b_ref[...])
    y2 = _unpack_row_parts(y2a_ref[...], y2b_ref[...])
    r = [alpha * x_ref[0, :, p * q:(p + 1) * q] + g_ref[0, :, p * q:(p + 1) * q] * (w1 * y1[p] + w2 * y2[p])
         for p in range(4)]
    mu = sum(jnp.sum(rp, axis=-1, keepdims=True) for rp in r) / d
    dev = [rp - mu for rp in r]
    var = sum(jnp.sum(dp * dp, axis=-1, keepdims=True) for dp in dev) / d
    inv = lax.rsqrt(var + NORM_EPS)
    for p in range(4):
        sl = slice(p * q, (p + 1) * q)
        o_ref[0, :, sl] = dev[p] * inv * lng_ref[:, sl] + lnb_ref[:, sl]


def _moe_layer(x, sh, sc, g, wg, bg, we, be, w1, w3, w2, layer, ln_g, ln_b, alpha):
    bsz, s, d = x.shape
    n = bsz * s
    quarter = d // 4
    e_num, blk = MOE_EXPERTS, MOE_BLOCK
    ff = w1.shape[-1]
    ts = min(512, s)
    nt_b = s // ts
    nt = n // ts
    n_rows = n * 2 + e_num * blk
    nb = n_rows // blk

    wcat = jnp.zeros((d, LANES), F32).at[:, :MOE_GROUPS].set(wg).at[:, MOE_GROUPS:MOE_GROUPS + e_num].set(we)
    bcat = jnp.zeros((1, LANES), F32).at[0, :MOE_GROUPS].set(bg).at[0, MOE_GROUPS:MOE_GROUPS + e_num].set(be)
    wcat_hi = wcat.astype(BF16)
    wcat_hl = jnp.concatenate([wcat_hi, (wcat - wcat_hi.astype(F32)).astype(BF16)], axis=1)
    tri = jnp.tri(ts, k=-1, dtype=BF16)

    vec = pl.BlockSpec((1, 1, d), lambda b, i: (b, 0, 0))
    tile = pl.BlockSpec((1, ts, d), lambda b, i: (b, i, 0))
    flat = lambda w: pl.BlockSpec((ts, w), lambda b, i: (b * nt_b + i, 0))
    hpa, hpb, mt, mf, cnt = pl.pallas_call(
        _router_kernel,
        grid=(bsz, nt_b),
        in_specs=[tile, vec, vec, _const_spec((d, 2 * LANES)), _const_spec((1, LANES)), _const_spec((ts, ts))],
        out_specs=[flat(quarter), flat(quarter), pl.BlockSpec((8, ts), lambda b, i: (0, b * nt_b + i)),
                   flat(LANES), _const_spec((1, LANES))],
        out_shape=[jax.ShapeDtypeStruct((n, quarter), I32), jax.ShapeDtypeStruct((n, quarter), I32),
                   jax.ShapeDtypeStruct((8, n), I32), jax.ShapeDtypeStruct((n, LANES), F32),
                   jax.ShapeDtypeStruct((1, LANES), F32)],
        scratch_shapes=[pltpu.VMEM((1, LANES), F32)],
        compiler_params=_cparams(("arbitrary", "arbitrary")),
        name="moe_router",
    )(x, sh, sc, wcat_hl, bcat, tri)

    counts = cnt[0, MOE_GROUPS:MOE_GROUPS + e_num].astype(I32)
    padded = ((counts + blk - 1) // blk) * blk
    pad_end = jnp.cumsum(padded)
    pad_start = pad_end - padded
    blk_start = jnp.arange(nb, dtype=I32) * blk
    blk_expert = jnp.minimum(jnp.sum((pad_end[None, :] <= blk_start[:, None]).astype(I32), axis=1), e_num - 1)
    blk_valid = jnp.clip(pad_start[blk_expert] + counts[blk_expert] - blk_start, 0, blk).astype(I32)

    td = min(8192, n)
    dest1, dest2 = pl.pallas_call(
        _dest_kernel,
        grid_spec=pltpu.PrefetchScalarGridSpec(
            num_scalar_prefetch=1,
            grid=(n // td,),
            in_specs=[pl.BlockSpec((8, td), lambda i, ps_r: (0, i))],
            out_specs=[pl.BlockSpec((1, td), lambda i, ps_r: (0, i))] * 2,
        ),
        out_shape=[jax.ShapeDtypeStruct((1, n), I32)] * 2,
        compiler_params=_cparams(("arbitrary",)),
        name="moe_dest",
    )(pad_start.astype(I32), mt)

    xa = _sc_scatter_rows(hpa, dest1, dest2, n_rows)
    xb = _sc_scatter_rows(hpb, dest1, dest2, n_rows)

    rows_spec = pl.BlockSpec((blk, quarter), lambda i, be_r, nv_r: (i, 0))
    ya, yb = pl.pallas_call(
        _expert_kernel,
        grid_spec=pltpu.PrefetchScalarGridSpec(
            num_scalar_prefetch=2,
            grid=(nb,),
            in_specs=[rows_spec, rows_spec,
                      pl.BlockSpec((1, d, ff), lambda i, be_r, nv_r: (layer * e_num + be_r[i], 0, 0)),
                      pl.BlockSpec((1, d, ff), lambda i, be_r, nv_r: (layer * e_num + be_r[i], 0, 0)),
                      pl.BlockSpec((1, ff, d), lambda i, be_r, nv_r: (layer * e_num + be_r[i], 0, 0))],
            out_specs=[rows_spec, rows_spec],
            scratch_shapes=[pltpu.VMEM((d, ff), BF16), pltpu.VMEM((d, ff), BF16), pltpu.VMEM((ff, d), BF16)],
        ),
        out_shape=[jax.ShapeDtypeStruct((n_rows, quarter), I32)] * 2,
        compiler_params=_cparams(("arbitrary",)),
        name="moe_experts",
    )(blk_expert, blk_valid, xa, xb, w1, w3, w2)

    dest12 = jnp.concatenate([dest1, dest2], axis=1)
    ga = _sc_gather_rows(ya, dest12)
    gb = _sc_gather_rows(yb, dest12)

    first = pl.BlockSpec((ts, quarter), lambda b, i: (b * nt_b + i, 0))
    second = pl.BlockSpec((ts, quarter), lambda b, i: (nt + b * nt_b + i, 0))
    return pl.pallas_call(
        functools.partial(_combine_kernel, alpha),
        grid=(bsz, nt_b),
        in_specs=[tile, vec, flat(LANES), _const_spec((1, d)), _const_spec((1, d)),
                  first, first, second, second],
        out_specs=tile,
        out_shape=jax.ShapeDtypeStruct((bsz, s, d), F32),
        compiler_params=_cparams(("arbitrary", "arbitrary")),
        name="moe_combine",
    )(x, g, mf, ln_g.reshape(1, d), ln_b.reshape(1, d), ga, gb, ga, gb)


def kernel(x, c, ada_w, ada_b, ln_g, ln_b, hgrn_w_in, hgrn_w_out, hgrn_lb, hgrn_norm_w, attn_w_in, attn_w_out,
           attn_lambda, attn_subln_w, router_g_w, router_g_b, router_e_w, router_e_b, moe_w1, moe_w3, moe_w2):
    depth = ada_w.shape[0]
    bsz, s, d = x.shape
    alpha = (2 * depth) ** 0.25
    lb_all = jnp.cumsum(jax.nn.softmax(hgrn_lb.astype(F32), axis=0), axis=0)
    lb_all = lb_all - lb_all[0:1]
    mod = _ada_mod(c, ada_w, ada_b).reshape(depth, bsz, 6, 1, d)
    w1_all = moe_w1.reshape((-1,) + moe_w1.shape[2:])
    w3_all = moe_w3.reshape((-1,) + moe_w3.shape[2:])
    w2_all = moe_w2.reshape((-1,) + moe_w2.shape[2:])
    for i in range(depth):
        sh1, sc1, g1, sh2, sc2, g2 = (mod[i, :, m] for m in range(6))
        j = i // 2
        if i % 2 == 0:
            x = _hgrn_layer(x, sh1, sc1, g1, hgrn_w_in[j], hgrn_w_out[j], lb_all[j], hgrn_norm_w[j],
                            ln_g[i, 0], ln_b[i, 0], alpha)
        else:
            lambda_init = 0.8 - 0.6 * math.exp(-0.3 * i)
            x = _attn_layer(x, sh1, sc1, g1, attn_w_in[j], attn_w_out[j], attn_lambda[j], attn_subln_w[j],
                            lambda_init, ln_g[i, 0], ln_b[i, 0], alpha)
        x = _moe_layer(x, sh2, sc2, g2, router_g_w[i], router_g_b[i], router_e_w[i], router_e_b[i],
                       w1_all, w3_all, w2_all, i, ln_g[i, 1], ln_b[i, 1], alpha)
    return x
```

```python
import functools
import math

import jax
import jax.numpy as jnp
from jax import lax
from jax.experimental import pallas as pl
from jax.experimental.pallas import tpu as pltpu
from jax.experimental.pallas import tpu_sc as plsc

F32 = jnp.float32
BF16 = jnp.bfloat16
I32 = jnp.int32
U32 = jnp.uint32
HIGHEST = lax.Precision.HIGHEST

LANES = 128
HEAD = 128
HGRN_CHUNK = 32
HGRN_GROUP = 2
HGRN_SAFE_SPAN = 80.0
ATTN_D = 64
FLASH_HEADS = 4
FLASH_ROW_CHUNK = 32
ROPE_THETA = 10000.0
MOE_GROUPS = 4
MOE_EPG = 8
MOE_EXPERTS = MOE_GROUPS * MOE_EPG
MOE_BLOCK = 512
SC_WINDOW = 128
NORM_EPS = 1e-5
VMEM_LIMIT = 56 * 1024 * 1024

NT_DIMS = (((1,), (1,)), ((), ()))
TN_DIMS = (((0,), (0,)), ((), ()))


def _cparams(sem):
    return pltpu.CompilerParams(dimension_semantics=sem, vmem_limit_bytes=VMEM_LIMIT)


def _const_spec(shape):
    nd = len(shape)
    return pl.BlockSpec(shape, lambda *_: (0,) * nd)


def _layer_norm(r, g, b):
    mu = jnp.mean(r, axis=-1, keepdims=True)
    d = r - mu
    var = jnp.mean(d * d, axis=-1, keepdims=True)
    return d * lax.rsqrt(var + NORM_EPS) * g + b


def _silu(x):
    return x * (1.0 / (1.0 + jnp.exp(-x)))


def _pack_bf16_pair(lo, hi):
    lo_b = lax.bitcast_convert_type(lo.astype(BF16).astype(F32), U32)
    hi_b = lax.bitcast_convert_type(hi.astype(BF16).astype(F32), U32)
    return (hi_b & jnp.uint32(0xFFFF0000)) | (lo_b >> 16)


def _unpack_bf16_pair(u):
    lo = lax.bitcast_convert_type(u << 16, F32)
    hi = lax.bitcast_convert_type(u & jnp.uint32(0xFFFF0000), F32)
    return lo, hi


def _ada_kernel(c_ref, w_ref, b_ref, o_ref):
    c = c_ref[...]
    o_ref[0] = jnp.dot(_silu(c), w_ref[0], precision=HIGHEST, preferred_element_type=F32) + b_ref[0]


def _ada_mod(c, ada_w, ada_b):
    depth, d, n6 = ada_w.shape
    bsz = c.shape[0]
    tn = d
    return pl.pallas_call(
        _ada_kernel,
        grid=(depth, n6 // tn),
        in_specs=[
            pl.BlockSpec((bsz, d), lambda i, j: (0, 0)),
            pl.BlockSpec((1, d, tn), lambda i, j: (i, 0, j)),
            pl.BlockSpec((1, 1, tn), lambda i, j: (i, 0, j)),
        ],
        out_specs=pl.BlockSpec((1, bsz, tn), lambda i, j: (i, 0, j)),
        out_shape=jax.ShapeDtypeStruct((depth, bsz, n6), F32),
        compiler_params=_cparams(("arbitrary", "arbitrary")),
        name="ada_mod",
    )(c, ada_w, ada_b.reshape(depth, 1, n6))


def _hgrn_kernel(alpha, x_ref, sh_ref, sc_ref, g_ref, win_ref, wout_ref, loglb_ref, oml_ref,
                 nw_ref, lng_ref, lnb_ref, o_ref,
                 proj_ref, st_ref, ocat_ref, b_ref, kk_ref, qi_ref, ks_ref, vb_ref, dec_ref):
    ts, d = x_ref.shape[1], x_ref.shape[2]
    nh = d // HEAD
    c = HGRN_CHUNK
    nc = ts // c
    gw = min(HGRN_GROUP * HEAD, d)
    ng = d // gw
    hpg = gw // HEAD

    @pl.when(pl.program_id(1) == 0)
    def _():
        st_ref[...] = jnp.zeros_like(st_ref)

    x = x_ref[0]
    hb = (x * (1.0 + sc_ref[0]) + sh_ref[0]).astype(BF16)
    pos = lax.broadcasted_iota(I32, (ts, 1), 0) % c
    row = lax.broadcasted_iota(I32, (ts, ts), 0)
    col = lax.broadcasted_iota(I32, (ts, ts), 1)
    keep = (row >= col) & (row // c == col // c)
    span = jnp.float32(0.0)

    def project(g):
        ps = slice(g * 4 * gw, (g + 1) * 4 * gw)
        proj_ref[:, ps] = jnp.dot(hb, win_ref[:, ps], preferred_element_type=F32)

    project(0)
    for g in range(ng):
        if g + 1 < ng:
            project(g + 1)
        cs = slice(g * gw, (g + 1) * gw)
        p0 = g * 4 * gw
        q2 = proj_ref[:, p0:p0 + gw]
        z = proj_ref[:, p0 + gw:p0 + 2 * gw]

        ls = jnp.minimum(z, 0.0) - jnp.log(1.0 + jnp.exp(-jnp.abs(z)))
        lsn = ls - z
        cc = loglb_ref[:, cs] + lsn
        log_f = jnp.maximum(ls, cc) + jnp.log(1.0 + jnp.exp(-jnp.abs(ls - cc)))
        kk = oml_ref[:, cs] * jnp.exp(lsn)

        b = log_f
        step = 1
        while step < c:
            b = b + jnp.where(pos >= step, pltpu.roll(b, step, 0), 0.0)
            step *= 2
        b_ref[:, cs] = b
        kk_ref[:, cs] = kk

        b3 = b.reshape(nc, c, gw)
        b_last = b3[:, c - 1:c, :]
        b_mid = b3[:, c // 2 - 1:c // 2, :]
        q3 = q2.reshape(nc, c, gw)
        k3 = kk.reshape(nc, c, gw)
        qi_ref[:, cs] = (q3 * jnp.exp(b3)).astype(BF16).reshape(ts, gw)
        ks_ref[:, cs] = (k3 * jnp.exp(b_last - b3)).astype(BF16).reshape(ts, gw)
        q_intra = (q3 * jnp.exp(b3 - b_mid)).astype(BF16).reshape(ts, gw)
        k_intra = (k3 * jnp.exp(b_mid - b3)).astype(BF16).reshape(ts, gw)
        dec_ref[:, cs] = jnp.exp(b_last).reshape(nc, gw)
        span = jnp.maximum(span, jnp.max(-b_last))
        v2 = proj_ref[:, p0 + 2 * gw:p0 + 3 * gw].astype(BF16)
        vb_ref[:, cs] = v2

        for hh in range(hpg):
            sl = slice(hh * HEAD, (hh + 1) * HEAD)
            sc = lax.dot_general(q_intra[:, sl], k_intra[:, sl], NT_DIMS, preferred_element_type=F32)
            p = jnp.where(keep, sc, 0.0).astype(BF16)
            ocat_ref[:, g * gw + hh * HEAD:g * gw + (hh + 1) * HEAD] = jnp.dot(
                p, v2[:, sl], preferred_element_type=F32)

    @pl.when(span > HGRN_SAFE_SPAN)
    def _():
        b = b_ref[...]
        kk = kk_ref[...]
        q2 = jnp.concatenate([proj_ref[:, g * 4 * gw:g * 4 * gw + gw] for g in range(ng)], axis=1)
        diag = q2 * kk
        tpos = lax.broadcasted_iota(I32, (ts, 1), 0)
        scores = [jnp.where(row == col, jnp.sum(diag[:, hd * HEAD:(hd + 1) * HEAD], axis=-1, keepdims=True), 0.0)
                  for hd in range(nh)]
        size = 2
        while size <= c:
            ref_row = (row // size) * size + (size // 2 - 1)
            pick = jnp.where(col == ref_row, 1.0, 0.0)
            b_at = jnp.dot(pick, b, precision=HIGHEST, preferred_element_type=F32)
            right = (tpos % size) >= (size // 2)
            qa = jnp.where(right, q2 * jnp.exp(b - b_at), 0.0).astype(BF16)
            ka = jnp.where(right, 0.0, kk * jnp.exp(b_at - b)).astype(BF16)
            same = (row // size) == (col // size)
            for hd in range(nh):
                sl = slice(hd * HEAD, (hd + 1) * HEAD)
                sc = lax.dot_general(qa[:, sl], ka[:, sl], NT_DIMS, preferred_element_type=F32)
                scores[hd] = scores[hd] + jnp.where(same, sc, 0.0)
            size *= 2
        for hd in range(nh):
            sl = slice(hd * HEAD, (hd + 1) * HEAD)
            ocat_ref[:, sl] = jnp.dot(scores[hd].astype(BF16), vb_ref[:, sl], preferred_element_type=F32)

    for ci in range(nc):
        rs = slice(ci * c, (ci + 1) * c)
        for hd in range(nh):
            sl = slice(hd * HEAD, (hd + 1) * HEAD)
            st = st_ref[hd]
            o_inter = lax.dot_general(qi_ref[rs, sl], st.astype(BF16), NT_DIMS,
                                      preferred_element_type=F32)
            ocat_ref[rs, sl] += o_inter
            upd = lax.dot_general(vb_ref[rs, sl], ks_ref[rs, sl], TN_DIMS,
                                  preferred_element_type=F32)
            st_ref[hd] = st * dec_ref[ci:ci + 1, sl] + upd

    for hd in range(nh):
        sl = slice(hd * HEAD, (hd + 1) * HEAD)
        g, hh = divmod(hd, hpg)
        oh = ocat_ref[:, sl]
        ms = jnp.mean(oh * oh, axis=-1, keepdims=True)
        gate = proj_ref[:, g * 4 * gw + 3 * gw + hh * HEAD:g * 4 * gw + 3 * gw + (hh + 1) * HEAD]
        ocat_ref[:, sl] = oh * lax.rsqrt(ms + NORM_EPS) * nw_ref[...] * _silu(gate)
    y = jnp.dot(ocat_ref[...].astype(BF16), wout_ref[...], preferred_element_type=F32)
    r = alpha * x + g_ref[0] * y
    o_ref[0] = _layer_norm(r, lng_ref[...], lnb_ref[...])


def _hgrn_layer(x, sh, sc, g, w_in, w_out, lb, norm_w, ln_g, ln_b, alpha):
    bsz, s, d = x.shape
    ts = min(256, s)
    nh = d // HEAD
    gw = min(HGRN_GROUP * HEAD, d)
    ng = d // gw
    w_grouped = w_in.reshape(d, 4, ng, gw).transpose(0, 2, 1, 3).reshape(d, 4 * d).astype(BF16)
    vec = pl.BlockSpec((1, 1, d), lambda b, i: (b, 0, 0))
    tile = pl.BlockSpec((1, ts, d), lambda b, i: (b, i, 0))
    return pl.pallas_call(
        functools.partial(_hgrn_kernel, alpha),
        grid=(bsz, s // ts),
        in_specs=[tile, vec, vec, vec,
                  _const_spec((d, 4 * d)), _const_spec((d, d)),
                  _const_spec((1, d)), _const_spec((1, d)), _const_spec((1, HEAD)),
                  _const_spec((1, d)), _const_spec((1, d))],
        out_specs=tile,
        out_shape=jax.ShapeDtypeStruct((bsz, s, d), F32),
        scratch_shapes=[pltpu.VMEM((ts, 4 * d), F32),
                        pltpu.VMEM((nh, HEAD, HEAD), F32),
                        pltpu.VMEM((ts, d), F32),
                        pltpu.VMEM((ts, d), F32), pltpu.VMEM((ts, d), F32),
                        pltpu.VMEM((ts, d), BF16), pltpu.VMEM((ts, d), BF16), pltpu.VMEM((ts, d), BF16),
                        pltpu.VMEM((ts // HGRN_CHUNK, d), F32)],
        compiler_params=_cparams(("arbitrary", "arbitrary")),
        name="hgrn_layer",
    )(x, sh, sc, g, w_grouped, w_out.astype(BF16),
      jnp.log(lb).reshape(1, d), (1.0 - lb).reshape(1, d), norm_w.reshape(1, HEAD),
      ln_g.reshape(1, d), ln_b.reshape(1, d))


def _attn_in_kernel(x_ref, sh_ref, sc_ref, w_ref, cos_ref, sin_ref, q_ref, k_ref, v_ref):
    d = x_ref.shape[2]
    x = x_ref[0]
    h = x * (1.0 + sc_ref[0]) + sh_ref[0]
    qkv = jnp.dot(h.astype(BF16), w_ref[...], preferred_element_type=F32)
    cos = cos_ref[...]
    sin = sin_ref[...]
    lane = lax.broadcasted_iota(I32, (1, LANES), 1)
    first_half = (lane % ATTN_D) < (ATTN_D // 2)
    scale = ATTN_D ** -0.5 * math.log2(math.e)
    for j in range(d // LANES):
        sl = slice(j * LANES, (j + 1) * LANES)
        for src, dst, mul in ((0, q_ref, scale), (d, k_ref, 1.0)):
            t = qkv[:, src + j * LANES:src + (j + 1) * LANES]
            partner = jnp.where(first_half, pltpu.roll(t, LANES - ATTN_D // 2, 1),
                                pltpu.roll(t, ATTN_D // 2, 1))
            dst[0, :, sl] = ((t * cos + partner * sin) * mul).astype(BF16)
    v_ref[0] = qkv[:, 2 * d:].astype(BF16)


def _flash_kernel(out_scale, lam_ref, q_ref, k_ref, v_ref, w_ref, o_ref, *scratch):
    tq = q_ref.shape[1]
    rows = 2 * tq
    qi = pl.program_id(2)
    nrep = tq // LANES
    rc = FLASH_ROW_CHUNK
    per_head = len(scratch) // FLASH_HEADS
    heads = [(hd,) + tuple(scratch[hd * per_head:(hd + 1) * per_head]) for hd in range(FLASH_HEADS)]
    lane = lax.broadcasted_iota(I32, (1, LANES), 1)

    for hd, m_ref, l_ref, acc_ref, _, _, _, qq_ref in heads:
        q = q_ref[0, :, hd * HEAD:(hd + 1) * HEAD]
        zero = jnp.zeros_like(q)
        qq_ref[0:tq, :] = jnp.where(lane < ATTN_D, q, zero)
        qq_ref[tq:rows, :] = jnp.where(lane >= ATTN_D, q, zero)
        m_ref[...] = jnp.full_like(m_ref, -jnp.inf)
        l_ref[...] = jnp.zeros_like(l_ref)
        acc_ref[...] = jnp.zeros_like(acc_ref)

    def scores(head, j):
        hd, _, _, _, _, s_ref, _, qq_ref = head
        kb = k_ref[0, pl.ds(pl.multiple_of(j * tq, tq), tq), hd * HEAD:(hd + 1) * HEAD]
        s_ref[...] = lax.dot_general(qq_ref[...], kb, NT_DIMS, preferred_element_type=F32)

    def softmax(head, masked):
        _, m_ref, l_ref, _, a_ref, s_ref, p_ref, _ = head
        for r0 in range(0, rows, rc):
            rs = slice(r0, r0 + rc)
            s = s_ref[rs, :]
            if masked:
                row = lax.broadcasted_iota(I32, (rc, tq), 0) + (r0 % tq)
                col = lax.broadcasted_iota(I32, (rc, tq), 1)
                s = jnp.where(row >= col, s, -jnp.inf)
            m_old = m_ref[rs, :]
            m_new = jnp.maximum(m_old, jnp.max(s, axis=-1, keepdims=True))
            a = jnp.exp2(m_old - m_new)
            p = jnp.exp2(s - jnp.concatenate([m_new] * nrep, axis=1))
            psum = p[:, 0:LANES]
            for r in range(1, nrep):
                psum = psum + p[:, r * LANES:(r + 1) * LANES]
            l_ref[rs, :] = a * l_ref[rs, :] + psum
            m_ref[rs, :] = m_new
            a_ref[rs, :] = a
            p_ref[rs, :] = p.astype(BF16)

    def values(head, j):
        hd, _, _, acc_ref, a_ref, _, p_ref, _ = head
        vb = v_ref[0, pl.ds(pl.multiple_of(j * tq, tq), tq), hd * HEAD:(hd + 1) * HEAD]
        acc_ref[...] = a_ref[...] * acc_ref[...] + jnp.dot(p_ref[...], vb, preferred_element_type=F32)

    nh_step = len(heads)
    n_blk = qi + 1

    def emit(base, k_lo, k_hi, n_items, masked_from):
        for k in range(k_lo, k_hi + 1):
            if k + 1 < n_items:
                scores(heads[(k + 1) % nh_step], base + (k + 1) // nh_step)
            if 0 <= k < n_items:
                softmax(heads[k % nh_step], k >= masked_from)
            if 0 <= k - 1 < n_items:
                values(heads[(k - 1) % nh_step], base + (k - 1) // nh_step)

    @pl.when(n_blk == 1)
    def _():
        emit(0, -1, nh_step, nh_step, 0)

    @pl.when(n_blk >= 2)
    def _():
        unbounded = 1 << 30
        emit(0, -1, 0, unbounded, unbounded)

        def body(t, carry):
            emit(t, 1, nh_step, unbounded, unbounded)
            return carry

        lax.fori_loop(0, n_blk - 2, body, 0)
        emit(n_blk - 2, 1, 2 * nh_step, 2 * nh_step, nh_step)

    for hd, _, l_ref, acc_ref, _, _, _, _ in heads:
        o_all = acc_ref[...] / jnp.sum(l_ref[...], axis=-1, keepdims=True)
        o = o_all[:tq] - lam_ref[0] * o_all[tq:]
        ms = jnp.mean(o * o, axis=-1, keepdims=True)
        o_ref[0, :, hd * HEAD:(hd + 1) * HEAD] = (
            o * lax.rsqrt(ms + NORM_EPS) * w_ref[...] * out_scale).astype(BF16)


def _resid_ln_kernel(alpha, x_ref, g_ref, o_in_ref, w_ref, lng_ref, lnb_ref, o_ref):
    y = jnp.dot(o_in_ref[0], w_ref[...], preferred_element_type=F32)
    r = alpha * x_ref[0] + g_ref[0] * y
    o_ref[0] = _layer_norm(r, lng_ref[...], lnb_ref[...])


def _attn_layer(x, sh, sc, g, w_in, w_out, lam_params, subln_w, lambda_init, ln_g, ln_b, alpha):
    bsz, s, d = x.shape
    nh = d // HEAD
    ts = min(512, s)
    tq = min(512, s)
    half = ATTN_D // 2
    inv_freq = ROPE_THETA ** (-jnp.arange(half, dtype=F32) / half)
    ang = jnp.arange(s, dtype=F32)[:, None] * inv_freq[None, :]
    cos_t = jnp.tile(jnp.cos(ang), (1, LANES // half))
    sin_h = jnp.sin(ang)
    sin_t = jnp.tile(jnp.concatenate([-sin_h, sin_h], axis=1), (1, LANES // ATTN_D))

    vec = pl.BlockSpec((1, 1, d), lambda b, i: (b, 0, 0))
    tile = pl.BlockSpec((1, ts, d), lambda b, i: (b, i, 0))
    rope = pl.BlockSpec((ts, LANES), lambda b, i: (i, 0))
    q, k, v = pl.pallas_call(
        _attn_in_kernel,
        grid=(bsz, s // ts),
        in_specs=[tile, vec, vec, _const_spec((d, 3 * d)), rope, rope],
        out_specs=[tile, tile, tile],
        out_shape=[jax.ShapeDtypeStruct((bsz, s, d), BF16)] * 3,
        compiler_params=_cparams(("arbitrary", "arbitrary")),
        name="attn_in",
    )(x, sh, sc, w_in.astype(BF16), cos_t, sin_t)

    lp = lam_params.astype(F32)
    lam = (jnp.exp(jnp.sum(lp[0] * lp[1])) - jnp.exp(jnp.sum(lp[2] * lp[3])) + lambda_init).reshape(1)
    hw = FLASH_HEADS * HEAD
    qspec = pl.BlockSpec((1, tq, hw), lambda b, h, i: (b, i, h))
    kvspec = pl.BlockSpec((1, s, hw), lambda b, h, i: (b, 0, h))
    head_scratch = [pltpu.VMEM((2 * tq, LANES), F32), pltpu.VMEM((2 * tq, LANES), F32),
                    pltpu.VMEM((2 * tq, HEAD), F32), pltpu.VMEM((2 * tq, LANES), F32),
                    pltpu.VMEM((2 * tq, tq), F32), pltpu.VMEM((2 * tq, tq), BF16),
                    pltpu.VMEM((2 * tq, HEAD), BF16)]
    o = pl.pallas_call(
        functools.partial(_flash_kernel, 1.0 - lambda_init),
        grid=(bsz, nh // FLASH_HEADS, s // tq),
        in_specs=[pl.BlockSpec(memory_space=pltpu.SMEM), qspec, kvspec, kvspec,
                  pl.BlockSpec((1, HEAD), lambda b, h, i: (0, 0))],
        out_specs=qspec,
        out_shape=jax.ShapeDtypeStruct((bsz, s, d), BF16),
        scratch_shapes=head_scratch * FLASH_HEADS,
        compiler_params=_cparams(("arbitrary", "arbitrary", "arbitrary")),
        name="diff_flash",
    )(lam, q, k, v, subln_w.reshape(1, HEAD))

    return pl.pallas_call(
        functools.partial(_resid_ln_kernel, alpha),
        grid=(bsz, s // ts),
        in_specs=[tile, vec, tile, _const_spec((d, d)), _const_spec((1, d)), _const_spec((1, d))],
        out_specs=tile,
        out_shape=jax.ShapeDtypeStruct((bsz, s, d), F32),
        compiler_params=_cparams(("arbitrary", "arbitrary")),
        name="attn_out",
    )(x, g, o, w_out.astype(BF16), ln_g.reshape(1, d), ln_b.reshape(1, d))


def _pack_row_parts(v):
    q = v.shape[1] // 4
    return tuple(lax.bitcast_convert_type(_pack_bf16_pair(v[:, p * q:(p + 1) * q], v[:, (2 + p) * q:(3 + p) * q]), I32)
                 for p in range(2))


def _unpack_row_parts(part_a, part_b):
    lo_a, hi_a = _unpack_bf16_pair(lax.bitcast_convert_type(part_a, U32))
    lo_b, hi_b = _unpack_bf16_pair(lax.bitcast_convert_type(part_b, U32))
    return lo_a, lo_b, hi_a, hi_b


def _router_kernel(x_ref, sh_ref, sc_ref, w_ref, bias_ref, tri_ref, hpa_ref, hpb_ref, mt_ref, mf_ref, cnt_ref,
                   cnt_scr):
    ts, d = x_ref.shape[1], x_ref.shape[2]
    first = (pl.program_id(0) == 0) & (pl.program_id(1) == 0)

    @pl.when(first)
    def _():
        cnt_scr[...] = jnp.zeros_like(cnt_scr)

    x = x_ref[0]
    h = x * (1.0 + sc_ref[0]) + sh_ref[0]
    hpa_ref[...], hpb_ref[...] = _pack_row_parts(h)

    h_hi = h.astype(BF16)
    h_lo = (h - h_hi.astype(F32)).astype(BF16)
    hh = jnp.dot(h_hi, w_ref[...], preferred_element_type=F32)
    lh = jnp.dot(h_lo, w_ref[:, 0:LANES], preferred_element_type=F32)
    logits = hh[:, 0:LANES] + hh[:, LANES:2 * LANES] + lh + bias_ref[...]
    lane = lax.broadcasted_iota(I32, (ts, LANES), 1)
    neg = -jnp.inf
    big = jnp.int32(LANES)
    is_g = lane < MOE_GROUPS
    gl = jnp.where(is_g, logits, neg)
    gmax = jnp.max(gl, axis=-1, keepdims=True)
    g_idx = jnp.min(jnp.where(gl == gmax, lane, big), axis=-1, keepdims=True)
    g_w = 1.0 / jnp.sum(jnp.exp(gl - gmax), axis=-1, keepdims=True)

    e_lane = lane - MOE_GROUPS
    in_grp = (e_lane >= g_idx * MOE_EPG) & (e_lane < (g_idx + 1) * MOE_EPG)
    el = jnp.where(in_grp, logits, neg)
    l1 = jnp.max(el, axis=-1, keepdims=True)
    i1 = jnp.min(jnp.where(el == l1, lane, big), axis=-1, keepdims=True)
    el2 = jnp.where(lane == i1, neg, el)
    l2 = jnp.max(el2, axis=-1, keepdims=True)
    i2 = jnp.min(jnp.where(el2 == l2, lane, big), axis=-1, keepdims=True)
    t = jnp.exp(l2 - l1)
    w1 = g_w / (1.0 + t)
    w2 = g_w * t / (1.0 + t)

    oh1 = (lane == i1)
    oh2 = (lane == i2)
    both = (oh1 | oh2).astype(BF16)
    before = jnp.dot(tri_ref[...], both, preferred_element_type=F32) + cnt_scr[...]
    r1 = jnp.sum(jnp.where(oh1, before, 0.0), axis=-1, keepdims=True)
    r2 = jnp.sum(jnp.where(oh2, before, 0.0), axis=-1, keepdims=True)
    cnt_scr[...] = cnt_scr[...] + jnp.sum(both.astype(F32), axis=0, keepdims=True)
    cnt_ref[...] = cnt_scr[...]

    e1 = i1 - MOE_GROUPS
    e2 = i2 - MOE_GROUPS
    meta = jnp.where(lane == 0, e1, jnp.where(lane == 1, e2, jnp.where(
        lane == 2, r1.astype(I32), jnp.where(lane == 3, r2.astype(I32), 0))))
    mt_ref[...] = jnp.transpose(meta)[0:8, :]
    mf_ref[...] = jnp.where(lane == 0, w1, jnp.where(lane == 1, w2, 0.0))


def _dest_kernel(ps_ref, mt_ref, d1_ref, d2_ref):
    e1 = mt_ref[0:1, :]
    e2 = mt_ref[1:2, :]
    p1 = jnp.zeros_like(e1)
    p2 = jnp.zeros_like(e2)
    for e in range(MOE_EXPERTS):
        p1 = jnp.where(e1 == e, ps_ref[e], p1)
        p2 = jnp.where(e2 == e, ps_ref[e], p2)
    d1_ref[...] = p1 + mt_ref[2:3, :]
    d2_ref[...] = p2 + mt_ref[3:4, :]


def _sc_mesh():
    return plsc.VectorSubcoreMesh(core_axis_name="c", subcore_axis_name="s")


def _sc_scatter_rows(src, idx_a, idx_b, n_rows):
    m, w = src.shape

    @functools.partial(pl.kernel, out_type=jax.ShapeDtypeStruct((n_rows, w), src.dtype),
                       mesh=_sc_mesh(), scratch_types=[], name="moe_sc_scatter")
    def scatter(x_hbm, ia_hbm, ib_hbm, o_hbm):
        def body(x_vmem, ia_vmem, ib_vmem):
            pltpu.sync_copy(x_vmem, o_hbm.at[ia_vmem.at[0]])
            pltpu.sync_copy(x_vmem, o_hbm.at[ib_vmem.at[0]])

        pltpu.emit_pipeline(
            body,
            grid=(m // SC_WINDOW,),
            in_specs=[pl.BlockSpec((SC_WINDOW, w), lambda i: (i, 0)),
                      pl.BlockSpec((1, SC_WINDOW), lambda i: (0, i)),
                      pl.BlockSpec((1, SC_WINDOW), lambda i: (0, i))],
            out_specs=[],
            core_axis_name=("c", "s"),
            dimension_semantics=(pltpu.PARALLEL,),
        )(x_hbm, ia_hbm, ib_hbm)

    return scatter(src, idx_a, idx_b)


def _sc_gather_rows(table, idx):
    m = idx.shape[1]
    w = table.shape[1]

    @functools.partial(pl.kernel, out_type=jax.ShapeDtypeStruct((m, w), table.dtype),
                       mesh=_sc_mesh(), scratch_types=[], name="moe_sc_gather")
    def gather(t_hbm, i_hbm, o_hbm):
        def body(i_vmem, o_vmem):
            pltpu.sync_copy(t_hbm.at[i_vmem.at[0]], o_vmem)

        pltpu.emit_pipeline(
            body,
            grid=(m // SC_WINDOW,),
            in_specs=[pl.BlockSpec((1, SC_WINDOW), lambda i: (0, i))],
            out_specs=[pl.BlockSpec((SC_WINDOW, w), lambda i: (i, 0))],
            core_axis_name=("c", "s"),
            dimension_semantics=(pltpu.PARALLEL,),
        )(i_hbm, o_hbm)

    return gather(table, idx)


def _expert_kernel(be_ref, nv_ref, xa_ref, xb_ref, w1_ref, w3_ref, w2_ref, ya_ref, yb_ref,
                   w1_scr, w3_scr, w2_scr):
    i = pl.program_id(0)
    blk = xa_ref.shape[0]
    n_valid = nv_ref[i]
    new_expert = (i == 0) | (be_ref[i] != be_ref[jnp.maximum(i - 1, 0)])

    @pl.when((n_valid > 0) & new_expert)
    def _():
        w1_scr[...] = w1_ref[0].astype(BF16)
        w3_scr[...] = w3_ref[0].astype(BF16)
        w2_scr[...] = w2_ref[0].astype(BF16)

    @pl.when(n_valid > 0)
    def _():
        valid = lax.broadcasted_iota(I32, (blk, 1), 0) < n_valid
        xq = _unpack_row_parts(xa_ref[...], xb_ref[...])
        xin = jnp.concatenate([jnp.where(valid, q, 0.0).astype(BF16) for q in xq], axis=1)
        a = jnp.dot(xin, w1_scr[...], preferred_element_type=F32)
        b = jnp.dot(xin, w3_scr[...], preferred_element_type=F32)
        hid = (_silu(a) * b).astype(BF16)
        y = jnp.dot(hid, w2_scr[...], preferred_element_type=F32)
        ya_ref[...], yb_ref[...] = _pack_row_parts(y)

    @pl.when(n_valid <= 0)
    def _():
        ya_ref[...] = jnp.zeros_like(ya_ref)
        yb_ref[...] = jnp.zeros_like(yb_ref)


def _combine_kernel(alpha, x_ref, g_ref, mf_ref, lng_ref, lnb_ref, y1a_ref, y1b_ref, y2a_ref, y2b_ref, o_ref):
    d = x_ref.shape[2]
    q = d // 4
    w1 = mf_ref[:, 0:1]
    w2 = mf_ref[:, 1:2]
    y1 = _unpack_row_parts(y1a_ref[...], y1b_ref[...])
    y2 = _unpack_row_parts(y2a_ref[...], y2b_ref[...])
    r = [alpha * x_ref[0, :, p * q:(p + 1) * q] + g_ref[0, :, p * q:(p + 1) * q] * (w1 * y1[p] + w2 * y2[p])
         for p in range(4)]
    mu = sum(jnp.sum(rp, axis=-1, keepdims=True) for rp in r) / d
    dev = [rp - mu for rp in r]
    var = sum(jnp.sum(dp * dp, axis=-1, keepdims=True) for dp in dev) / d
    inv = lax.rsqrt(var + NORM_EPS)
    for p in range(4):
        sl = slice(p * q, (p + 1) * q)
        o_ref[0, :, sl] = dev[p] * inv * lng_ref[:, sl] + lnb_ref[:, sl]


def _moe_layer(x, sh, sc, g, wg, bg, we, be, w1, w3, w2, layer, ln_g, ln_b, alpha):
    bsz, s, d = x.shape
    n = bsz * s
    quarter = d // 4
    e_num, blk = MOE_EXPERTS, MOE_BLOCK
    ff = w1.shape[-1]
    ts = min(512, s)
    nt_b = s // ts
    nt = n // ts
    n_rows = n * 2 + e_num * blk
    nb = n_rows // blk

    wcat = jnp.zeros((d, LANES), F32).at[:, :MOE_GROUPS].set(wg).at[:, MOE_GROUPS:MOE_GROUPS + e_num].set(we)
    bcat = jnp.zeros((1, LANES), F32).at[0, :MOE_GROUPS].set(bg).at[0, MOE_GROUPS:MOE_GROUPS + e_num].set(be)
    wcat_hi = wcat.astype(BF16)
    wcat_hl = jnp.concatenate([wcat_hi, (wcat - wcat_hi.astype(F32)).astype(BF16)], axis=1)
    tri = jnp.tri(ts, k=-1, dtype=BF16)

    vec = pl.BlockSpec((1, 1, d), lambda b, i: (b, 0, 0))
    tile = pl.BlockSpec((1, ts, d), lambda b, i: (b, i, 0))
    flat = lambda w: pl.BlockSpec((ts, w), lambda b, i: (b * nt_b + i, 0))
    hpa, hpb, mt, mf, cnt = pl.pallas_call(
        _router_kernel,
        grid=(bsz, nt_b),
        in_specs=[tile, vec, vec, _const_spec((d, 2 * LANES)), _const_spec((1, LANES)), _const_spec((ts, ts))],
        out_specs=[flat(quarter), flat(quarter), pl.BlockSpec((8, ts), lambda b, i: (0, b * nt_b + i)),
                   flat(LANES), _const_spec((1, LANES))],
        out_shape=[jax.ShapeDtypeStruct((n, quarter), I32), jax.ShapeDtypeStruct((n, quarter), I32),
                   jax.ShapeDtypeStruct((8, n), I32), jax.ShapeDtypeStruct((n, LANES), F32),
                   jax.ShapeDtypeStruct((1, LANES), F32)],
        scratch_shapes=[pltpu.VMEM((1, LANES), F32)],
        compiler_params=_cparams(("arbitrary", "arbitrary")),
        name="moe_router",
    )(x, sh, sc, wcat_hl, bcat, tri)

    counts = cnt[0, MOE_GROUPS:MOE_GROUPS + e_num].astype(I32)
    padded = ((counts + blk - 1) // blk) * blk
    pad_end = jnp.cumsum(padded)
    pad_start = pad_end - padded
    blk_start = jnp.arange(nb, dtype=I32) * blk
    blk_expert = jnp.minimum(jnp.sum((pad_end[None, :] <= blk_start[:, None]).astype(I32), axis=1), e_num - 1)
    blk_valid = jnp.clip(pad_start[blk_expert] + counts[blk_expert] - blk_start, 0, blk).astype(I32)

    td = min(8192, n)
    dest1, dest2 = pl.pallas_call(
        _dest_kernel,
        grid_spec=pltpu.PrefetchScalarGridSpec(
            num_scalar_prefetch=1,
            grid=(n // td,),
            in_specs=[pl.BlockSpec((8, td), lambda i, ps_r: (0, i))],
            out_specs=[pl.BlockSpec((1, td), lambda i, ps_r: (0, i))] * 2,
        ),
        out_shape=[jax.ShapeDtypeStruct((1, n), I32)] * 2,
        compiler_params=_cparams(("arbitrary",)),
        name="moe_dest",
    )(pad_start.astype(I32), mt)

    xa = _sc_scatter_rows(hpa, dest1, dest2, n_rows)
    xb = _sc_scatter_rows(hpb, dest1, dest2, n_rows)

    rows_spec = pl.BlockSpec((blk, quarter), lambda i, be_r, nv_r: (i, 0))
    ya, yb = pl.pallas_call(
        _expert_kernel,
        grid_spec=pltpu.PrefetchScalarGridSpec(
            num_scalar_prefetch=2,
            grid=(nb,),
            in_specs=[rows_spec, rows_spec,
                      pl.BlockSpec((1, d, ff), lambda i, be_r, nv_r: (layer * e_num + be_r[i], 0, 0)),
                      pl.BlockSpec((1, d, ff), lambda i, be_r, nv_r: (layer * e_num + be_r[i], 0, 0)),
                      pl.BlockSpec((1, ff, d), lambda i, be_r, nv_r: (layer * e_num + be_r[i], 0, 0))],
            out_specs=[rows_spec, rows_spec],
            scratch_shapes=[pltpu.VMEM((d, ff), BF16), pltpu.VMEM((d, ff), BF16), pltpu.VMEM((ff, d), BF16)],
        ),
        out_shape=[jax.ShapeDtypeStruct((n_rows, quarter), I32)] * 2,
        compiler_params=_cparams(("arbitrary",)),
        name="moe_experts",
    )(blk_expert, blk_valid, xa, xb, w1, w3, w2)

    dest12 = jnp.concatenate([dest1, dest2], axis=1)
    ga = _sc_gather_rows(ya, dest12)
    gb = _sc_gather_rows(yb, dest12)

    first = pl.BlockSpec((ts, quarter), lambda b, i: (b * nt_b + i, 0))
    second = pl.BlockSpec((ts, quarter), lambda b, i: (nt + b * nt_b + i, 0))
    return pl.pallas_call(
        functools.partial(_combine_kernel, alpha),
        grid=(bsz, nt_b),
        in_specs=[tile, vec, flat(LANES), _const_spec((1, d)), _const_spec((1, d)),
                  first, first, second, second],
        out_specs=tile,
        out_shape=jax.ShapeDtypeStruct((bsz, s, d), F32),
        compiler_params=_cparams(("arbitrary", "arbitrary")),
        name="moe_combine",
    )(x, g, mf, ln_g.reshape(1, d), ln_b.reshape(1, d), ga, gb, ga, gb)


def kernel(x, c, ada_w, ada_b, ln_g, ln_b, hgrn_w_in, hgrn_w_out, hgrn_lb, hgrn_norm_w, attn_w_in, attn_w_out,
           attn_lambda, attn_subln_w, router_g_w, router_g_b, router_e_w, router_e_b, moe_w1, moe_w3, moe_w2):
    depth = ada_w.shape[0]
    bsz, s, d = x.shape
    alpha = (2 * depth) ** 0.25
    lb_all = jnp.cumsum(jax.nn.softmax(hgrn_lb.astype(F32), axis=0), axis=0)
    lb_all = lb_all - lb_all[0:1]
    mod = _ada_mod(c, ada_w, ada_b).reshape(depth, bsz, 6, 1, d)
    w1_all = moe_w1.reshape((-1,) + moe_w1.shape[2:])
    w3_all = moe_w3.reshape((-1,) + moe_w3.shape[2:])
    w2_all = moe_w2.reshape((-1,) + moe_w2.shape[2:])
    for i in range(depth):
        sh1, sc1, g1, sh2, sc2, g2 = (mod[i, :, m] for m in range(6))
        j = i // 2
        if i % 2 == 0:
            x = _hgrn_layer(x, sh1, sc1, g1, hgrn_w_in[j], hgrn_w_out[j], lb_all[j], hgrn_norm_w[j],
                            ln_g[i, 0], ln_b[i, 0], alpha)
        else:
            lambda_init = 0.8 - 0.6 * math.exp(-0.3 * i)
            x = _attn_layer(x, sh1, sc1, g1, attn_w_in[j], attn_w_out[j], attn_lambda[j], attn_subln_w[j],
                            lambda_init, ln_g[i, 0], ln_b[i, 0], alpha)
        x = _moe_layer(x, sh2, sc2, g2, router_g_w[i], router_g_b[i], router_e_w[i], router_e_b[i],
                       w1_all, w3_all, w2_all, i, ln_g[i, 1], ln_b[i, 1], alpha)
    return x
```

```python
import functools
import math

import jax
import jax.numpy as jnp
from jax import lax
from jax.experimental import pallas as pl
from jax.experimental.pallas import tpu as pltpu
from jax.experimental.pallas import tpu_sc as plsc

F32 = jnp.float32
BF16 = jnp.bfloat16
I32 = jnp.int32
U32 = jnp.uint32
HIGHEST = lax.Precision.HIGHEST

LANES = 128
HEAD = 128
HGRN_CHUNK = 32
HGRN_GROUP = 2
HGRN_SAFE_SPAN = 80.0
ATTN_D = 64
FLASH_HEADS = 8
FLASH_ROW_CHUNK = 32
ROPE_THETA = 10000.0
MOE_GROUPS = 4
MOE_EPG = 8
MOE_EXPERTS = MOE_GROUPS * MOE_EPG
MOE_BLOCK = 512
SC_WINDOW = 128
NORM_EPS = 1e-5
VMEM_LIMIT = 56 * 1024 * 1024

NT_DIMS = (((1,), (1,)), ((), ()))
TN_DIMS = (((0,), (0,)), ((), ()))


def _cparams(sem):
    return pltpu.CompilerParams(dimension_semantics=sem, vmem_limit_bytes=VMEM_LIMIT)


def _const_spec(shape):
    nd = len(shape)
    return pl.BlockSpec(shape, lambda *_: (0,) * nd)


def _layer_norm(r, g, b):
    mu = jnp.mean(r, axis=-1, keepdims=True)
    d = r - mu
    var = jnp.mean(d * d, axis=-1, keepdims=True)
    return d * lax.rsqrt(var + NORM_EPS) * g + b


def _silu(x):
    return x * (1.0 / (1.0 + jnp.exp(-x)))


def _pack_bf16_pair(lo, hi):
    lo_b = lax.bitcast_convert_type(lo.astype(BF16).astype(F32), U32)
    hi_b = lax.bitcast_convert_type(hi.astype(BF16).astype(F32), U32)
    return (hi_b & jnp.uint32(0xFFFF0000)) | (lo_b >> 16)


def _unpack_bf16_pair(u):
    lo = lax.bitcast_convert_type(u << 16, F32)
    hi = lax.bitcast_convert_type(u & jnp.uint32(0xFFFF0000), F32)
    return lo, hi


def _ada_kernel(c_ref, w_ref, b_ref, o_ref):
    c = c_ref[...]
    o_ref[0] = jnp.dot(_silu(c), w_ref[0], precision=HIGHEST, preferred_element_type=F32) + b_ref[0]


def _ada_mod(c, ada_w, ada_b):
    depth, d, n6 = ada_w.shape
    bsz = c.shape[0]
    tn = d
    return pl.pallas_call(
        _ada_kernel,
        grid=(depth, n6 // tn),
        in_specs=[
            pl.BlockSpec((bsz, d), lambda i, j: (0, 0)),
            pl.BlockSpec((1, d, tn), lambda i, j: (i, 0, j)),
            pl.BlockSpec((1, 1, tn), lambda i, j: (i, 0, j)),
        ],
        out_specs=pl.BlockSpec((1, bsz, tn), lambda i, j: (i, 0, j)),
        out_shape=jax.ShapeDtypeStruct((depth, bsz, n6), F32),
        compiler_params=_cparams(("arbitrary", "arbitrary")),
        name="ada_mod",
    )(c, ada_w, ada_b.reshape(depth, 1, n6))


def _hgrn_kernel(alpha, x_ref, sh_ref, sc_ref, g_ref, win_ref, wout_ref, loglb_ref, oml_ref,
                 nw_ref, lng_ref, lnb_ref, o_ref,
                 proj_ref, st_ref, ocat_ref, b_ref, kk_ref, qi_ref, ks_ref, vb_ref, dec_ref):
    ts, d = x_ref.shape[1], x_ref.shape[2]
    nh = d // HEAD
    c = HGRN_CHUNK
    nc = ts // c
    gw = min(HGRN_GROUP * HEAD, d)
    ng = d // gw
    hpg = gw // HEAD

    @pl.when(pl.program_id(1) == 0)
    def _():
        st_ref[...] = jnp.zeros_like(st_ref)

    x = x_ref[0]
    hb = (x * (1.0 + sc_ref[0]) + sh_ref[0]).astype(BF16)
    pos = lax.broadcasted_iota(I32, (ts, 1), 0) % c
    row = lax.broadcasted_iota(I32, (ts, ts), 0)
    col = lax.broadcasted_iota(I32, (ts, ts), 1)
    keep = (row >= col) & (row // c == col // c)
    span = jnp.float32(0.0)

    def project(g):
        ps = slice(g * 4 * gw, (g + 1) * 4 * gw)
        proj_ref[:, ps] = jnp.dot(hb, win_ref[:, ps], preferred_element_type=F32)

    project(0)
    for g in range(ng):
        if g + 1 < ng:
            project(g + 1)
        cs = slice(g * gw, (g + 1) * gw)
        p0 = g * 4 * gw
        q2 = proj_ref[:, p0:p0 + gw]
        z = proj_ref[:, p0 + gw:p0 + 2 * gw]

        ls = jnp.minimum(z, 0.0) - jnp.log(1.0 + jnp.exp(-jnp.abs(z)))
        lsn = ls - z
        cc = loglb_ref[:, cs] + lsn
        log_f = jnp.maximum(ls, cc) + jnp.log(1.0 + jnp.exp(-jnp.abs(ls - cc)))
        kk = oml_ref[:, cs] * jnp.exp(lsn)

        b = log_f
        step = 1
        while step < c:
            b = b + jnp.where(pos >= step, pltpu.roll(b, step, 0), 0.0)
            step *= 2
        b_ref[:, cs] = b
        kk_ref[:, cs] = kk

        b3 = b.reshape(nc, c, gw)
        b_last = b3[:, c - 1:c, :]
        b_mid = b3[:, c // 2 - 1:c // 2, :]
        q3 = q2.reshape(nc, c, gw)
        k3 = kk.reshape(nc, c, gw)
        qi_ref[:, cs] = (q3 * jnp.exp(b3)).astype(BF16).reshape(ts, gw)
        ks_ref[:, cs] = (k3 * jnp.exp(b_last - b3)).astype(BF16).reshape(ts, gw)
        q_intra = (q3 * jnp.exp(b3 - b_mid)).astype(BF16).reshape(ts, gw)
        k_intra = (k3 * jnp.exp(b_mid - b3)).astype(BF16).reshape(ts, gw)
        dec_ref[:, cs] = jnp.exp(b_last).reshape(nc, gw)
        span = jnp.maximum(span, jnp.max(-b_last))
        v2 = proj_ref[:, p0 + 2 * gw:p0 + 3 * gw].astype(BF16)
        vb_ref[:, cs] = v2

        for hh in range(hpg):
            sl = slice(hh * HEAD, (hh + 1) * HEAD)
            sc = lax.dot_general(q_intra[:, sl], k_intra[:, sl], NT_DIMS, preferred_element_type=F32)
            p = jnp.where(keep, sc, 0.0).astype(BF16)
            ocat_ref[:, g * gw + hh * HEAD:g * gw + (hh + 1) * HEAD] = jnp.dot(
                p, v2[:, sl], preferred_element_type=F32)

    @pl.when(span > HGRN_SAFE_SPAN)
    def _():
        b = b_ref[...]
        kk = kk_ref[...]
        q2 = jnp.concatenate([proj_ref[:, g * 4 * gw:g * 4 * gw + gw] for g in range(ng)], axis=1)
        diag = q2 * kk
        tpos = lax.broadcasted_iota(I32, (ts, 1), 0)
        scores = [jnp.where(row == col, jnp.sum(diag[:, hd * HEAD:(hd + 1) * HEAD], axis=-1, keepdims=True), 0.0)
                  for hd in range(nh)]
        size = 2
        while size <= c:
            ref_row = (row // size) * size + (size // 2 - 1)
            pick = jnp.where(col == ref_row, 1.0, 0.0)
            b_at = jnp.dot(pick, b, precision=HIGHEST, preferred_element_type=F32)
            right = (tpos % size) >= (size // 2)
            qa = jnp.where(right, q2 * jnp.exp(b - b_at), 0.0).astype(BF16)
            ka = jnp.where(right, 0.0, kk * jnp.exp(b_at - b)).astype(BF16)
            same = (row // size) == (col // size)
            for hd in range(nh):
                sl = slice(hd * HEAD, (hd + 1) * HEAD)
                sc = lax.dot_general(qa[:, sl], ka[:, sl], NT_DIMS, preferred_element_type=F32)
                scores[hd] = scores[hd] + jnp.where(same, sc, 0.0)
            size *= 2
        for hd in range(nh):
            sl = slice(hd * HEAD, (hd + 1) * HEAD)
            ocat_ref[:, sl] = jnp.dot(scores[hd].astype(BF16), vb_ref[:, sl], preferred_element_type=F32)

    for ci in range(nc):
        rs = slice(ci * c, (ci + 1) * c)
        for hd in range(nh):
            sl = slice(hd * HEAD, (hd + 1) * HEAD)
            st = st_ref[hd]
            o_inter = lax.dot_general(qi_ref[rs, sl], st.astype(BF16), NT_DIMS,
                                      preferred_element_type=F32)
            ocat_ref[rs, sl] += o_inter
            upd = lax.dot_general(vb_ref[rs, sl], ks_ref[rs, sl], TN_DIMS,
                                  preferred_element_type=F32)
            st_ref[hd] = st * dec_ref[ci:ci + 1, sl] + upd

    for hd in range(nh):
        sl = slice(hd * HEAD, (hd + 1) * HEAD)
        g, hh = divmod(hd, hpg)
        oh = ocat_ref[:, sl]
        ms = jnp.mean(oh * oh, axis=-1, keepdims=True)
        gate = proj_ref[:, g * 4 * gw + 3 * gw + hh * HEAD:g * 4 * gw + 3 * gw + (hh + 1) * HEAD]
        ocat_ref[:, sl] = oh * lax.rsqrt(ms + NORM_EPS) * nw_ref[...] * _silu(gate)
    y = jnp.dot(ocat_ref[...].astype(BF16), wout_ref[...], preferred_element_type=F32)
    r = alpha * x + g_ref[0] * y
    o_ref[0] = _layer_norm(r, lng_ref[...], lnb_ref[...])


def _hgrn_layer(x, sh, sc, g, w_in, w_out, lb, norm_w, ln_g, ln_b, alpha):
    bsz, s, d = x.shape
    ts = min(256, s)
    nh = d // HEAD
    gw = min(HGRN_GROUP * HEAD, d)
    ng = d // gw
    w_grouped = w_in.reshape(d, 4, ng, gw).transpose(0, 2, 1, 3).reshape(d, 4 * d).astype(BF16)
    vec = pl.BlockSpec((1, 1, d), lambda b, i: (b, 0, 0))
    tile = pl.BlockSpec((1, ts, d), lambda b, i: (b, i, 0))
    return pl.pallas_call(
        functools.partial(_hgrn_kernel, alpha),
        grid=(bsz, s // ts),
        in_specs=[tile, vec, vec, vec,
                  _const_spec((d, 4 * d)), _const_spec((d, d)),
                  _const_spec((1, d)), _const_spec((1, d)), _const_spec((1, HEAD)),
                  _const_spec((1, d)), _const_spec((1, d))],
        out_specs=tile,
        out_shape=jax.ShapeDtypeStruct((bsz, s, d), F32),
        scratch_shapes=[pltpu.VMEM((ts, 4 * d), F32),
                        pltpu.VMEM((nh, HEAD, HEAD), F32),
                        pltpu.VMEM((ts, d), F32),
                        pltpu.VMEM((ts, d), F32), pltpu.VMEM((ts, d), F32),
                        pltpu.VMEM((ts, d), BF16), pltpu.VMEM((ts, d), BF16), pltpu.VMEM((ts, d), BF16),
                        pltpu.VMEM((ts // HGRN_CHUNK, d), F32)],
        compiler_params=_cparams(("arbitrary", "arbitrary")),
        name="hgrn_layer",
    )(x, sh, sc, g, w_grouped, w_out.astype(BF16),
      jnp.log(lb).reshape(1, d), (1.0 - lb).reshape(1, d), norm_w.reshape(1, HEAD),
      ln_g.reshape(1, d), ln_b.reshape(1, d))


def _attn_in_kernel(x_ref, sh_ref, sc_ref, w_ref, cos_ref, sin_ref, q_ref, k_ref, v_ref):
    d = x_ref.shape[2]
    x = x_ref[0]
    h = x * (1.0 + sc_ref[0]) + sh_ref[0]
    qkv = jnp.dot(h.astype(BF16), w_ref[...], preferred_element_type=F32)
    cos = cos_ref[...]
    sin = sin_ref[...]
    lane = lax.broadcasted_iota(I32, (1, LANES), 1)
    first_half = (lane % ATTN_D) < (ATTN_D // 2)
    scale = ATTN_D ** -0.5 * math.log2(math.e)
    for j in range(d // LANES):
        sl = slice(j * LANES, (j + 1) * LANES)
        for src, dst, mul in ((0, q_ref, scale), (d, k_ref, 1.0)):
            t = qkv[:, src + j * LANES:src + (j + 1) * LANES]
            partner = jnp.where(first_half, pltpu.roll(t, LANES - ATTN_D // 2, 1),
                                pltpu.roll(t, ATTN_D // 2, 1))
            dst[0, :, sl] = ((t * cos + partner * sin) * mul).astype(BF16)
    v_ref[0] = qkv[:, 2 * d:].astype(BF16)


def _flash_kernel(out_scale, lam_ref, q_ref, k_ref, v_ref, w_ref, o_ref, *scratch):
    tq = q_ref.shape[1]
    rows = 2 * tq
    qi = pl.program_id(2)
    nrep = tq // LANES
    rc = FLASH_ROW_CHUNK
    per_head = len(scratch) // FLASH_HEADS
    heads = [(hd,) + tuple(scratch[hd * per_head:(hd + 1) * per_head]) for hd in range(FLASH_HEADS)]
    lane = lax.broadcasted_iota(I32, (1, LANES), 1)

    for hd, m_ref, l_ref, acc_ref, _, _, _, qq_ref in heads:
        q = q_ref[0, :, hd * HEAD:(hd + 1) * HEAD]
        zero = jnp.zeros_like(q)
        qq_ref[0:tq, :] = jnp.where(lane < ATTN_D, q, zero)
        qq_ref[tq:rows, :] = jnp.where(lane >= ATTN_D, q, zero)
        m_ref[...] = jnp.full_like(m_ref, -jnp.inf)
        l_ref[...] = jnp.zeros_like(l_ref)
        acc_ref[...] = jnp.zeros_like(acc_ref)

    def scores(head, j):
        hd, _, _, _, _, s_ref, _, qq_ref = head
        kb = k_ref[0, pl.ds(pl.multiple_of(j * tq, tq), tq), hd * HEAD:(hd + 1) * HEAD]
        s_ref[...] = lax.dot_general(qq_ref[...], kb, NT_DIMS, preferred_element_type=F32)

    def softmax(head, masked):
        _, m_ref, l_ref, _, a_ref, s_ref, p_ref, _ = head
        for r0 in range(0, rows, rc):
            rs = slice(r0, r0 + rc)
            s = s_ref[rs, :]
            if masked:
                row = lax.broadcasted_iota(I32, (rc, tq), 0) + (r0 % tq)
                col = lax.broadcasted_iota(I32, (rc, tq), 1)
                s = jnp.where(row >= col, s, -jnp.inf)
            m_old = m_ref[rs, :]
            m_new = jnp.maximum(m_old, jnp.max(s, axis=-1, keepdims=True))
            a = jnp.exp2(m_old - m_new)
            p = jnp.exp2(s - jnp.concatenate([m_new] * nrep, axis=1))
            psum = p[:, 0:LANES]
            for r in range(1, nrep):
                psum = psum + p[:, r * LANES:(r + 1) * LANES]
            l_ref[rs, :] = a * l_ref[rs, :] + psum
            m_ref[rs, :] = m_new
            a_ref[rs, :] = a
            p_ref[rs, :] = p.astype(BF16)

    def values(head, j):
        hd, _, _, acc_ref, a_ref, _, p_ref, _ = head
        vb = v_ref[0, pl.ds(pl.multiple_of(j * tq, tq), tq), hd * HEAD:(hd + 1) * HEAD]
        acc_ref[...] = a_ref[...] * acc_ref[...] + jnp.dot(p_ref[...], vb, preferred_element_type=F32)

    nh_step = len(heads)
    n_blk = qi + 1

    def emit(base, k_lo, k_hi, n_items, masked_from):
        for k in range(k_lo, k_hi + 1):
            if k + 1 < n_items:
                scores(heads[(k + 1) % nh_step], base + (k + 1) // nh_step)
            if 0 <= k < n_items:
                softmax(heads[k % nh_step], k >= masked_from)
            if 0 <= k - 1 < n_items:
                values(heads[(k - 1) % nh_step], base + (k - 1) // nh_step)

    @pl.when(n_blk == 1)
    def _():
        emit(0, -1, nh_step, nh_step, 0)

    @pl.when(n_blk >= 2)
    def _():
        unbounded = 1 << 30
        emit(0, -1, 0, unbounded, unbounded)

        def body(t, carry):
            emit(t, 1, nh_step, unbounded, unbounded)
            return carry

        lax.fori_loop(0, n_blk - 2, body, 0)
        emit(n_blk - 2, 1, 2 * nh_step, 2 * nh_step, nh_step)

    for hd, _, l_ref, acc_ref, _, _, _, _ in heads:
        o_all = acc_ref[...] / jnp.sum(l_ref[...], axis=-1, keepdims=True)
        o = o_all[:tq] - lam_ref[0] * o_all[tq:]
        ms = jnp.mean(o * o, axis=-1, keepdims=True)
        o_ref[0, :, hd * HEAD:(hd + 1) * HEAD] = (
            o * lax.rsqrt(ms + NORM_EPS) * w_ref[...] * out_scale).astype(BF16)


def _resid_ln_kernel(alpha, x_ref, g_ref, o_in_ref, w_ref, lng_ref, lnb_ref, o_ref):
    y = jnp.dot(o_in_ref[0], w_ref[...], preferred_element_type=F32)
    r = alpha * x_ref[0] + g_ref[0] * y
    o_ref[0] = _layer_norm(r, lng_ref[...], lnb_ref[...])


def _attn_layer(x, sh, sc, g, w_in, w_out, lam_params, subln_w, lambda_init, ln_g, ln_b, alpha):
    bsz, s, d = x.shape
    nh = d // HEAD
    ts = min(512, s)
    tq = min(256, s)
    half = ATTN_D // 2
    inv_freq = ROPE_THETA ** (-jnp.arange(half, dtype=F32) / half)
    ang = jnp.arange(s, dtype=F32)[:, None] * inv_freq[None, :]
    cos_t = jnp.tile(jnp.cos(ang), (1, LANES // half))
    sin_h = jnp.sin(ang)
    sin_t = jnp.tile(jnp.concatenate([-sin_h, sin_h], axis=1), (1, LANES // ATTN_D))

    vec = pl.BlockSpec((1, 1, d), lambda b, i: (b, 0, 0))
    tile = pl.BlockSpec((1, ts, d), lambda b, i: (b, i, 0))
    rope = pl.BlockSpec((ts, LANES), lambda b, i: (i, 0))
    q, k, v = pl.pallas_call(
        _attn_in_kernel,
        grid=(bsz, s // ts),
        in_specs=[tile, vec, vec, _const_spec((d, 3 * d)), rope, rope],
        out_specs=[tile, tile, tile],
        out_shape=[jax.ShapeDtypeStruct((bsz, s, d), BF16)] * 3,
        compiler_params=_cparams(("arbitrary", "arbitrary")),
        name="attn_in",
    )(x, sh, sc, w_in.astype(BF16), cos_t, sin_t)

    lp = lam_params.astype(F32)
    lam = (jnp.exp(jnp.sum(lp[0] * lp[1])) - jnp.exp(jnp.sum(lp[2] * lp[3])) + lambda_init).reshape(1)
    hw = FLASH_HEADS * HEAD
    qspec = pl.BlockSpec((1, tq, hw), lambda b, h, i: (b, i, h))
    kvspec = pl.BlockSpec((1, s, hw), lambda b, h, i: (b, 0, h))
    head_scratch = [pltpu.VMEM((2 * tq, LANES), F32), pltpu.VMEM((2 * tq, LANES), F32),
                    pltpu.VMEM((2 * tq, HEAD), F32), pltpu.VMEM((2 * tq, LANES), F32),
                    pltpu.VMEM((2 * tq, tq), F32), pltpu.VMEM((2 * tq, tq), BF16),
                    pltpu.VMEM((2 * tq, HEAD), BF16)]
    o = pl.pallas_call(
        functools.partial(_flash_kernel, 1.0 - lambda_init),
        grid=(bsz, nh // FLASH_HEADS, s // tq),
        in_specs=[pl.BlockSpec(memory_space=pltpu.SMEM), qspec, kvspec, kvspec,
                  pl.BlockSpec((1, HEAD), lambda b, h, i: (0, 0))],
        out_specs=qspec,
        out_shape=jax.ShapeDtypeStruct((bsz, s, d), BF16),
        scratch_shapes=head_scratch * FLASH_HEADS,
        compiler_params=_cparams(("arbitrary", "arbitrary", "arbitrary")),
        name="diff_flash",
    )(lam, q, k, v, subln_w.reshape(1, HEAD))

    return pl.pallas_call(
        functools.partial(_resid_ln_kernel, alpha),
        grid=(bsz, s // ts),
        in_specs=[tile, vec, tile, _const_spec((d, d)), _const_spec((1, d)), _const_spec((1, d))],
        out_specs=tile,
        out_shape=jax.ShapeDtypeStruct((bsz, s, d), F32),
        compiler_params=_cparams(("arbitrary", "arbitrary")),
        name="attn_out",
    )(x, g, o, w_out.astype(BF16), ln_g.reshape(1, d), ln_b.reshape(1, d))


def _pack_row_parts(v):
    q = v.shape[1] // 4
    return tuple(lax.bitcast_convert_type(_pack_bf16_pair(v[:, p * q:(p + 1) * q], v[:, (2 + p) * q:(3 + p) * q]), I32)
                 for p in range(2))


def _unpack_row_parts(part_a, part_b):
    lo_a, hi_a = _unpack_bf16_pair(lax.bitcast_convert_type(part_a, U32))
    lo_b, hi_b = _unpack_bf16_pair(lax.bitcast_convert_type(part_b, U32))
    return lo_a, lo_b, hi_a, hi_b


def _router_kernel(x_ref, sh_ref, sc_ref, w_ref, bias_ref, tri_ref, hpa_ref, hpb_ref, mt_ref, mf_ref, cnt_ref,
                   cnt_scr):
    ts, d = x_ref.shape[1], x_ref.shape[2]
    first = (pl.program_id(0) == 0) & (pl.program_id(1) == 0)

    @pl.when(first)
    def _():
        cnt_scr[...] = jnp.zeros_like(cnt_scr)

    x = x_ref[0]
    h = x * (1.0 + sc_ref[0]) + sh_ref[0]
    hpa_ref[...], hpb_ref[...] = _pack_row_parts(h)

    h_hi = h.astype(BF16)
    h_lo = (h - h_hi.astype(F32)).astype(BF16)
    hh = jnp.dot(h_hi, w_ref[...], preferred_element_type=F32)
    lh = jnp.dot(h_lo, w_ref[:, 0:LANES], preferred_element_type=F32)
    logits = hh[:, 0:LANES] + hh[:, LANES:2 * LANES] + lh + bias_ref[...]
    lane = lax.broadcasted_iota(I32, (ts, LANES), 1)
    neg = -jnp.inf
    big = jnp.int32(LANES)
    is_g = lane < MOE_GROUPS
    gl = jnp.where(is_g, logits, neg)
    gmax = jnp.max(gl, axis=-1, keepdims=True)
    g_idx = jnp.min(jnp.where(gl == gmax, lane, big), axis=-1, keepdims=True)
    g_w = 1.0 / jnp.sum(jnp.exp(gl - gmax), axis=-1, keepdims=True)

    e_lane = lane - MOE_GROUPS
    in_grp = (e_lane >= g_idx * MOE_EPG) & (e_lane < (g_idx + 1) * MOE_EPG)
    el = jnp.where(in_grp, logits, neg)
    l1 = jnp.max(el, axis=-1, keepdims=True)
    i1 = jnp.min(jnp.where(el == l1, lane, big), axis=-1, keepdims=True)
    el2 = jnp.where(lane == i1, neg, el)
    l2 = jnp.max(el2, axis=-1, keepdims=True)
    i2 = jnp.min(jnp.where(el2 == l2, lane, big), axis=-1, keepdims=True)
    t = jnp.exp(l2 - l1)
    w1 = g_w / (1.0 + t)
    w2 = g_w * t / (1.0 + t)

    oh1 = (lane == i1)
    oh2 = (lane == i2)
    both = (oh1 | oh2).astype(BF16)
    before = jnp.dot(tri_ref[...], both, preferred_element_type=F32) + cnt_scr[...]
    r1 = jnp.sum(jnp.where(oh1, before, 0.0), axis=-1, keepdims=True)
    r2 = jnp.sum(jnp.where(oh2, before, 0.0), axis=-1, keepdims=True)
    cnt_scr[...] = cnt_scr[...] + jnp.sum(both.astype(F32), axis=0, keepdims=True)
    cnt_ref[...] = cnt_scr[...]

    e1 = i1 - MOE_GROUPS
    e2 = i2 - MOE_GROUPS
    meta = jnp.where(lane == 0, e1, jnp.where(lane == 1, e2, jnp.where(
        lane == 2, r1.astype(I32), jnp.where(lane == 3, r2.astype(I32), 0))))
    mt_ref[...] = jnp.transpose(meta)[0:8, :]
    mf_ref[...] = jnp.where(lane == 0, w1, jnp.where(lane == 1, w2, 0.0))


def _dest_kernel(ps_ref, mt_ref, d1_ref, d2_ref):
    e1 = mt_ref[0:1, :]
    e2 = mt_ref[1:2, :]
    p1 = jnp.zeros_like(e1)
    p2 = jnp.zeros_like(e2)
    for e in range(MOE_EXPERTS):
        p1 = jnp.where(e1 == e, ps_ref[e], p1)
        p2 = jnp.where(e2 == e, ps_ref[e], p2)
    d1_ref[...] = p1 + mt_ref[2:3, :]
    d2_ref[...] = p2 + mt_ref[3:4, :]


def _sc_mesh():
    return plsc.VectorSubcoreMesh(core_axis_name="c", subcore_axis_name="s")


def _sc_scatter_rows(src, idx_a, idx_b, n_rows):
    m, w = src.shape

    @functools.partial(pl.kernel, out_type=jax.ShapeDtypeStruct((n_rows, w), src.dtype),
                       mesh=_sc_mesh(), scratch_types=[], name="moe_sc_scatter")
    def scatter(x_hbm, ia_hbm, ib_hbm, o_hbm):
        def body(x_vmem, ia_vmem, ib_vmem):
            pltpu.sync_copy(x_vmem, o_hbm.at[ia_vmem.at[0]])
            pltpu.sync_copy(x_vmem, o_hbm.at[ib_vmem.at[0]])

        pltpu.emit_pipeline(
            body,
            grid=(m // SC_WINDOW,),
            in_specs=[pl.BlockSpec((SC_WINDOW, w), lambda i: (i, 0)),
                      pl.BlockSpec((1, SC_WINDOW), lambda i: (0, i)),
                      pl.BlockSpec((1, SC_WINDOW), lambda i: (0, i))],
            out_specs=[],
            core_axis_name=("c", "s"),
            dimension_semantics=(pltpu.PARALLEL,),
        )(x_hbm, ia_hbm, ib_hbm)

    return scatter(src, idx_a, idx_b)


def _sc_gather_rows(table, idx):
    m = idx.shape[1]
    w = table.shape[1]

    @functools.partial(pl.kernel, out_type=jax.ShapeDtypeStruct((m, w), table.dtype),
                       mesh=_sc_mesh(), scratch_types=[], name="moe_sc_gather")
    def gather(t_hbm, i_hbm, o_hbm):
        def body(i_vmem, o_vmem):
            pltpu.sync_copy(t_hbm.at[i_vmem.at[0]], o_vmem)

        pltpu.emit_pipeline(
            body,
            grid=(m // SC_WINDOW,),
            in_specs=[pl.BlockSpec((1, SC_WINDOW), lambda i: (0, i))],
            out_specs=[pl.BlockSpec((SC_WINDOW, w), lambda i: (i, 0))],
            core_axis_name=("c", "s"),
            dimension_semantics=(pltpu.PARALLEL,),
        )(i_hbm, o_hbm)

    return gather(table, idx)


def _expert_kernel(be_ref, nv_ref, xa_ref, xb_ref, w1_ref, w3_ref, w2_ref, ya_ref, yb_ref,
                   w1_scr, w3_scr, w2_scr):
    i = pl.program_id(0)
    blk = xa_ref.shape[0]
    n_valid = nv_ref[i]
    new_expert = (i == 0) | (be_ref[i] != be_ref[jnp.maximum(i - 1, 0)])

    @pl.when((n_valid > 0) & new_expert)
    def _():
        w1_scr[...] = w1_ref[0].astype(BF16)
        w3_scr[...] = w3_ref[0].astype(BF16)
        w2_scr[...] = w2_ref[0].astype(BF16)

    @pl.when(n_valid > 0)
    def _():
        valid = lax.broadcasted_iota(I32, (blk, 1), 0) < n_valid
        xq = _unpack_row_parts(xa_ref[...], xb_ref[...])
        xin = jnp.concatenate([jnp.where(valid, q, 0.0).astype(BF16) for q in xq], axis=1)
        a = jnp.dot(xin, w1_scr[...], preferred_element_type=F32)
        b = jnp.dot(xin, w3_scr[...], preferred_element_type=F32)
        hid = (_silu(a) * b).astype(BF16)
        y = jnp.dot(hid, w2_scr[...], preferred_element_type=F32)
        ya_ref[...], yb_ref[...] = _pack_row_parts(y)

    @pl.when(n_valid <= 0)
    def _():
        ya_ref[...] = jnp.zeros_like(ya_ref)
        yb_ref[...] = jnp.zeros_like(yb_ref)


def _combine_kernel(alpha, x_ref, g_ref, mf_ref, lng_ref, lnb_ref, y1a_ref, y1b_ref, y2a_ref, y2b_ref, o_ref):
    d = x_ref.shape[2]
    q = d // 4
    w1 = mf_ref[:, 0:1]
    w2 = mf_ref[:, 1:2]
    y1 = _unpack_row_parts(y1a_ref[...], y1b_ref[...])
    y2 = _unpack_row_parts(y2a_ref[...], y2b_ref[...])
    r = [alpha * x_ref[0, :, p * q:(p + 1) * q] + g_ref[0, :, p * q:(p + 1) * q] * (w1 * y1[p] + w2 * y2[p])
         for p in range(4)]
    mu = sum(jnp.sum(rp, axis=-1, keepdims=True) for rp in r) / d
    dev = [rp - mu for rp in r]
    var = sum(jnp.sum(dp * dp, axis=-1, keepdims=True) for dp in dev) / d
    inv = lax.rsqrt(var + NORM_EPS)
    for p in range(4):
        sl = slice(p * q, (p + 1) * q)
        o_ref[0, :, sl] = dev[p] * inv * lng_ref[:, sl] + lnb_ref[:, sl]


def _moe_layer(x, sh, sc, g, wg, bg, we, be, w1, w3, w2, layer, ln_g, ln_b, alpha):
    bsz, s, d = x.shape
    n = bsz * s
    quarter = d // 4
    e_num, blk = MOE_EXPERTS, MOE_BLOCK
    ff = w1.shape[-1]
    ts = min(512, s)
    nt_b = s // ts
    nt = n // ts
    n_rows = n * 2 + e_num * blk
    nb = n_rows // blk

    wcat = jnp.zeros((d, LANES), F32).at[:, :MOE_GROUPS].set(wg).at[:, MOE_GROUPS:MOE_GROUPS + e_num].set(we)
    bcat = jnp.zeros((1, LANES), F32).at[0, :MOE_GROUPS].set(bg).at[0, MOE_GROUPS:MOE_GROUPS + e_num].set(be)
    wcat_hi = wcat.astype(BF16)
    wcat_hl = jnp.concatenate([wcat_hi, (wcat - wcat_hi.astype(F32)).astype(BF16)], axis=1)
    tri = jnp.tri(ts, k=-1, dtype=BF16)

    vec = pl.BlockSpec((1, 1, d), lambda b, i: (b, 0, 0))
    tile = pl.BlockSpec((1, ts, d), lambda b, i: (b, i, 0))
    flat = lambda w: pl.BlockSpec((ts, w), lambda b, i: (b * nt_b + i, 0))
    hpa, hpb, mt, mf, cnt = pl.pallas_call(
        _router_kernel,
        grid=(bsz, nt_b),
        in_specs=[tile, vec, vec, _const_spec((d, 2 * LANES)), _const_spec((1, LANES)), _const_spec((ts, ts))],
        out_specs=[flat(quarter), flat(quarter), pl.BlockSpec((8, ts), lambda b, i: (0, b * nt_b + i)),
                   flat(LANES), _const_spec((1, LANES))],
        out_shape=[jax.ShapeDtypeStruct((n, quarter), I32), jax.ShapeDtypeStruct((n, quarter), I32),
                   jax.ShapeDtypeStruct((8, n), I32), jax.ShapeDtypeStruct((n, LANES), F32),
                   jax.ShapeDtypeStruct((1, LANES), F32)],
        scratch_shapes=[pltpu.VMEM((1, LANES), F32)],
        compiler_params=_cparams(("arbitrary", "arbitrary")),
        name="moe_router",
    )(x, sh, sc, wcat_hl, bcat, tri)

    counts = cnt[0, MOE_GROUPS:MOE_GROUPS + e_num].astype(I32)
    padded = ((counts + blk - 1) // blk) * blk
    pad_end = jnp.cumsum(padded)
    pad_start = pad_end - padded
    blk_start = jnp.arange(nb, dtype=I32) * blk
    blk_expert = jnp.minimum(jnp.sum((pad_end[None, :] <= blk_start[:, None]).astype(I32), axis=1), e_num - 1)
    blk_valid = jnp.clip(pad_start[blk_expert] + counts[blk_expert] - blk_start, 0, blk).astype(I32)

    td = min(8192, n)
    dest1, dest2 = pl.pallas_call(
        _dest_kernel,
        grid_spec=pltpu.PrefetchScalarGridSpec(
            num_scalar_prefetch=1,
            grid=(n // td,),
            in_specs=[pl.BlockSpec((8, td), lambda i, ps_r: (0, i))],
            out_specs=[pl.BlockSpec((1, td), lambda i, ps_r: (0, i))] * 2,
        ),
        out_shape=[jax.ShapeDtypeStruct((1, n), I32)] * 2,
        compiler_params=_cparams(("arbitrary",)),
        name="moe_dest",
    )(pad_start.astype(I32), mt)

    xa = _sc_scatter_rows(hpa, dest1, dest2, n_rows)
    xb = _sc_scatter_rows(hpb, dest1, dest2, n_rows)

    rows_spec = pl.BlockSpec((blk, quarter), lambda i, be_r, nv_r: (i, 0))
    ya, yb = pl.pallas_call(
        _expert_kernel,
        grid_spec=pltpu.PrefetchScalarGridSpec(
            num_scalar_prefetch=2,
            grid=(nb,),
            in_specs=[rows_spec, rows_spec,
                      pl.BlockSpec((1, d, ff), lambda i, be_r, nv_r: (layer * e_num + be_r[i], 0, 0)),
                      pl.BlockSpec((1, d, ff), lambda i, be_r, nv_r: (layer * e_num + be_r[i], 0, 0)),
                      pl.BlockSpec((1, ff, d), lambda i, be_r, nv_r: (layer * e_num + be_r[i], 0, 0))],
            out_specs=[rows_spec, rows_spec],
            scratch_shapes=[pltpu.VMEM((d, ff), BF16), pltpu.VMEM((d, ff), BF16), pltpu.VMEM((ff, d), BF16)],
        ),
        out_shape=[jax.ShapeDtypeStruct((n_rows, quarter), I32)] * 2,
        compiler_params=_cparams(("arbitrary",)),
        name="moe_experts",
    )(blk_expert, blk_valid, xa, xb, w1, w3, w2)

    dest12 = jnp.concatenate([dest1, dest2], axis=1)
    ga = _sc_gather_rows(ya, dest12)
    gb = _sc_gather_rows(yb, dest12)

    first = pl.BlockSpec((ts, quarter), lambda b, i: (b * nt_b + i, 0))
    second = pl.BlockSpec((ts, quarter), lambda b, i: (nt + b * nt_b + i, 0))
    return pl.pallas_call(
        functools.partial(_combine_kernel, alpha),
        grid=(bsz, nt_b),
        in_specs=[tile, vec, flat(LANES), _const_spec((1, d)), _const_spec((1, d)),
                  first, first, second, second],
        out_specs=tile,
        out_shape=jax.ShapeDtypeStruct((bsz, s, d), F32),
        compiler_params=_cparams(("arbitrary", "arbitrary")),
        name="moe_combine",
    )(x, g, mf, ln_g.reshape(1, d), ln_b.reshape(1, d), ga, gb, ga, gb)


def kernel(x, c, ada_w, ada_b, ln_g, ln_b, hgrn_w_in, hgrn_w_out, hgrn_lb, hgrn_norm_w, attn_w_in, attn_w_out,
           attn_lambda, attn_subln_w, router_g_w, router_g_b, router_e_w, router_e_b, moe_w1, moe_w3, moe_w2):
    depth = ada_w.shape[0]
    bsz, s, d = x.shape
    alpha = (2 * depth) ** 0.25
    lb_all = jnp.cumsum(jax.nn.softmax(hgrn_lb.astype(F32), axis=0), axis=0)
    lb_all = lb_all - lb_all[0:1]
    mod = _ada_mod(c, ada_w, ada_b).reshape(depth, bsz, 6, 1, d)
    w1_all = moe_w1.reshape((-1,) + moe_w1.shape[2:])
    w3_all = moe_w3.reshape((-1,) + moe_w3.shape[2:])
    w2_all = moe_w2.reshape((-1,) + moe_w2.shape[2:])
    for i in range(depth):
        sh1, sc1, g1, sh2, sc2, g2 = (mod[i, :, m] for m in range(6))
        j = i // 2
        if i % 2 == 0:
            x = _hgrn_layer(x, sh1, sc1, g1, hgrn_w_in[j], hgrn_w_out[j], lb_all[j], hgrn_norm_w[j],
                            ln_g[i, 0], ln_b[i, 0], alpha)
        else:
            lambda_init = 0.8 - 0.6 * math.exp(-0.3 * i)
            x = _attn_layer(x, sh1, sc1, g1, attn_w_in[j], attn_w_out[j], attn_lambda[j], attn_subln_w[j],
                            lambda_init, ln_g[i, 0], ln_b[i, 0], alpha)
        x = _moe_layer(x, sh2, sc2, g2, router_g_w[i], router_g_b[i], router_e_w[i], router_e_b[i],
                       w1_all, w3_all, w2_all, i, ln_g[i, 1], ln_b[i, 1], alpha)
    return x
```

```python
import functools
import math

import jax
import jax.numpy as jnp
from jax import lax
from jax.experimental import pallas as pl
from jax.experimental.pallas import tpu as pltpu
from jax.experimental.pallas import tpu_sc as plsc

F32 = jnp.float32
BF16 = jnp.bfloat16
I32 = jnp.int32
U32 = jnp.uint32
HIGHEST = lax.Precision.HIGHEST

LANES = 128
HEAD = 128
HGRN_CHUNK = 32
HGRN_ROWS = 2
HGRN_GROUP = 2
HGRN_SAFE_SPAN = 80.0
ATTN_D = 64
FLASH_HEADS = 8
FLASH_ROW_CHUNK = 32
ROPE_THETA = 10000.0
MOE_GROUPS = 4
MOE_EPG = 8
MOE_EXPERTS = MOE_GROUPS * MOE_EPG
MOE_BLOCK = 512
SC_WINDOW = 128
NORM_EPS = 1e-5
VMEM_LIMIT = 56 * 1024 * 1024

NT_DIMS = (((1,), (1,)), ((), ()))
TN_DIMS = (((0,), (0,)), ((), ()))


def _cparams(sem):
    return pltpu.CompilerParams(dimension_semantics=sem, vmem_limit_bytes=VMEM_LIMIT)


def _const_spec(shape):
    nd = len(shape)
    return pl.BlockSpec(shape, lambda *_: (0,) * nd)


def _layer_norm(r, g, b):
    mu = jnp.mean(r, axis=-1, keepdims=True)
    d = r - mu
    var = jnp.mean(d * d, axis=-1, keepdims=True)
    return d * lax.rsqrt(var + NORM_EPS) * g + b


def _silu(x):
    return x * (1.0 / (1.0 + jnp.exp(-x)))


def _pack_bf16_pair(lo, hi):
    lo_b = lax.bitcast_convert_type(lo.astype(BF16).astype(F32), U32)
    hi_b = lax.bitcast_convert_type(hi.astype(BF16).astype(F32), U32)
    return (hi_b & jnp.uint32(0xFFFF0000)) | (lo_b >> 16)


def _unpack_bf16_pair(u):
    lo = lax.bitcast_convert_type(u << 16, F32)
    hi = lax.bitcast_convert_type(u & jnp.uint32(0xFFFF0000), F32)
    return lo, hi


def _ada_kernel(c_ref, w_ref, b_ref, o_ref):
    c = c_ref[...]
    o_ref[0] = jnp.dot(_silu(c), w_ref[0], precision=HIGHEST, preferred_element_type=F32) + b_ref[0]


def _ada_mod(c, ada_w, ada_b):
    depth, d, n6 = ada_w.shape
    bsz = c.shape[0]
    tn = d
    return pl.pallas_call(
        _ada_kernel,
        grid=(depth, n6 // tn),
        in_specs=[
            pl.BlockSpec((bsz, d), lambda i, j: (0, 0)),
            pl.BlockSpec((1, d, tn), lambda i, j: (i, 0, j)),
            pl.BlockSpec((1, 1, tn), lambda i, j: (i, 0, j)),
        ],
        out_specs=pl.BlockSpec((1, bsz, tn), lambda i, j: (i, 0, j)),
        out_shape=jax.ShapeDtypeStruct((depth, bsz, n6), F32),
        compiler_params=_cparams(("arbitrary", "arbitrary")),
        name="ada_mod",
    )(c, ada_w, ada_b.reshape(depth, 1, n6))


def _hgrn_kernel(alpha, x_ref, sh_ref, sc_ref, g_ref, win_ref, wout_ref, loglb_ref, oml_ref,
                 nw_ref, lng_ref, lnb_ref, o_ref,
                 proj_ref, st_ref, ocat_ref, b_ref, kk_ref, qi_ref, ks_ref, vb_ref, dec_ref):
    nrow, ts, d = x_ref.shape
    nh = d // HEAD
    c = HGRN_CHUNK
    nc = ts // c
    gw = min(HGRN_GROUP * HEAD, d)
    ng = d // gw
    hpg = gw // HEAD

    @pl.when(pl.program_id(1) == 0)
    def _():
        st_ref[...] = jnp.zeros_like(st_ref)

    hb = [(x_ref[r] * (1.0 + sc_ref[r]) + sh_ref[r]).astype(BF16) for r in range(nrow)]
    pos = lax.broadcasted_iota(I32, (ts, 1), 0) % c
    row = lax.broadcasted_iota(I32, (ts, ts), 0)
    col = lax.broadcasted_iota(I32, (ts, ts), 1)
    keep = (row >= col) & (row // c == col // c)
    span = jnp.float32(0.0)

    def project(r, g):
        ps = slice(g * 4 * gw, (g + 1) * 4 * gw)
        proj_ref[r, :, ps] = jnp.dot(hb[r], win_ref[:, ps], preferred_element_type=F32)

    for r in range(nrow):
        project(r, 0)
    for g, r in [(g, r) for g in range(ng) for r in range(nrow)]:
        if g + 1 < ng:
            project(r, g + 1)
        cs = slice(g * gw, (g + 1) * gw)
        p0 = g * 4 * gw
        q2 = proj_ref[r, :, p0:p0 + gw]
        z = proj_ref[r, :, p0 + gw:p0 + 2 * gw]

        ls = jnp.minimum(z, 0.0) - jnp.log(1.0 + jnp.exp(-jnp.abs(z)))
        lsn = ls - z
        cc = loglb_ref[:, cs] + lsn
        log_f = jnp.maximum(ls, cc) + jnp.log(1.0 + jnp.exp(-jnp.abs(ls - cc)))
        kk = oml_ref[:, cs] * jnp.exp(lsn)

        b = log_f
        step = 1
        while step < c:
            b = b + jnp.where(pos >= step, pltpu.roll(b, step, 0), 0.0)
            step *= 2
        b_ref[r, :, cs] = b
        kk_ref[r, :, cs] = kk

        b3 = b.reshape(nc, c, gw)
        b_last = b3[:, c - 1:c, :]
        b_mid = b3[:, c // 2 - 1:c // 2, :]
        q3 = q2.reshape(nc, c, gw)
        k3 = kk.reshape(nc, c, gw)
        qi_ref[r, :, cs] = (q3 * jnp.exp(b3)).astype(BF16).reshape(ts, gw)
        ks_ref[r, :, cs] = (k3 * jnp.exp(b_last - b3)).astype(BF16).reshape(ts, gw)
        q_intra = (q3 * jnp.exp(b3 - b_mid)).astype(BF16).reshape(ts, gw)
        k_intra = (k3 * jnp.exp(b_mid - b3)).astype(BF16).reshape(ts, gw)
        dec_ref[r, :, cs] = jnp.exp(b_last).reshape(nc, gw)
        span = jnp.maximum(span, jnp.max(-b_last))
        v2 = proj_ref[r, :, p0 + 2 * gw:p0 + 3 * gw].astype(BF16)
        vb_ref[r, :, cs] = v2

        for hh in range(hpg):
            sl = slice(hh * HEAD, (hh + 1) * HEAD)
            sc = lax.dot_general(q_intra[:, sl], k_intra[:, sl], NT_DIMS, preferred_element_type=F32)
            p = jnp.where(keep, sc, 0.0).astype(BF16)
            ocat_ref[r, :, g * gw + hh * HEAD:g * gw + (hh + 1) * HEAD] = jnp.dot(
                p, v2[:, sl], preferred_element_type=F32)

    @pl.when(span > HGRN_SAFE_SPAN)
    def _():
        tpos = lax.broadcasted_iota(I32, (ts, 1), 0)
        for r in range(nrow):
            b = b_ref[r]
            kk = kk_ref[r]
            q2 = jnp.concatenate([proj_ref[r, :, g * 4 * gw:g * 4 * gw + gw] for g in range(ng)], axis=1)
            diag = q2 * kk
            scores = [jnp.where(row == col, jnp.sum(diag[:, hd * HEAD:(hd + 1) * HEAD], axis=-1, keepdims=True), 0.0)
                      for hd in range(nh)]
            size = 2
            while size <= c:
                ref_row = (row // size) * size + (size // 2 - 1)
                pick = jnp.where(col == ref_row, 1.0, 0.0)
                b_at = jnp.dot(pick, b, precision=HIGHEST, preferred_element_type=F32)
                right = (tpos % size) >= (size // 2)
                qa = jnp.where(right, q2 * jnp.exp(b - b_at), 0.0).astype(BF16)
                ka = jnp.where(right, 0.0, kk * jnp.exp(b_at - b)).astype(BF16)
                same = (row // size) == (col // size)
                for hd in range(nh):
                    sl = slice(hd * HEAD, (hd + 1) * HEAD)
                    sc = lax.dot_general(qa[:, sl], ka[:, sl], NT_DIMS, preferred_element_type=F32)
                    scores[hd] = scores[hd] + jnp.where(same, sc, 0.0)
                size *= 2
            for hd in range(nh):
                sl = slice(hd * HEAD, (hd + 1) * HEAD)
                ocat_ref[r, :, sl] = jnp.dot(scores[hd].astype(BF16), vb_ref[r, :, sl],
                                             preferred_element_type=F32)

    for ci in range(nc):
        rs = slice(ci * c, (ci + 1) * c)
        for r, hd in [(r, hd) for hd in range(nh) for r in range(nrow)]:
            sl = slice(hd * HEAD, (hd + 1) * HEAD)
            st = st_ref[r, hd]
            o_inter = lax.dot_general(qi_ref[r, rs, sl], st.astype(BF16), NT_DIMS,
                                      preferred_element_type=F32)
            ocat_ref[r, rs, sl] += o_inter
            upd = lax.dot_general(vb_ref[r, rs, sl], ks_ref[r, rs, sl], TN_DIMS,
                                  preferred_element_type=F32)
            st_ref[r, hd] = st * dec_ref[r, ci:ci + 1, sl] + upd

    for r in range(nrow):
        for hd in range(nh):
            sl = slice(hd * HEAD, (hd + 1) * HEAD)
            g, hh = divmod(hd, hpg)
            oh = ocat_ref[r, :, sl]
            ms = jnp.mean(oh * oh, axis=-1, keepdims=True)
            gate = proj_ref[r, :, g * 4 * gw + 3 * gw + hh * HEAD:g * 4 * gw + 3 * gw + (hh + 1) * HEAD]
            ocat_ref[r, :, sl] = oh * lax.rsqrt(ms + NORM_EPS) * nw_ref[...] * _silu(gate)
        y = jnp.dot(ocat_ref[r].astype(BF16), wout_ref[...], preferred_element_type=F32)
        res = alpha * x_ref[r] + g_ref[r] * y
        o_ref[r] = _layer_norm(res, lng_ref[...], lnb_ref[...])


def _hgrn_layer(x, sh, sc, g, w_in, w_out, lb, norm_w, ln_g, ln_b, alpha):
    bsz, s, d = x.shape
    ts = min(256, s)
    nh = d // HEAD
    gw = min(HGRN_GROUP * HEAD, d)
    ng = d // gw
    w_grouped = w_in.reshape(d, 4, ng, gw).transpose(0, 2, 1, 3).reshape(d, 4 * d).astype(BF16)
    nr = math.gcd(HGRN_ROWS, bsz)
    vec = pl.BlockSpec((nr, 1, d), lambda b, i: (b, 0, 0))
    tile = pl.BlockSpec((nr, ts, d), lambda b, i: (b, i, 0))
    return pl.pallas_call(
        functools.partial(_hgrn_kernel, alpha),
        grid=(bsz // nr, s // ts),
        in_specs=[tile, vec, vec, vec,
                  _const_spec((d, 4 * d)), _const_spec((d, d)),
                  _const_spec((1, d)), _const_spec((1, d)), _const_spec((1, HEAD)),
                  _const_spec((1, d)), _const_spec((1, d))],
        out_specs=tile,
        out_shape=jax.ShapeDtypeStruct((bsz, s, d), F32),
        scratch_shapes=[pltpu.VMEM((nr, ts, 4 * d), F32),
                        pltpu.VMEM((nr, nh, HEAD, HEAD), F32),
                        pltpu.VMEM((nr, ts, d), F32),
                        pltpu.VMEM((nr, ts, d), F32), pltpu.VMEM((nr, ts, d), F32),
                        pltpu.VMEM((nr, ts, d), BF16), pltpu.VMEM((nr, ts, d), BF16),
                        pltpu.VMEM((nr, ts, d), BF16),
                        pltpu.VMEM((nr, ts // HGRN_CHUNK, d), F32)],
        compiler_params=_cparams(("arbitrary", "arbitrary")),
        name="hgrn_layer",
    )(x, sh, sc, g, w_grouped, w_out.astype(BF16),
      jnp.log(lb).reshape(1, d), (1.0 - lb).reshape(1, d), norm_w.reshape(1, HEAD),
      ln_g.reshape(1, d), ln_b.reshape(1, d))


def _attn_in_kernel(x_ref, sh_ref, sc_ref, w_ref, cos_ref, sin_ref, q_ref, k_ref, v_ref):
    d = x_ref.shape[2]
    x = x_ref[0]
    h = x * (1.0 + sc_ref[0]) + sh_ref[0]
    qkv = jnp.dot(h.astype(BF16), w_ref[...], preferred_element_type=F32)
    cos = cos_ref[...]
    sin = sin_ref[...]
    lane = lax.broadcasted_iota(I32, (1, LANES), 1)
    first_half = (lane % ATTN_D) < (ATTN_D // 2)
    scale = ATTN_D ** -0.5 * math.log2(math.e)
    for j in range(d // LANES):
        sl = slice(j * LANES, (j + 1) * LANES)
        for src, dst, mul in ((0, q_ref, scale), (d, k_ref, 1.0)):
            t = qkv[:, src + j * LANES:src + (j + 1) * LANES]
            partner = jnp.where(first_half, pltpu.roll(t, LANES - ATTN_D // 2, 1),
                                pltpu.roll(t, ATTN_D // 2, 1))
            dst[0, :, sl] = ((t * cos + partner * sin) * mul).astype(BF16)
    v_ref[0] = qkv[:, 2 * d:].astype(BF16)


def _flash_kernel(out_scale, lam_ref, q_ref, k_ref, v_ref, w_ref, o_ref, *scratch):
    tq = q_ref.shape[1]
    rows = 2 * tq
    qi = pl.program_id(2)
    nrep = tq // LANES
    rc = FLASH_ROW_CHUNK
    n_heads = q_ref.shape[2] // HEAD
    per_head = len(scratch) // n_heads
    heads = [(hd,) + tuple(scratch[hd * per_head:(hd + 1) * per_head]) for hd in range(n_heads)]
    lane = lax.broadcasted_iota(I32, (1, LANES), 1)

    for hd, m_ref, l_ref, acc_ref, _, _, _, qq_ref in heads:
        q = q_ref[0, :, hd * HEAD:(hd + 1) * HEAD]
        zero = jnp.zeros_like(q)
        qq_ref[0:tq, :] = jnp.where(lane < ATTN_D, q, zero)
        qq_ref[tq:rows, :] = jnp.where(lane >= ATTN_D, q, zero)
        m_ref[...] = jnp.full_like(m_ref, -jnp.inf)
        l_ref[...] = jnp.zeros_like(l_ref)
        acc_ref[...] = jnp.zeros_like(acc_ref)

    def scores(head, j):
        hd, _, _, _, _, s_ref, _, qq_ref = head
        kb = k_ref[0, pl.ds(pl.multiple_of(j * tq, tq), tq), hd * HEAD:(hd + 1) * HEAD]
        s_ref[...] = lax.dot_general(qq_ref[...], kb, NT_DIMS, preferred_element_type=F32)

    def softmax(head, masked):
        _, m_ref, l_ref, _, a_ref, s_ref, p_ref, _ = head
        for r0 in range(0, rows, rc):
            rs = slice(r0, r0 + rc)
            s = s_ref[rs, :]
            if masked:
                row = lax.broadcasted_iota(I32, (rc, tq), 0) + (r0 % tq)
                col = lax.broadcasted_iota(I32, (rc, tq), 1)
                s = jnp.where(row >= col, s, -jnp.inf)
            m_old = m_ref[rs, :]
            m_new = jnp.maximum(m_old, jnp.max(s, axis=-1, keepdims=True))
            a = jnp.exp2(m_old - m_new)
            p = jnp.exp2(s - jnp.concatenate([m_new] * nrep, axis=1))
            psum = p[:, 0:LANES]
            for r in range(1, nrep):
                psum = psum + p[:, r * LANES:(r + 1) * LANES]
            l_ref[rs, :] = a * l_ref[rs, :] + psum
            m_ref[rs, :] = m_new
            a_ref[rs, :] = a
            p_ref[rs, :] = p.astype(BF16)

    def values(head, j):
        hd, _, _, acc_ref, a_ref, _, p_ref, _ = head
        vb = v_ref[0, pl.ds(pl.multiple_of(j * tq, tq), tq), hd * HEAD:(hd + 1) * HEAD]
        acc_ref[...] = a_ref[...] * acc_ref[...] + jnp.dot(p_ref[...], vb, preferred_element_type=F32)

    nh_step = len(heads)
    n_blk = qi + 1

    def emit(base, k_lo, k_hi, n_items, masked_from):
        for k in range(k_lo, k_hi + 1):
            if k + 1 < n_items:
                scores(heads[(k + 1) % nh_step], base + (k + 1) // nh_step)
            if 0 <= k < n_items:
                softmax(heads[k % nh_step], k >= masked_from)
            if 0 <= k - 1 < n_items:
                values(heads[(k - 1) % nh_step], base + (k - 1) // nh_step)

    @pl.when(n_blk == 1)
    def _():
        emit(0, -1, nh_step, nh_step, 0)

    @pl.when(n_blk >= 2)
    def _():
        unbounded = 1 << 30
        emit(0, -1, 0, unbounded, unbounded)

        def body(t, carry):
            emit(t, 1, nh_step, unbounded, unbounded)
            return carry

        lax.fori_loop(0, n_blk - 2, body, 0)
        emit(n_blk - 2, 1, 2 * nh_step, 2 * nh_step, nh_step)

    for hd, _, l_ref, acc_ref, _, _, _, _ in heads:
        o_all = acc_ref[...] / jnp.sum(l_ref[...], axis=-1, keepdims=True)
        o = o_all[:tq] - lam_ref[0] * o_all[tq:]
        ms = jnp.mean(o * o, axis=-1, keepdims=True)
        o_ref[0, :, hd * HEAD:(hd + 1) * HEAD] = (
            o * lax.rsqrt(ms + NORM_EPS) * w_ref[...] * out_scale).astype(BF16)


def _resid_ln_kernel(alpha, x_ref, g_ref, o_in_ref, w_ref, lng_ref, lnb_ref, o_ref):
    y = jnp.dot(o_in_ref[0], w_ref[...], preferred_element_type=F32)
    r = alpha * x_ref[0] + g_ref[0] * y
    o_ref[0] = _layer_norm(r, lng_ref[...], lnb_ref[...])


def _attn_layer(x, sh, sc, g, w_in, w_out, lam_params, subln_w, lambda_init, ln_g, ln_b, alpha):
    bsz, s, d = x.shape
    nh = d // HEAD
    ts = min(512, s)
    tq = min(256, s)
    half = ATTN_D // 2
    inv_freq = ROPE_THETA ** (-jnp.arange(half, dtype=F32) / half)
    ang = jnp.arange(s, dtype=F32)[:, None] * inv_freq[None, :]
    cos_t = jnp.tile(jnp.cos(ang), (1, LANES // half))
    sin_h = jnp.sin(ang)
    sin_t = jnp.tile(jnp.concatenate([-sin_h, sin_h], axis=1), (1, LANES // ATTN_D))

    vec = pl.BlockSpec((1, 1, d), lambda b, i: (b, 0, 0))
    tile = pl.BlockSpec((1, ts, d), lambda b, i: (b, i, 0))
    rope = pl.BlockSpec((ts, LANES), lambda b, i: (i, 0))
    q, k, v = pl.pallas_call(
        _attn_in_kernel,
        grid=(bsz, s // ts),
        in_specs=[tile, vec, vec, _const_spec((d, 3 * d)), rope, rope],
        out_specs=[tile, tile, tile],
        out_shape=[jax.ShapeDtypeStruct((bsz, s, d), BF16)] * 3,
        compiler_params=_cparams(("arbitrary", "arbitrary")),
        name="attn_in",
    )(x, sh, sc, w_in.astype(BF16), cos_t, sin_t)

    lp = lam_params.astype(F32)
    lam = (jnp.exp(jnp.sum(lp[0] * lp[1])) - jnp.exp(jnp.sum(lp[2] * lp[3])) + lambda_init).reshape(1)
    heads_per_step = math.gcd(FLASH_HEADS, nh)
    hw = heads_per_step * HEAD
    qspec = pl.BlockSpec((1, tq, hw), lambda b, h, i: (b, i, h))
    kvspec = pl.BlockSpec((1, s, hw), lambda b, h, i: (b, 0, h))
    head_scratch = [pltpu.VMEM((2 * tq, LANES), F32), pltpu.VMEM((2 * tq, LANES), F32),
                    pltpu.VMEM((2 * tq, HEAD), F32), pltpu.VMEM((2 * tq, LANES), F32),
                    pltpu.VMEM((2 * tq, tq), F32), pltpu.VMEM((2 * tq, tq), BF16),
                    pltpu.VMEM((2 * tq, HEAD), BF16)]
    o = pl.pallas_call(
        functools.partial(_flash_kernel, 1.0 - lambda_init),
        grid=(bsz, nh // heads_per_step, s // tq),
        in_specs=[pl.BlockSpec(memory_space=pltpu.SMEM), qspec, kvspec, kvspec,
                  pl.BlockSpec((1, HEAD), lambda b, h, i: (0, 0))],
        out_specs=qspec,
        out_shape=jax.ShapeDtypeStruct((bsz, s, d), BF16),
        scratch_shapes=head_scratch * heads_per_step,
        compiler_params=_cparams(("arbitrary", "arbitrary", "arbitrary")),
        name="diff_flash",
    )(lam, q, k, v, subln_w.reshape(1, HEAD))

    return pl.pallas_call(
        functools.partial(_resid_ln_kernel, alpha),
        grid=(bsz, s // ts),
        in_specs=[tile, vec, tile, _const_spec((d, d)), _const_spec((1, d)), _const_spec((1, d))],
        out_specs=tile,
        out_shape=jax.ShapeDtypeStruct((bsz, s, d), F32),
        compiler_params=_cparams(("arbitrary", "arbitrary")),
        name="attn_out",
    )(x, g, o, w_out.astype(BF16), ln_g.reshape(1, d), ln_b.reshape(1, d))


def _pack_row_parts(v):
    q = v.shape[1] // 4
    return tuple(lax.bitcast_convert_type(_pack_bf16_pair(v[:, p * q:(p + 1) * q], v[:, (2 + p) * q:(3 + p) * q]), I32)
                 for p in range(2))


def _unpack_row_parts(part_a, part_b):
    lo_a, hi_a = _unpack_bf16_pair(lax.bitcast_convert_type(part_a, U32))
    lo_b, hi_b = _unpack_bf16_pair(lax.bitcast_convert_type(part_b, U32))
    return lo_a, lo_b, hi_a, hi_b


def _router_kernel(x_ref, sh_ref, sc_ref, w_ref, bias_ref, tri_ref, hpa_ref, hpb_ref, mt_ref, mf_ref, cnt_ref,
                   cnt_scr):
    ts, d = x_ref.shape[1], x_ref.shape[2]
    first = (pl.program_id(0) == 0) & (pl.program_id(1) == 0)

    @pl.when(first)
    def _():
        cnt_scr[...] = jnp.zeros_like(cnt_scr)

    x = x_ref[0]
    h = x * (1.0 + sc_ref[0]) + sh_ref[0]
    hpa_ref[...], hpb_ref[...] = _pack_row_parts(h)

    h_hi = h.astype(BF16)
    h_lo = (h - h_hi.astype(F32)).astype(BF16)
    hh = jnp.dot(h_hi, w_ref[...], preferred_element_type=F32)
    lh = jnp.dot(h_lo, w_ref[:, 0:LANES], preferred_element_type=F32)
    logits = hh[:, 0:LANES] + hh[:, LANES:2 * LANES] + lh + bias_ref[...]
    lane = lax.broadcasted_iota(I32, (ts, LANES), 1)
    neg = -jnp.inf
    big = jnp.int32(LANES)
    is_g = lane < MOE_GROUPS
    gl = jnp.where(is_g, logits, neg)
    gmax = jnp.max(gl, axis=-1, keepdims=True)
    g_idx = jnp.min(jnp.where(gl == gmax, lane, big), axis=-1, keepdims=True)
    g_w = 1.0 / jnp.sum(jnp.exp(gl - gmax), axis=-1, keepdims=True)

    e_lane = lane - MOE_GROUPS
    in_grp = (e_lane >= g_idx * MOE_EPG) & (e_lane < (g_idx + 1) * MOE_EPG)
    el = jnp.where(in_grp, logits, neg)
    l1 = jnp.max(el, axis=-1, keepdims=True)
    i1 = jnp.min(jnp.where(el == l1, lane, big), axis=-1, keepdims=True)
    el2 = jnp.where(lane == i1, neg, el)
    l2 = jnp.max(el2, axis=-1, keepdims=True)
    i2 = jnp.min(jnp.where(el2 == l2, lane, big), axis=-1, keepdims=True)
    t = jnp.exp(l2 - l1)
    w1 = g_w / (1.0 + t)
    w2 = g_w * t / (1.0 + t)

    oh1 = (lane == i1)
    oh2 = (lane == i2)
    both = (oh1 | oh2).astype(BF16)
    before = jnp.dot(tri_ref[...], both, preferred_element_type=F32) + cnt_scr[...]
    r1 = jnp.sum(jnp.where(oh1, before, 0.0), axis=-1, keepdims=True)
    r2 = jnp.sum(jnp.where(oh2, before, 0.0), axis=-1, keepdims=True)
    cnt_scr[...] = cnt_scr[...] + jnp.sum(both.astype(F32), axis=0, keepdims=True)
    cnt_ref[...] = cnt_scr[...]

    e1 = i1 - MOE_GROUPS
    e2 = i2 - MOE_GROUPS
    meta = jnp.where(lane == 0, e1, jnp.where(lane == 1, e2, jnp.where(
        lane == 2, r1.astype(I32), jnp.where(lane == 3, r2.astype(I32), 0))))
    mt_ref[...] = jnp.transpose(meta)[0:8, :]
    mf_ref[...] = jnp.where(lane == 0, w1, jnp.where(lane == 1, w2, 0.0))


def _dest_kernel(ps_ref, mt_ref, d1_ref, d2_ref):
    e1 = mt_ref[0:1, :]
    e2 = mt_ref[1:2, :]
    p1 = jnp.zeros_like(e1)
    p2 = jnp.zeros_like(e2)
    for e in range(MOE_EXPERTS):
        p1 = jnp.where(e1 == e, ps_ref[e], p1)
        p2 = jnp.where(e2 == e, ps_ref[e], p2)
    d1_ref[...] = p1 + mt_ref[2:3, :]
    d2_ref[...] = p2 + mt_ref[3:4, :]


def _sc_mesh():
    return plsc.VectorSubcoreMesh(core_axis_name="c", subcore_axis_name="s")


def _sc_scatter_rows(src, idx_a, idx_b, n_rows):
    m, w = src.shape

    @functools.partial(pl.kernel, out_type=jax.ShapeDtypeStruct((n_rows, w), src.dtype),
                       mesh=_sc_mesh(), scratch_types=[], name="moe_sc_scatter")
    def scatter(x_hbm, ia_hbm, ib_hbm, o_hbm):
        def body(x_vmem, ia_vmem, ib_vmem):
            pltpu.sync_copy(x_vmem, o_hbm.at[ia_vmem.at[0]])
            pltpu.sync_copy(x_vmem, o_hbm.at[ib_vmem.at[0]])

        pltpu.emit_pipeline(
            body,
            grid=(m // SC_WINDOW,),
            in_specs=[pl.BlockSpec((SC_WINDOW, w), lambda i: (i, 0)),
                      pl.BlockSpec((1, SC_WINDOW), lambda i: (0, i)),
                      pl.BlockSpec((1, SC_WINDOW), lambda i: (0, i))],
            out_specs=[],
            core_axis_name=("c", "s"),
            dimension_semantics=(pltpu.PARALLEL,),
        )(x_hbm, ia_hbm, ib_hbm)

    return scatter(src, idx_a, idx_b)


def _sc_gather_rows(table, idx):
    m = idx.shape[1]
    w = table.shape[1]

    @functools.partial(pl.kernel, out_type=jax.ShapeDtypeStruct((m, w), table.dtype),
                       mesh=_sc_mesh(), scratch_types=[], name="moe_sc_gather")
    def gather(t_hbm, i_hbm, o_hbm):
        def body(i_vmem, o_vmem):
            pltpu.sync_copy(t_hbm.at[i_vmem.at[0]], o_vmem)

        pltpu.emit_pipeline(
            body,
            grid=(m // SC_WINDOW,),
            in_specs=[pl.BlockSpec((1, SC_WINDOW), lambda i: (0, i))],
            out_specs=[pl.BlockSpec((SC_WINDOW, w), lambda i: (i, 0))],
            core_axis_name=("c", "s"),
            dimension_semantics=(pltpu.PARALLEL,),
        )(i_hbm, o_hbm)

    return gather(table, idx)


def _expert_kernel(be_ref, nv_ref, xa_ref, xb_ref, w1_ref, w3_ref, w2_ref, ya_ref, yb_ref,
                   w1_scr, w3_scr, w2_scr):
    i = pl.program_id(0)
    blk = xa_ref.shape[0]
    n_valid = nv_ref[i]
    new_expert = (i == 0) | (be_ref[i] != be_ref[jnp.maximum(i - 1, 0)])

    @pl.when((n_valid > 0) & new_expert)
    def _():
        w1_scr[...] = w1_ref[0].astype(BF16)
        w3_scr[...] = w3_ref[0].astype(BF16)
        w2_scr[...] = w2_ref[0].astype(BF16)

    @pl.when(n_valid > 0)
    def _():
        valid = lax.broadcasted_iota(I32, (blk, 1), 0) < n_valid
        xq = _unpack_row_parts(xa_ref[...], xb_ref[...])
        xin = jnp.concatenate([jnp.where(valid, q, 0.0).astype(BF16) for q in xq], axis=1)
        a = jnp.dot(xin, w1_scr[...], preferred_element_type=F32)
        b = jnp.dot(xin, w3_scr[...], preferred_element_type=F32)
        hid = (_silu(a) * b).astype(BF16)
        y = jnp.dot(hid, w2_scr[...], preferred_element_type=F32)
        ya_ref[...], yb_ref[...] = _pack_row_parts(y)

    @pl.when(n_valid <= 0)
    def _():
        ya_ref[...] = jnp.zeros_like(ya_ref)
        yb_ref[...] = jnp.zeros_like(yb_ref)


def _combine_kernel(alpha, x_ref, g_ref, mf_ref, lng_ref, lnb_ref, y1a_ref, y1b_ref, y2a_ref, y2b_ref, o_ref):
    d = x_ref.shape[2]
    q = d // 4
    w1 = mf_ref[:, 0:1]
    w2 = mf_ref[:, 1:2]
    y1 = _unpack_row_parts(y1a_ref[...], y1b_ref[...])
    y2 = _unpack_row_parts(y2a_ref[...], y2b_ref[...])
    r = [alpha * x_ref[0, :, p * q:(p + 1) * q] + g_ref[0, :, p * q:(p + 1) * q] * (w1 * y1[p] + w2 * y2[p])
         for p in range(4)]
    mu = sum(jnp.sum(rp, axis=-1, keepdims=True) for rp in r) / d
    dev = [rp - mu for rp in r]
    var = sum(jnp.sum(dp * dp, axis=-1, keepdims=True) for dp in dev) / d
    inv = lax.rsqrt(var + NORM_EPS)
    for p in range(4):
        sl = slice(p * q, (p + 1) * q)
        o_ref[0, :, sl] = dev[p] * inv * lng_ref[:, sl] + lnb_ref[:, sl]


def _moe_layer(x, sh, sc, g, wg, bg, we, be, w1, w3, w2, layer, ln_g, ln_b, alpha):
    bsz, s, d = x.shape
    n = bsz * s
    quarter = d // 4
    e_num, blk = MOE_EXPERTS, MOE_BLOCK
    ff = w1.shape[-1]
    ts = min(512, s)
    nt_b = s // ts
    nt = n // ts
    n_rows = n * 2 + e_num * blk
    nb = n_rows // blk

    wcat = jnp.zeros((d, LANES), F32).at[:, :MOE_GROUPS].set(wg).at[:, MOE_GROUPS:MOE_GROUPS + e_num].set(we)
    bcat = jnp.zeros((1, LANES), F32).at[0, :MOE_GROUPS].set(bg).at[0, MOE_GROUPS:MOE_GROUPS + e_num].set(be)
    wcat_hi = wcat.astype(BF16)
    wcat_hl = jnp.concatenate([wcat_hi, (wcat - wcat_hi.astype(F32)).astype(BF16)], axis=1)
    tri = jnp.tri(ts, k=-1, dtype=BF16)

    vec = pl.BlockSpec((1, 1, d), lambda b, i: (b, 0, 0))
    tile = pl.BlockSpec((1, ts, d), lambda b, i: (b, i, 0))
    flat = lambda w: pl.BlockSpec((ts, w), lambda b, i: (b * nt_b + i, 0))
    hpa, hpb, mt, mf, cnt = pl.pallas_call(
        _router_kernel,
        grid=(bsz, nt_b),
        in_specs=[tile, vec, vec, _const_spec((d, 2 * LANES)), _const_spec((1, LANES)), _const_spec((ts, ts))],
        out_specs=[flat(quarter), flat(quarter), pl.BlockSpec((8, ts), lambda b, i: (0, b * nt_b + i)),
                   flat(LANES), _const_spec((1, LANES))],
        out_shape=[jax.ShapeDtypeStruct((n, quarter), I32), jax.ShapeDtypeStruct((n, quarter), I32),
                   jax.ShapeDtypeStruct((8, n), I32), jax.ShapeDtypeStruct((n, LANES), F32),
                   jax.ShapeDtypeStruct((1, LANES), F32)],
        scratch_shapes=[pltpu.VMEM((1, LANES), F32)],
        compiler_params=_cparams(("arbitrary", "arbitrary")),
        name="moe_router",
    )(x, sh, sc, wcat_hl, bcat, tri)

    counts = cnt[0, MOE_GROUPS:MOE_GROUPS + e_num].astype(I32)
    padded = ((counts + blk - 1) // blk) * blk
    pad_end = jnp.cumsum(padded)
    pad_start = pad_end - padded
    blk_start = jnp.arange(nb, dtype=I32) * blk
    blk_expert = jnp.minimum(jnp.sum((pad_end[None, :] <= blk_start[:, None]).astype(I32), axis=1), e_num - 1)
    blk_valid = jnp.clip(pad_start[blk_expert] + counts[blk_expert] - blk_start, 0, blk).astype(I32)

    td = min(8192, n)
    dest1, dest2 = pl.pallas_call(
        _dest_kernel,
        grid_spec=pltpu.PrefetchScalarGridSpec(
            num_scalar_prefetch=1,
            grid=(n // td,),
            in_specs=[pl.BlockSpec((8, td), lambda i, ps_r: (0, i))],
            out_specs=[pl.BlockSpec((1, td), lambda i, ps_r: (0, i))] * 2,
        ),
        out_shape=[jax.ShapeDtypeStruct((1, n), I32)] * 2,
        compiler_params=_cparams(("arbitrary",)),
        name="moe_dest",
    )(pad_start.astype(I32), mt)

    xa = _sc_scatter_rows(hpa, dest1, dest2, n_rows)
    xb = _sc_scatter_rows(hpb, dest1, dest2, n_rows)

    rows_spec = pl.BlockSpec((blk, quarter), lambda i, be_r, nv_r: (i, 0))
    ya, yb = pl.pallas_call(
        _expert_kernel,
        grid_spec=pltpu.PrefetchScalarGridSpec(
            num_scalar_prefetch=2,
            grid=(nb,),
            in_specs=[rows_spec, rows_spec,
                      pl.BlockSpec((1, d, ff), lambda i, be_r, nv_r: (layer * e_num + be_r[i], 0, 0)),
                      pl.BlockSpec((1, d, ff), lambda i, be_r, nv_r: (layer * e_num + be_r[i], 0, 0)),
                      pl.BlockSpec((1, ff, d), lambda i, be_r, nv_r: (layer * e_num + be_r[i], 0, 0))],
            out_specs=[rows_spec, rows_spec],
            scratch_shapes=[pltpu.VMEM((d, ff), BF16), pltpu.VMEM((d, ff), BF16), pltpu.VMEM((ff, d), BF16)],
        ),
        out_shape=[jax.ShapeDtypeStruct((n_rows, quarter), I32)] * 2,
        compiler_params=_cparams(("arbitrary",)),
        name="moe_experts",
    )(blk_expert, blk_valid, xa, xb, w1, w3, w2)

    dest12 = jnp.concatenate([dest1, dest2], axis=1)
    ga = _sc_gather_rows(ya, dest12)
    gb = _sc_gather_rows(yb, dest12)

    first = pl.BlockSpec((ts, quarter), lambda b, i: (b * nt_b + i, 0))
    second = pl.BlockSpec((ts, quarter), lambda b, i: (nt + b * nt_b + i, 0))
    return pl.pallas_call(
        functools.partial(_combine_kernel, alpha),
        grid=(bsz, nt_b),
        in_specs=[tile, vec, flat(LANES), _const_spec((1, d)), _const_spec((1, d)),
                  first, first, second, second],
        out_specs=tile,
        out_shape=jax.ShapeDtypeStruct((bsz, s, d), F32),
        compiler_params=_cparams(("arbitrary", "arbitrary")),
        name="moe_combine",
    )(x, g, mf, ln_g.reshape(1, d), ln_b.reshape(1, d), ga, gb, ga, gb)


def kernel(x, c, ada_w, ada_b, ln_g, ln_b, hgrn_w_in, hgrn_w_out, hgrn_lb, hgrn_norm_w, attn_w_in, attn_w_out,
           attn_lambda, attn_subln_w, router_g_w, router_g_b, router_e_w, router_e_b, moe_w1, moe_w3, moe_w2):
    depth = ada_w.shape[0]
    bsz, s, d = x.shape
    alpha = (2 * depth) ** 0.25
    lb_all = jnp.cumsum(jax.nn.softmax(hgrn_lb.astype(F32), axis=0), axis=0)
    lb_all = lb_all - lb_all[0:1]
    mod = _ada_mod(c, ada_w, ada_b).reshape(depth, bsz, 6, 1, d)
    w1_all = moe_w1.reshape((-1,) + moe_w1.shape[2:])
    w3_all = moe_w3.reshape((-1,) + moe_w3.shape[2:])
    w2_all = moe_w2.reshape((-1,) + moe_w2.shape[2:])
    for i in range(depth):
        sh1, sc1, g1, sh2, sc2, g2 = (mod[i, :, m] for m in range(6))
        j = i // 2
        if i % 2 == 0:
            x = _hgrn_layer(x, sh1, sc1, g1, hgrn_w_in[j], hgrn_w_out[j], lb_all[j], hgrn_norm_w[j],
                            ln_g[i, 0], ln_b[i, 0], alpha)
        else:
            lambda_init = 0.8 - 0.6 * math.exp(-0.3 * i)
            x = _attn_layer(x, sh1, sc1, g1, attn_w_in[j], attn_w_out[j], attn_lambda[j], attn_subln_w[j],
                            lambda_init, ln_g[i, 0], ln_b[i, 0], alpha)
        x = _moe_layer(x, sh2, sc2, g2, router_g_w[i], router_g_b[i], router_e_w[i], router_e_b[i],
                       w1_all, w3_all, w2_all, i, ln_g[i, 1], ln_b[i, 1], alpha)
    return x
```

```python
import functools
import math

import jax
import jax.numpy as jnp
from jax import lax
from jax.experimental import pallas as pl
from jax.experimental.pallas import tpu as pltpu
from jax.experimental.pallas import tpu_sc as plsc

F32 = jnp.float32
BF16 = jnp.bfloat16
I32 = jnp.int32
U32 = jnp.uint32
HIGHEST = lax.Precision.HIGHEST

LANES = 128
HEAD = 128
HGRN_CHUNK = 32
HGRN_ROWS = 2
HGRN_GROUP = 2
HGRN_SAFE_SPAN = 80.0
ATTN_D = 64
FLASH_HEADS = 8
FLASH_ROW_CHUNK = 32
ROPE_THETA = 10000.0
MOE_GROUPS = 4
MOE_EPG = 8
MOE_EXPERTS = MOE_GROUPS * MOE_EPG
MOE_BLOCK = 512
SC_WINDOW = 128
NORM_EPS = 1e-5
VMEM_LIMIT = 56 * 1024 * 1024

NT_DIMS = (((1,), (1,)), ((), ()))
TN_DIMS = (((0,), (0,)), ((), ()))


def _cparams(sem):
    return pltpu.CompilerParams(dimension_semantics=sem, vmem_limit_bytes=VMEM_LIMIT)


def _const_spec(shape):
    nd = len(shape)
    return pl.BlockSpec(shape, lambda *_: (0,) * nd)


def _layer_norm(r, g, b):
    mu = jnp.mean(r, axis=-1, keepdims=True)
    d = r - mu
    var = jnp.mean(d * d, axis=-1, keepdims=True)
    return d * lax.rsqrt(var + NORM_EPS) * g + b


def _silu(x):
    return x * (1.0 / (1.0 + jnp.exp(-x)))


def _pack_bf16_pair(lo, hi):
    lo_b = lax.bitcast_convert_type(lo.astype(BF16).astype(F32), U32)
    hi_b = lax.bitcast_convert_type(hi.astype(BF16).astype(F32), U32)
    return (hi_b & jnp.uint32(0xFFFF0000)) | (lo_b >> 16)


def _unpack_bf16_pair(u):
    lo = lax.bitcast_convert_type(u << 16, F32)
    hi = lax.bitcast_convert_type(u & jnp.uint32(0xFFFF0000), F32)
    return lo, hi


def _ada_kernel(c_ref, w_ref, b_ref, o_ref):
    c = c_ref[...]
    o_ref[0] = jnp.dot(_silu(c), w_ref[0], precision=HIGHEST, preferred_element_type=F32) + b_ref[0]


def _ada_mod(c, ada_w, ada_b):
    depth, d, n6 = ada_w.shape
    bsz = c.shape[0]
    tn = d
    return pl.pallas_call(
        _ada_kernel,
        grid=(depth, n6 // tn),
        in_specs=[
            pl.BlockSpec((bsz, d), lambda i, j: (0, 0)),
            pl.BlockSpec((1, d, tn), lambda i, j: (i, 0, j)),
            pl.BlockSpec((1, 1, tn), lambda i, j: (i, 0, j)),
        ],
        out_specs=pl.BlockSpec((1, bsz, tn), lambda i, j: (i, 0, j)),
        out_shape=jax.ShapeDtypeStruct((depth, bsz, n6), F32),
        compiler_params=_cparams(("arbitrary", "arbitrary")),
        name="ada_mod",
    )(c, ada_w, ada_b.reshape(depth, 1, n6))


def _hgrn_kernel(alpha, x_ref, sh_ref, sc_ref, g_ref, win_ref, wout_ref, loglb_ref, oml_ref,
                 nw_ref, lng_ref, lnb_ref, o_ref,
                 proj_ref, st_ref, ocat_ref, b_ref, kk_ref, qi_ref, ks_ref, vb_ref, dec_ref):
    nrow, ts, d = x_ref.shape
    nh = d // HEAD
    c = HGRN_CHUNK
    nc = ts // c
    gw = min(HGRN_GROUP * HEAD, d)
    ng = d // gw
    hpg = gw // HEAD

    @pl.when(pl.program_id(1) == 0)
    def _():
        st_ref[...] = jnp.zeros_like(st_ref)

    hb = [(x_ref[r] * (1.0 + sc_ref[r]) + sh_ref[r]).astype(BF16) for r in range(nrow)]
    pos = lax.broadcasted_iota(I32, (ts, 1), 0) % c
    row = lax.broadcasted_iota(I32, (ts, ts), 0)
    col = lax.broadcasted_iota(I32, (ts, ts), 1)
    keep = (row >= col) & (row // c == col // c)
    span = jnp.float32(0.0)

    def project(r, g):
        ps = slice(g * 4 * gw, (g + 1) * 4 * gw)
        proj_ref[r, :, ps] = jnp.dot(hb[r], win_ref[:, ps], preferred_element_type=F32)

    for r in range(nrow):
        project(r, 0)
    for g, r in [(g, r) for g in range(ng) for r in range(nrow)]:
        if g + 1 < ng:
            project(r, g + 1)
        cs = slice(g * gw, (g + 1) * gw)
        p0 = g * 4 * gw
        q2 = proj_ref[r, :, p0:p0 + gw]
        z = proj_ref[r, :, p0 + gw:p0 + 2 * gw]

        ls = jnp.minimum(z, 0.0) - jnp.log(1.0 + jnp.exp(-jnp.abs(z)))
        lsn = ls - z
        cc = loglb_ref[:, cs] + lsn
        log_f = jnp.maximum(ls, cc) + jnp.log(1.0 + jnp.exp(-jnp.abs(ls - cc)))
        kk = oml_ref[:, cs] * jnp.exp(lsn)

        b = log_f
        step = 1
        while step < c:
            b = b + jnp.where(pos >= step, pltpu.roll(b, step, 0), 0.0)
            step *= 2
        b_ref[r, :, cs] = b
        kk_ref[r, :, cs] = kk

        b3 = b.reshape(nc, c, gw)
        b_last = b3[:, c - 1:c, :]
        b_mid = b3[:, c // 2 - 1:c // 2, :]
        q3 = q2.reshape(nc, c, gw)
        k3 = kk.reshape(nc, c, gw)
        qi_ref[r, :, cs] = (q3 * jnp.exp(b3)).astype(BF16).reshape(ts, gw)
        ks_ref[r, :, cs] = (k3 * jnp.exp(b_last - b3)).astype(BF16).reshape(ts, gw)
        q_intra = (q3 * jnp.exp(b3 - b_mid)).astype(BF16).reshape(ts, gw)
        k_intra = (k3 * jnp.exp(b_mid - b3)).astype(BF16).reshape(ts, gw)
        dec_ref[r, :, cs] = jnp.exp(b_last).reshape(nc, gw)
        span = jnp.maximum(span, jnp.max(-b_last))
        v2 = proj_ref[r, :, p0 + 2 * gw:p0 + 3 * gw].astype(BF16)
        vb_ref[r, :, cs] = v2

        for hh in range(hpg):
            sl = slice(hh * HEAD, (hh + 1) * HEAD)
            sc = lax.dot_general(q_intra[:, sl], k_intra[:, sl], NT_DIMS, preferred_element_type=F32)
            p = jnp.where(keep, sc, 0.0).astype(BF16)
            ocat_ref[r, :, g * gw + hh * HEAD:g * gw + (hh + 1) * HEAD] = jnp.dot(
                p, v2[:, sl], preferred_element_type=F32)

    @pl.when(span > HGRN_SAFE_SPAN)
    def _():
        tpos = lax.broadcasted_iota(I32, (ts, 1), 0)
        for r in range(nrow):
            b = b_ref[r]
            kk = kk_ref[r]
            q2 = jnp.concatenate([proj_ref[r, :, g * 4 * gw:g * 4 * gw + gw] for g in range(ng)], axis=1)
            diag = q2 * kk
            scores = [jnp.where(row == col, jnp.sum(diag[:, hd * HEAD:(hd + 1) * HEAD], axis=-1, keepdims=True), 0.0)
                      for hd in range(nh)]
            size = 2
            while size <= c:
                ref_row = (row // size) * size + (size // 2 - 1)
                pick = jnp.where(col == ref_row, 1.0, 0.0)
                b_at = jnp.dot(pick, b, precision=HIGHEST, preferred_element_type=F32)
                right = (tpos % size) >= (size // 2)
                qa = jnp.where(right, q2 * jnp.exp(b - b_at), 0.0).astype(BF16)
                ka = jnp.where(right, 0.0, kk * jnp.exp(b_at - b)).astype(BF16)
                same = (row // size) == (col // size)
                for hd in range(nh):
                    sl = slice(hd * HEAD, (hd + 1) * HEAD)
                    sc = lax.dot_general(qa[:, sl], ka[:, sl], NT_DIMS, preferred_element_type=F32)
                    scores[hd] = scores[hd] + jnp.where(same, sc, 0.0)
                size *= 2
            for hd in range(nh):
                sl = slice(hd * HEAD, (hd + 1) * HEAD)
                ocat_ref[r, :, sl] = jnp.dot(scores[hd].astype(BF16), vb_ref[r, :, sl],
                                             preferred_element_type=F32)

    for ci in range(nc):
        rs = slice(ci * c, (ci + 1) * c)
        for r, hd in [(r, hd) for hd in range(nh) for r in range(nrow)]:
            sl = slice(hd * HEAD, (hd + 1) * HEAD)
            st = st_ref[r, hd]
            o_inter = lax.dot_general(qi_ref[r, rs, sl], st.astype(BF16), NT_DIMS,
                                      preferred_element_type=F32)
            ocat_ref[r, rs, sl] += o_inter
            upd = lax.dot_general(vb_ref[r, rs, sl], ks_ref[r, rs, sl], TN_DIMS,
                                  preferred_element_type=F32)
            st_ref[r, hd] = st * dec_ref[r, ci:ci + 1, sl] + upd

    for r in range(nrow):
        for hd in range(nh):
            sl = slice(hd * HEAD, (hd + 1) * HEAD)
            g, hh = divmod(hd, hpg)
            oh = ocat_ref[r, :, sl]
            ms = jnp.mean(oh * oh, axis=-1, keepdims=True)
            gate = proj_ref[r, :, g * 4 * gw + 3 * gw + hh * HEAD:g * 4 * gw + 3 * gw + (hh + 1) * HEAD]
            ocat_ref[r, :, sl] = oh * lax.rsqrt(ms + NORM_EPS) * nw_ref[...] * _silu(gate)
        y = jnp.dot(ocat_ref[r].astype(BF16), wout_ref[...], preferred_element_type=F32)
        res = alpha * x_ref[r] + g_ref[r] * y
        o_ref[r] = _layer_norm(res, lng_ref[...], lnb_ref[...])


def _hgrn_layer(x, sh, sc, g, w_in, w_out, lb, norm_w, ln_g, ln_b, alpha):
    bsz, s, d = x.shape
    ts = min(256, s)
    nh = d // HEAD
    gw = min(HGRN_GROUP * HEAD, d)
    ng = d // gw
    w_grouped = w_in.reshape(d, 4, ng, gw).transpose(0, 2, 1, 3).reshape(d, 4 * d).astype(BF16)
    nr = math.gcd(HGRN_ROWS, bsz)
    vec = pl.BlockSpec((nr, 1, d), lambda b, i: (b, 0, 0))
    tile = pl.BlockSpec((nr, ts, d), lambda b, i: (b, i, 0))
    return pl.pallas_call(
        functools.partial(_hgrn_kernel, alpha),
        grid=(bsz // nr, s // ts),
        in_specs=[tile, vec, vec, vec,
                  _const_spec((d, 4 * d)), _const_spec((d, d)),
                  _const_spec((1, d)), _const_spec((1, d)), _const_spec((1, HEAD)),
                  _const_spec((1, d)), _const_spec((1, d))],
        out_specs=tile,
        out_shape=jax.ShapeDtypeStruct((bsz, s, d), F32),
        scratch_shapes=[pltpu.VMEM((nr, ts, 4 * d), F32),
                        pltpu.VMEM((nr, nh, HEAD, HEAD), F32),
                        pltpu.VMEM((nr, ts, d), F32),
                        pltpu.VMEM((nr, ts, d), F32), pltpu.VMEM((nr, ts, d), F32),
                        pltpu.VMEM((nr, ts, d), BF16), pltpu.VMEM((nr, ts, d), BF16),
                        pltpu.VMEM((nr, ts, d), BF16),
                        pltpu.VMEM((nr, ts // HGRN_CHUNK, d), F32)],
        compiler_params=_cparams(("arbitrary", "arbitrary")),
        name="hgrn_layer",
    )(x, sh, sc, g, w_grouped, w_out.astype(BF16),
      jnp.log(lb).reshape(1, d), (1.0 - lb).reshape(1, d), norm_w.reshape(1, HEAD),
      ln_g.reshape(1, d), ln_b.reshape(1, d))


def _attn_in_kernel(x_ref, sh_ref, sc_ref, w_ref, cos_ref, sin_ref, q_ref, k_ref, v_ref):
    d = x_ref.shape[2]
    x = x_ref[0]
    h = x * (1.0 + sc_ref[0]) + sh_ref[0]
    qkv = jnp.dot(h.astype(BF16), w_ref[...], preferred_element_type=F32)
    cos = cos_ref[...]
    sin = sin_ref[...]
    lane = lax.broadcasted_iota(I32, (1, LANES), 1)
    first_half = (lane % ATTN_D) < (ATTN_D // 2)
    scale = ATTN_D ** -0.5 * math.log2(math.e)
    for j in range(d // LANES):
        sl = slice(j * LANES, (j + 1) * LANES)
        for src, dst, mul in ((0, q_ref, scale), (d, k_ref, 1.0)):
            t = qkv[:, src + j * LANES:src + (j + 1) * LANES]
            partner = jnp.where(first_half, pltpu.roll(t, LANES - ATTN_D // 2, 1),
                                pltpu.roll(t, ATTN_D // 2, 1))
            dst[0, :, sl] = ((t * cos + partner * sin) * mul).astype(BF16)
    v_ref[0] = qkv[:, 2 * d:].astype(BF16)


def _flash_kernel(out_scale, lam_ref, q_ref, k_ref, v_ref, w_ref, o_ref, *scratch):
    tq = q_ref.shape[1]
    rows = 2 * tq
    qi = pl.program_id(2)
    nrep = tq // LANES
    rc = FLASH_ROW_CHUNK
    n_heads = q_ref.shape[2] // HEAD
    per_head = len(scratch) // n_heads
    heads = [(hd,) + tuple(scratch[hd * per_head:(hd + 1) * per_head]) for hd in range(n_heads)]
    lane = lax.broadcasted_iota(I32, (1, LANES), 1)

    for hd, m_ref, l_ref, acc_ref, _, _, _, qq_ref in heads:
        q = q_ref[0, :, hd * HEAD:(hd + 1) * HEAD]
        zero = jnp.zeros_like(q)
        qq_ref[0:tq, :] = jnp.where(lane < ATTN_D, q, zero)
        qq_ref[tq:rows, :] = jnp.where(lane >= ATTN_D, q, zero)
        m_ref[...] = jnp.full_like(m_ref, -jnp.inf)
        l_ref[...] = jnp.zeros_like(l_ref)
        acc_ref[...] = jnp.zeros_like(acc_ref)

    def scores(head, j):
        hd, _, _, _, _, s_ref, _, qq_ref = head
        kb = k_ref[0, pl.ds(pl.multiple_of(j * tq, tq), tq), hd * HEAD:(hd + 1) * HEAD]
        s_ref[...] = lax.dot_general(qq_ref[...], kb, NT_DIMS, preferred_element_type=F32)

    def softmax(head, masked):
        _, m_ref, l_ref, _, a_ref, s_ref, p_ref, _ = head
        for r0 in range(0, rows, rc):
            rs = slice(r0, r0 + rc)
            s = s_ref[rs, :]
            if masked:
                row = lax.broadcasted_iota(I32, (rc, tq), 0) + (r0 % tq)
                col = lax.broadcasted_iota(I32, (rc, tq), 1)
                s = jnp.where(row >= col, s, -jnp.inf)
            m_old = m_ref[rs, :]
            m_new = jnp.maximum(m_old, jnp.max(s, axis=-1, keepdims=True))
            a = jnp.exp2(m_old - m_new)
            p = jnp.exp2(s - jnp.concatenate([m_new] * nrep, axis=1))
            psum = p[:, 0:LANES]
            for r in range(1, nrep):
                psum = psum + p[:, r * LANES:(r + 1) * LANES]
            l_ref[rs, :] = a * l_ref[rs, :] + psum
            m_ref[rs, :] = m_new
            a_ref[rs, :] = a
            p_ref[rs, :] = p.astype(BF16)

    def values(head, j):
        hd, _, _, acc_ref, a_ref, _, p_ref, _ = head
        vb = v_ref[0, pl.ds(pl.multiple_of(j * tq, tq), tq), hd * HEAD:(hd + 1) * HEAD]
        acc_ref[...] = a_ref[...] * acc_ref[...] + jnp.dot(p_ref[...], vb, preferred_element_type=F32)

    nh_step = len(heads)
    n_blk = qi + 1

    def finalize(head):
        hd, _, l_ref, acc_ref, _, _, _, _ = head
        o_all = acc_ref[...] / jnp.sum(l_ref[...], axis=-1, keepdims=True)
        o = o_all[:tq] - lam_ref[0] * o_all[tq:]
        ms = jnp.mean(o * o, axis=-1, keepdims=True)
        o_ref[0, :, hd * HEAD:(hd + 1) * HEAD] = (
            o * lax.rsqrt(ms + NORM_EPS) * w_ref[...] * out_scale).astype(BF16)

    def emit(base, k_lo, k_hi, n_items, masked_from):
        for k in range(k_lo, k_hi + 1):
            if k + 1 < n_items:
                scores(heads[(k + 1) % nh_step], base + (k + 1) // nh_step)
            if 0 <= k < n_items:
                softmax(heads[k % nh_step], k >= masked_from)
            if 0 <= k - 1 < n_items:
                values(heads[(k - 1) % nh_step], base + (k - 1) // nh_step)
                if k - 1 >= masked_from:
                    finalize(heads[(k - 1) % nh_step])

    @pl.when(n_blk == 1)
    def _():
        emit(0, -1, nh_step, nh_step, 0)

    @pl.when(n_blk >= 2)
    def _():
        unbounded = 1 << 30
        emit(0, -1, 0, unbounded, unbounded)

        def body(t, carry):
            emit(t, 1, nh_step, unbounded, unbounded)
            return carry

        lax.fori_loop(0, n_blk - 2, body, 0)
        emit(n_blk - 2, 1, 2 * nh_step, 2 * nh_step, nh_step)


def _resid_ln_kernel(alpha, x_ref, g_ref, o_in_ref, w_ref, lng_ref, lnb_ref, o_ref):
    y = jnp.dot(o_in_ref[0], w_ref[...], preferred_element_type=F32)
    r = alpha * x_ref[0] + g_ref[0] * y
    o_ref[0] = _layer_norm(r, lng_ref[...], lnb_ref[...])


def _attn_layer(x, sh, sc, g, w_in, w_out, lam_params, subln_w, lambda_init, ln_g, ln_b, alpha):
    bsz, s, d = x.shape
    nh = d // HEAD
    ts = min(512, s)
    tq = min(256, s)
    half = ATTN_D // 2
    inv_freq = ROPE_THETA ** (-jnp.arange(half, dtype=F32) / half)
    ang = jnp.arange(s, dtype=F32)[:, None] * inv_freq[None, :]
    cos_t = jnp.tile(jnp.cos(ang), (1, LANES // half))
    sin_h = jnp.sin(ang)
    sin_t = jnp.tile(jnp.concatenate([-sin_h, sin_h], axis=1), (1, LANES // ATTN_D))

    vec = pl.BlockSpec((1, 1, d), lambda b, i: (b, 0, 0))
    tile = pl.BlockSpec((1, ts, d), lambda b, i: (b, i, 0))
    rope = pl.BlockSpec((ts, LANES), lambda b, i: (i, 0))
    q, k, v = pl.pallas_call(
        _attn_in_kernel,
        grid=(bsz, s // ts),
        in_specs=[tile, vec, vec, _const_spec((d, 3 * d)), rope, rope],
        out_specs=[tile, tile, tile],
        out_shape=[jax.ShapeDtypeStruct((bsz, s, d), BF16)] * 3,
        compiler_params=_cparams(("arbitrary", "arbitrary")),
        name="attn_in",
    )(x, sh, sc, w_in.astype(BF16), cos_t, sin_t)

    lp = lam_params.astype(F32)
    lam = (jnp.exp(jnp.sum(lp[0] * lp[1])) - jnp.exp(jnp.sum(lp[2] * lp[3])) + lambda_init).reshape(1)
    heads_per_step = math.gcd(FLASH_HEADS, nh)
    hw = heads_per_step * HEAD
    qspec = pl.BlockSpec((1, tq, hw), lambda b, h, i: (b, i, h))
    kvspec = pl.BlockSpec((1, s, hw), lambda b, h, i: (b, 0, h))
    head_scratch = [pltpu.VMEM((2 * tq, LANES), F32), pltpu.VMEM((2 * tq, LANES), F32),
                    pltpu.VMEM((2 * tq, HEAD), F32), pltpu.VMEM((2 * tq, LANES), F32),
                    pltpu.VMEM((2 * tq, tq), F32), pltpu.VMEM((2 * tq, tq), BF16),
                    pltpu.VMEM((2 * tq, HEAD), BF16)]
    o = pl.pallas_call(
        functools.partial(_flash_kernel, 1.0 - lambda_init),
        grid=(bsz, nh // heads_per_step, s // tq),
        in_specs=[pl.BlockSpec(memory_space=pltpu.SMEM), qspec, kvspec, kvspec,
                  pl.BlockSpec((1, HEAD), lambda b, h, i: (0, 0))],
        out_specs=qspec,
        out_shape=jax.ShapeDtypeStruct((bsz, s, d), BF16),
        scratch_shapes=head_scratch * heads_per_step,
        compiler_params=_cparams(("arbitrary", "arbitrary", "arbitrary")),
        name="diff_flash",
    )(lam, q, k, v, subln_w.reshape(1, HEAD))

    return pl.pallas_call(
        functools.partial(_resid_ln_kernel, alpha),
        grid=(bsz, s // ts),
        in_specs=[tile, vec, tile, _const_spec((d, d)), _const_spec((1, d)), _const_spec((1, d))],
        out_specs=tile,
        out_shape=jax.ShapeDtypeStruct((bsz, s, d), F32),
        compiler_params=_cparams(("arbitrary", "arbitrary")),
        name="attn_out",
    )(x, g, o, w_out.astype(BF16), ln_g.reshape(1, d), ln_b.reshape(1, d))


def _pack_row_parts(v):
    q = v.shape[1] // 4
    return tuple(lax.bitcast_convert_type(_pack_bf16_pair(v[:, p * q:(p + 1) * q], v[:, (2 + p) * q:(3 + p) * q]), I32)
                 for p in range(2))


def _unpack_row_parts(part_a, part_b):
    lo_a, hi_a = _unpack_bf16_pair(lax.bitcast_convert_type(part_a, U32))
    lo_b, hi_b = _unpack_bf16_pair(lax.bitcast_convert_type(part_b, U32))
    return lo_a, lo_b, hi_a, hi_b


def _router_kernel(x_ref, sh_ref, sc_ref, w_ref, bias_ref, tri_ref, hpa_ref, hpb_ref, mt_ref, mf_ref, cnt_ref,
                   cnt_scr):
    ts, d = x_ref.shape[1], x_ref.shape[2]
    first = (pl.program_id(0) == 0) & (pl.program_id(1) == 0)

    @pl.when(first)
    def _():
        cnt_scr[...] = jnp.zeros_like(cnt_scr)

    x = x_ref[0]
    h = x * (1.0 + sc_ref[0]) + sh_ref[0]
    hpa_ref[...], hpb_ref[...] = _pack_row_parts(h)

    h_hi = h.astype(BF16)
    h_lo = (h - h_hi.astype(F32)).astype(BF16)
    hh = jnp.dot(h_hi, w_ref[...], preferred_element_type=F32)
    lh = jnp.dot(h_lo, w_ref[:, 0:LANES], preferred_element_type=F32)
    logits = hh[:, 0:LANES] + hh[:, LANES:2 * LANES] + lh + bias_ref[...]
    lane = lax.broadcasted_iota(I32, (ts, LANES), 1)
    neg = -jnp.inf
    big = jnp.int32(LANES)
    is_g = lane < MOE_GROUPS
    gl = jnp.where(is_g, logits, neg)
    gmax = jnp.max(gl, axis=-1, keepdims=True)
    g_idx = jnp.min(jnp.where(gl == gmax, lane, big), axis=-1, keepdims=True)
    g_w = 1.0 / jnp.sum(jnp.exp(gl - gmax), axis=-1, keepdims=True)

    e_lane = lane - MOE_GROUPS
    in_grp = (e_lane >= g_idx * MOE_EPG) & (e_lane < (g_idx + 1) * MOE_EPG)
    el = jnp.where(in_grp, logits, neg)
    l1 = jnp.max(el, axis=-1, keepdims=True)
    i1 = jnp.min(jnp.where(el == l1, lane, big), axis=-1, keepdims=True)
    el2 = jnp.where(lane == i1, neg, el)
    l2 = jnp.max(el2, axis=-1, keepdims=True)
    i2 = jnp.min(jnp.where(el2 == l2, lane, big), axis=-1, keepdims=True)
    t = jnp.exp(l2 - l1)
    w1 = g_w / (1.0 + t)
    w2 = g_w * t / (1.0 + t)

    oh1 = (lane == i1)
    oh2 = (lane == i2)
    both = (oh1 | oh2).astype(BF16)
    before = jnp.dot(tri_ref[...], both, preferred_element_type=F32) + cnt_scr[...]
    r1 = jnp.sum(jnp.where(oh1, before, 0.0), axis=-1, keepdims=True)
    r2 = jnp.sum(jnp.where(oh2, before, 0.0), axis=-1, keepdims=True)
    cnt_scr[...] = cnt_scr[...] + jnp.sum(both.astype(F32), axis=0, keepdims=True)
    cnt_ref[...] = cnt_scr[...]

    e1 = i1 - MOE_GROUPS
    e2 = i2 - MOE_GROUPS
    meta = jnp.where(lane == 0, e1, jnp.where(lane == 1, e2, jnp.where(
        lane == 2, r1.astype(I32), jnp.where(lane == 3, r2.astype(I32), 0))))
    mt_ref[...] = jnp.transpose(meta)[0:8, :]
    mf_ref[...] = jnp.where(lane == 0, w1, jnp.where(lane == 1, w2, 0.0))


def _dest_kernel(ps_ref, mt_ref, d1_ref, d2_ref):
    e1 = mt_ref[0:1, :]
    e2 = mt_ref[1:2, :]
    p1 = jnp.zeros_like(e1)
    p2 = jnp.zeros_like(e2)
    for e in range(MOE_EXPERTS):
        p1 = jnp.where(e1 == e, ps_ref[e], p1)
        p2 = jnp.where(e2 == e, ps_ref[e], p2)
    d1_ref[...] = p1 + mt_ref[2:3, :]
    d2_ref[...] = p2 + mt_ref[3:4, :]


def _sc_mesh():
    return plsc.VectorSubcoreMesh(core_axis_name="c", subcore_axis_name="s")


def _sc_scatter_rows(src, idx_a, idx_b, n_rows):
    m, w = src.shape

    @functools.partial(pl.kernel, out_type=jax.ShapeDtypeStruct((n_rows, w), src.dtype),
                       mesh=_sc_mesh(), scratch_types=[], name="moe_sc_scatter")
    def scatter(x_hbm, ia_hbm, ib_hbm, o_hbm):
        def body(x_vmem, ia_vmem, ib_vmem):
            pltpu.sync_copy(x_vmem, o_hbm.at[ia_vmem.at[0]])
            pltpu.sync_copy(x_vmem, o_hbm.at[ib_vmem.at[0]])

        pltpu.emit_pipeline(
            body,
            grid=(m // SC_WINDOW,),
            in_specs=[pl.BlockSpec((SC_WINDOW, w), lambda i: (i, 0)),
                      pl.BlockSpec((1, SC_WINDOW), lambda i: (0, i)),
                      pl.BlockSpec((1, SC_WINDOW), lambda i: (0, i))],
            out_specs=[],
            core_axis_name=("c", "s"),
            dimension_semantics=(pltpu.PARALLEL,),
        )(x_hbm, ia_hbm, ib_hbm)

    return scatter(src, idx_a, idx_b)


def _sc_gather_rows(table, idx):
    m = idx.shape[1]
    w = table.shape[1]

    @functools.partial(pl.kernel, out_type=jax.ShapeDtypeStruct((m, w), table.dtype),
                       mesh=_sc_mesh(), scratch_types=[], name="moe_sc_gather")
    def gather(t_hbm, i_hbm, o_hbm):
        def body(i_vmem, o_vmem):
            pltpu.sync_copy(t_hbm.at[i_vmem.at[0]], o_vmem)

        pltpu.emit_pipeline(
            body,
            grid=(m // SC_WINDOW,),
            in_specs=[pl.BlockSpec((1, SC_WINDOW), lambda i: (0, i))],
            out_specs=[pl.BlockSpec((SC_WINDOW, w), lambda i: (i, 0))],
            core_axis_name=("c", "s"),
            dimension_semantics=(pltpu.PARALLEL,),
        )(i_hbm, o_hbm)

    return gather(table, idx)


def _expert_kernel(be_ref, nv_ref, xa_ref, xb_ref, w1_ref, w3_ref, w2_ref, ya_ref, yb_ref,
                   w1_scr, w3_scr, w2_scr):
    i = pl.program_id(0)
    blk = xa_ref.shape[0]
    n_valid = nv_ref[i]
    new_expert = (i == 0) | (be_ref[i] != be_ref[jnp.maximum(i - 1, 0)])

    @pl.when((n_valid > 0) & new_expert)
    def _():
        w1_scr[...] = w1_ref[0].astype(BF16)
        w3_scr[...] = w3_ref[0].astype(BF16)
        w2_scr[...] = w2_ref[0].astype(BF16)

    @pl.when(n_valid > 0)
    def _():
        valid = lax.broadcasted_iota(I32, (blk, 1), 0) < n_valid
        xq = _unpack_row_parts(xa_ref[...], xb_ref[...])
        xin = jnp.concatenate([jnp.where(valid, q, 0.0).astype(BF16) for q in xq], axis=1)
        a = jnp.dot(xin, w1_scr[...], preferred_element_type=F32)
        b = jnp.dot(xin, w3_scr[...], preferred_element_type=F32)
        hid = (_silu(a) * b).astype(BF16)
        y = jnp.dot(hid, w2_scr[...], preferred_element_type=F32)
        ya_ref[...], yb_ref[...] = _pack_row_parts(y)

    @pl.when(n_valid <= 0)
    def _():
        ya_ref[...] = jnp.zeros_like(ya_ref)
        yb_ref[...] = jnp.zeros_like(yb_ref)


def _combine_kernel(alpha, x_ref, g_ref, mf_ref, lng_ref, lnb_ref, y1a_ref, y1b_ref, y2a_ref, y2b_ref, o_ref):
    d = x_ref.shape[2]
    q = d // 4
    w1 = mf_ref[:, 0:1]
    w2 = mf_ref[:, 1:2]
    y1 = _unpack_row_parts(y1a_ref[...], y1b_ref[...])
    y2 = _unpack_row_parts(y2a_ref[...], y2b_ref[...])
    r = [alpha * x_ref[0, :, p * q:(p + 1) * q] + g_ref[0, :, p * q:(p + 1) * q] * (w1 * y1[p] + w2 * y2[p])
         for p in range(4)]
    mu = sum(jnp.sum(rp, axis=-1, keepdims=True) for rp in r) / d
    dev = [rp - mu for rp in r]
    var = sum(jnp.sum(dp * dp, axis=-1, keepdims=True) for dp in dev) / d
    inv = lax.rsqrt(var + NORM_EPS)
    for p in range(4):
        sl = slice(p * q, (p + 1) * q)
        o_ref[0, :, sl] = dev[p] * inv * lng_ref[:, sl] + lnb_ref[:, sl]


def _moe_layer(x, sh, sc, g, wg, bg, we, be, w1, w3, w2, layer, ln_g, ln_b, alpha):
    bsz, s, d = x.shape
    n = bsz * s
    quarter = d // 4
    e_num, blk = MOE_EXPERTS, MOE_BLOCK
    ff = w1.shape[-1]
    ts = min(512, s)
    nt_b = s // ts
    nt = n // ts
    n_rows = n * 2 + e_num * blk
    nb = n_rows // blk

    wcat = jnp.zeros((d, LANES), F32).at[:, :MOE_GROUPS].set(wg).at[:, MOE_GROUPS:MOE_GROUPS + e_num].set(we)
    bcat = jnp.zeros((1, LANES), F32).at[0, :MOE_GROUPS].set(bg).at[0, MOE_GROUPS:MOE_GROUPS + e_num].set(be)
    wcat_hi = wcat.astype(BF16)
    wcat_hl = jnp.concatenate([wcat_hi, (wcat - wcat_hi.astype(F32)).astype(BF16)], axis=1)
    tri = jnp.tri(ts, k=-1, dtype=BF16)

    vec = pl.BlockSpec((1, 1, d), lambda b, i: (b, 0, 0))
    tile = pl.BlockSpec((1, ts, d), lambda b, i: (b, i, 0))
    flat = lambda w: pl.BlockSpec((ts, w), lambda b, i: (b * nt_b + i, 0))
    hpa, hpb, mt, mf, cnt = pl.pallas_call(
        _router_kernel,
        grid=(bsz, nt_b),
        in_specs=[tile, vec, vec, _const_spec((d, 2 * LANES)), _const_spec((1, LANES)), _const_spec((ts, ts))],
        out_specs=[flat(quarter), flat(quarter), pl.BlockSpec((8, ts), lambda b, i: (0, b * nt_b + i)),
                   flat(LANES), _const_spec((1, LANES))],
        out_shape=[jax.ShapeDtypeStruct((n, quarter), I32), jax.ShapeDtypeStruct((n, quarter), I32),
                   jax.ShapeDtypeStruct((8, n), I32), jax.ShapeDtypeStruct((n, LANES), F32),
                   jax.ShapeDtypeStruct((1, LANES), F32)],
        scratch_shapes=[pltpu.VMEM((1, LANES), F32)],
        compiler_params=_cparams(("arbitrary", "arbitrary")),
        name="moe_router",
    )(x, sh, sc, wcat_hl, bcat, tri)

    counts = cnt[0, MOE_GROUPS:MOE_GROUPS + e_num].astype(I32)
    padded = ((counts + blk - 1) // blk) * blk
    pad_end = jnp.cumsum(padded)
    pad_start = pad_end - padded
    blk_start = jnp.arange(nb, dtype=I32) * blk
    blk_expert = jnp.minimum(jnp.sum((pad_end[None, :] <= blk_start[:, None]).astype(I32), axis=1), e_num - 1)
    blk_valid = jnp.clip(pad_start[blk_expert] + counts[blk_expert] - blk_start, 0, blk).astype(I32)

    td = min(8192, n)
    dest1, dest2 = pl.pallas_call(
        _dest_kernel,
        grid_spec=pltpu.PrefetchScalarGridSpec(
            num_scalar_prefetch=1,
            grid=(n // td,),
            in_specs=[pl.BlockSpec((8, td), lambda i, ps_r: (0, i))],
            out_specs=[pl.BlockSpec((1, td), lambda i, ps_r: (0, i))] * 2,
        ),
        out_shape=[jax.ShapeDtypeStruct((1, n), I32)] * 2,
        compiler_params=_cparams(("arbitrary",)),
        name="moe_dest",
    )(pad_start.astype(I32), mt)

    xa = _sc_scatter_rows(hpa, dest1, dest2, n_rows)
    xb = _sc_scatter_rows(hpb, dest1, dest2, n_rows)

    rows_spec = pl.BlockSpec((blk, quarter), lambda i, be_r, nv_r: (i, 0))
    ya, yb = pl.pallas_call(
        _expert_kernel,
        grid_spec=pltpu.PrefetchScalarGridSpec(
            num_scalar_prefetch=2,
            grid=(nb,),
            in_specs=[rows_spec, rows_spec,
                      pl.BlockSpec((1, d, ff), lambda i, be_r, nv_r: (layer * e_num + be_r[i], 0, 0)),
                      pl.BlockSpec((1, d, ff), lambda i, be_r, nv_r: (layer * e_num + be_r[i], 0, 0)),
                      pl.BlockSpec((1, ff, d), lambda i, be_r, nv_r: (layer * e_num + be_r[i], 0, 0))],
            out_specs=[rows_spec, rows_spec],
            scratch_shapes=[pltpu.VMEM((d, ff), BF16), pltpu.VMEM((d, ff), BF16), pltpu.VMEM((ff, d), BF16)],
        ),
        out_shape=[jax.ShapeDtypeStruct((n_rows, quarter), I32)] * 2,
        compiler_params=_cparams(("arbitrary",)),
        name="moe_experts",
    )(blk_expert, blk_valid, xa, xb, w1, w3, w2)

    dest12 = jnp.concatenate([dest1, dest2], axis=1)
    ga = _sc_gather_rows(ya, dest12)
    gb = _sc_gather_rows(yb, dest12)

    first = pl.BlockSpec((ts, quarter), lambda b, i: (b * nt_b + i, 0))
    second = pl.BlockSpec((ts, quarter), lambda b, i: (nt + b * nt_b + i, 0))
    return pl.pallas_call(
        functools.partial(_combine_kernel, alpha),
        grid=(bsz, nt_b),
        in_specs=[tile, vec, flat(LANES), _const_spec((1, d)), _const_spec((1, d)),
                  first, first, second, second],
        out_specs=tile,
        out_shape=jax.ShapeDtypeStruct((bsz, s, d), F32),
        compiler_params=_cparams(("arbitrary", "arbitrary")),
        name="moe_combine",
    )(x, g, mf, ln_g.reshape(1, d), ln_b.reshape(1, d), ga, gb, ga, gb)


def kernel(x, c, ada_w, ada_b, ln_g, ln_b, hgrn_w_in, hgrn_w_out, hgrn_lb, hgrn_norm_w, attn_w_in, attn_w_out,
           attn_lambda, attn_subln_w, router_g_w, router_g_b, router_e_w, router_e_b, moe_w1, moe_w3, moe_w2):
    depth = ada_w.shape[0]
    bsz, s, d = x.shape
    alpha = (2 * depth) ** 0.25
    lb_all = jnp.cumsum(jax.nn.softmax(hgrn_lb.astype(F32), axis=0), axis=0)
    lb_all = lb_all - lb_all[0:1]
    mod = _ada_mod(c, ada_w, ada_b).reshape(depth, bsz, 6, 1, d)
    w1_all = moe_w1.reshape((-1,) + moe_w1.shape[2:])
    w3_all = moe_w3.reshape((-1,) + moe_w3.shape[2:])
    w2_all = moe_w2.reshape((-1,) + moe_w2.shape[2:])
    for i in range(depth):
        sh1, sc1, g1, sh2, sc2, g2 = (mod[i, :, m] for m in range(6))
        j = i // 2
        if i % 2 == 0:
            x = _hgrn_layer(x, sh1, sc1, g1, hgrn_w_in[j], hgrn_w_out[j], lb_all[j], hgrn_norm_w[j],
                            ln_g[i, 0], ln_b[i, 0], alpha)
        else:
            lambda_init = 0.8 - 0.6 * math.exp(-0.3 * i)
            x = _attn_layer(x, sh1, sc1, g1, attn_w_in[j], attn_w_out[j], attn_lambda[j], attn_subln_w[j],
                            lambda_init, ln_g[i, 0], ln_b[i, 0], alpha)
        x = _moe_layer(x, sh2, sc2, g2, router_g_w[i], router_g_b[i], router_e_w[i], router_e_b[i],
                       w1_all, w3_all, w2_all, i, ln_g[i, 1], ln_b[i, 1], alpha)
    return x
```

```python
import functools
import math

import jax
import jax.numpy as jnp
from jax import lax
from jax.experimental import pallas as pl
from jax.experimental.pallas import tpu as pltpu
from jax.experimental.pallas import tpu_sc as plsc

F32 = jnp.float32
BF16 = jnp.bfloat16
I32 = jnp.int32
U32 = jnp.uint32
HIGHEST = lax.Precision.HIGHEST

LANES = 128
HEAD = 128
HGRN_CHUNK = 32
HGRN_ROWS = 2
HGRN_GROUP = 1
HGRN_SAFE_SPAN = 80.0
ATTN_D = 64
FLASH_HEADS = 8
FLASH_ROW_CHUNK = 32
ROPE_THETA = 10000.0
MOE_GROUPS = 4
MOE_EPG = 8
MOE_EXPERTS = MOE_GROUPS * MOE_EPG
MOE_BLOCK = 512
SC_WINDOW = 128
NORM_EPS = 1e-5
VMEM_LIMIT = 56 * 1024 * 1024

NT_DIMS = (((1,), (1,)), ((), ()))
TN_DIMS = (((0,), (0,)), ((), ()))


def _cparams(sem):
    return pltpu.CompilerParams(dimension_semantics=sem, vmem_limit_bytes=VMEM_LIMIT)


def _const_spec(shape):
    nd = len(shape)
    return pl.BlockSpec(shape, lambda *_: (0,) * nd)


def _layer_norm(r, g, b):
    mu = jnp.mean(r, axis=-1, keepdims=True)
    d = r - mu
    var = jnp.mean(d * d, axis=-1, keepdims=True)
    return d * lax.rsqrt(var + NORM_EPS) * g + b


def _silu(x):
    return x * (1.0 / (1.0 + jnp.exp(-x)))


def _pack_bf16_pair(lo, hi):
    lo_b = lax.bitcast_convert_type(lo.astype(BF16).astype(F32), U32)
    hi_b = lax.bitcast_convert_type(hi.astype(BF16).astype(F32), U32)
    return (hi_b & jnp.uint32(0xFFFF0000)) | (lo_b >> 16)


def _unpack_bf16_pair(u):
    lo = lax.bitcast_convert_type(u << 16, F32)
    hi = lax.bitcast_convert_type(u & jnp.uint32(0xFFFF0000), F32)
    return lo, hi


def _ada_kernel(c_ref, w_ref, b_ref, o_ref):
    c = c_ref[...]
    o_ref[0] = jnp.dot(_silu(c), w_ref[0], precision=HIGHEST, preferred_element_type=F32) + b_ref[0]


def _ada_mod(c, ada_w, ada_b):
    depth, d, n6 = ada_w.shape
    bsz = c.shape[0]
    tn = d
    return pl.pallas_call(
        _ada_kernel,
        grid=(depth, n6 // tn),
        in_specs=[
            pl.BlockSpec((bsz, d), lambda i, j: (0, 0)),
            pl.BlockSpec((1, d, tn), lambda i, j: (i, 0, j)),
            pl.BlockSpec((1, 1, tn), lambda i, j: (i, 0, j)),
        ],
        out_specs=pl.BlockSpec((1, bsz, tn), lambda i, j: (i, 0, j)),
        out_shape=jax.ShapeDtypeStruct((depth, bsz, n6), F32),
        compiler_params=_cparams(("arbitrary", "arbitrary")),
        name="ada_mod",
    )(c, ada_w, ada_b.reshape(depth, 1, n6))


def _hgrn_kernel(alpha, x_ref, sh_ref, sc_ref, g_ref, win_ref, wout_ref, loglb_ref, oml_ref,
                 nw_ref, lng_ref, lnb_ref, o_ref,
                 proj_ref, st_ref, ocat_ref, b_ref, kk_ref, qi_ref, ks_ref, vb_ref, dec_ref):
    nrow, ts, d = x_ref.shape
    nh = d // HEAD
    c = HGRN_CHUNK
    nc = ts // c
    gw = min(HGRN_GROUP * HEAD, d)
    ng = d // gw
    hpg = gw // HEAD

    @pl.when(pl.program_id(1) == 0)
    def _():
        st_ref[...] = jnp.zeros_like(st_ref)

    hb = [(x_ref[r] * (1.0 + sc_ref[r]) + sh_ref[r]).astype(BF16) for r in range(nrow)]
    pos = lax.broadcasted_iota(I32, (ts, 1), 0) % c
    row = lax.broadcasted_iota(I32, (ts, ts), 0)
    col = lax.broadcasted_iota(I32, (ts, ts), 1)
    keep = (row >= col) & (row // c == col // c)
    span = jnp.float32(0.0)

    def project(r, g):
        ps = slice(g * 4 * gw, (g + 1) * 4 * gw)
        proj_ref[r, :, ps] = jnp.dot(hb[r], win_ref[:, ps], preferred_element_type=F32)

    for r in range(nrow):
        project(r, 0)
    for g, r in [(g, r) for g in range(ng) for r in range(nrow)]:
        if g + 1 < ng:
            project(r, g + 1)
        cs = slice(g * gw, (g + 1) * gw)
        p0 = g * 4 * gw
        q2 = proj_ref[r, :, p0:p0 + gw]
        z = proj_ref[r, :, p0 + gw:p0 + 2 * gw]

        ls = jnp.minimum(z, 0.0) - jnp.log(1.0 + jnp.exp(-jnp.abs(z)))
        lsn = ls - z
        cc = loglb_ref[:, cs] + lsn
        log_f = jnp.maximum(ls, cc) + jnp.log(1.0 + jnp.exp(-jnp.abs(ls - cc)))
        kk = oml_ref[:, cs] * jnp.exp(lsn)

        b = log_f
        step = 1
        while step < c:
            b = b + jnp.where(pos >= step, pltpu.roll(b, step, 0), 0.0)
            step *= 2
        b_ref[r, :, cs] = b
        kk_ref[r, :, cs] = kk

        b3 = b.reshape(nc, c, gw)
        b_last = b3[:, c - 1:c, :]
        b_mid = b3[:, c // 2 - 1:c // 2, :]
        q3 = q2.reshape(nc, c, gw)
        k3 = kk.reshape(nc, c, gw)
        qi_ref[r, :, cs] = (q3 * jnp.exp(b3)).astype(BF16).reshape(ts, gw)
        ks_ref[r, :, cs] = (k3 * jnp.exp(b_last - b3)).astype(BF16).reshape(ts, gw)
        q_intra = (q3 * jnp.exp(b3 - b_mid)).astype(BF16).reshape(ts, gw)
        k_intra = (k3 * jnp.exp(b_mid - b3)).astype(BF16).reshape(ts, gw)
        dec_ref[r, :, cs] = jnp.exp(b_last).reshape(nc, gw)
        span = jnp.maximum(span, jnp.max(-b_last))
        v2 = proj_ref[r, :, p0 + 2 * gw:p0 + 3 * gw].astype(BF16)
        vb_ref[r, :, cs] = v2

        for hh in range(hpg):
            sl = slice(hh * HEAD, (hh + 1) * HEAD)
            sc = lax.dot_general(q_intra[:, sl], k_intra[:, sl], NT_DIMS, preferred_element_type=F32)
            p = jnp.where(keep, sc, 0.0).astype(BF16)
            ocat_ref[r, :, g * gw + hh * HEAD:g * gw + (hh + 1) * HEAD] = jnp.dot(
                p, v2[:, sl], preferred_element_type=F32)

    @pl.when(span > HGRN_SAFE_SPAN)
    def _():
        tpos = lax.broadcasted_iota(I32, (ts, 1), 0)
        for r in range(nrow):
            b = b_ref[r]
            kk = kk_ref[r]
            q2 = jnp.concatenate([proj_ref[r, :, g * 4 * gw:g * 4 * gw + gw] for g in range(ng)], axis=1)
            diag = q2 * kk
            scores = [jnp.where(row == col, jnp.sum(diag[:, hd * HEAD:(hd + 1) * HEAD], axis=-1, keepdims=True), 0.0)
                      for hd in range(nh)]
            size = 2
            while size <= c:
                ref_row = (row // size) * size + (size // 2 - 1)
                pick = jnp.where(col == ref_row, 1.0, 0.0)
                b_at = jnp.dot(pick, b, precision=HIGHEST, preferred_element_type=F32)
                right = (tpos % size) >= (size // 2)
                qa = jnp.where(right, q2 * jnp.exp(b - b_at), 0.0).astype(BF16)
                ka = jnp.where(right, 0.0, kk * jnp.exp(b_at - b)).astype(BF16)
                same = (row // size) == (col // size)
                for hd in range(nh):
                    sl = slice(hd * HEAD, (hd + 1) * HEAD)
                    sc = lax.dot_general(qa[:, sl], ka[:, sl], NT_DIMS, preferred_element_type=F32)
                    scores[hd] = scores[hd] + jnp.where(same, sc, 0.0)
                size *= 2
            for hd in range(nh):
                sl = slice(hd * HEAD, (hd + 1) * HEAD)
                ocat_ref[r, :, sl] = jnp.dot(scores[hd].astype(BF16), vb_ref[r, :, sl],
                                             preferred_element_type=F32)

    for ci in range(nc):
        rs = slice(ci * c, (ci + 1) * c)
        for r, hd in [(r, hd) for hd in range(nh) for r in range(nrow)]:
            sl = slice(hd * HEAD, (hd + 1) * HEAD)
            st = st_ref[r, hd]
            o_inter = lax.dot_general(qi_ref[r, rs, sl], st.astype(BF16), NT_DIMS,
                                      preferred_element_type=F32)
            ocat_ref[r, rs, sl] += o_inter
            upd = lax.dot_general(vb_ref[r, rs, sl], ks_ref[r, rs, sl], TN_DIMS,
                                  preferred_element_type=F32)
            st_ref[r, hd] = st * dec_ref[r, ci:ci + 1, sl] + upd

    for r in range(nrow):
        for hd in range(nh):
            sl = slice(hd * HEAD, (hd + 1) * HEAD)
            g, hh = divmod(hd, hpg)
            oh = ocat_ref[r, :, sl]
            ms = jnp.mean(oh * oh, axis=-1, keepdims=True)
            gate = proj_ref[r, :, g * 4 * gw + 3 * gw + hh * HEAD:g * 4 * gw + 3 * gw + (hh + 1) * HEAD]
            ocat_ref[r, :, sl] = oh * lax.rsqrt(ms + NORM_EPS) * nw_ref[...] * _silu(gate)
        y = jnp.dot(ocat_ref[r].astype(BF16), wout_ref[...], preferred_element_type=F32)
        res = alpha * x_ref[r] + g_ref[r] * y
        o_ref[r] = _layer_norm(res, lng_ref[...], lnb_ref[...])


def _hgrn_layer(x, sh, sc, g, w_in, w_out, lb, norm_w, ln_g, ln_b, alpha):
    bsz, s, d = x.shape
    ts = min(256, s)
    nh = d // HEAD
    gw = min(HGRN_GROUP * HEAD, d)
    ng = d // gw
    w_grouped = w_in.reshape(d, 4, ng, gw).transpose(0, 2, 1, 3).reshape(d, 4 * d).astype(BF16)
    nr = math.gcd(HGRN_ROWS, bsz)
    vec = pl.BlockSpec((nr, 1, d), lambda b, i: (b, 0, 0))
    tile = pl.BlockSpec((nr, ts, d), lambda b, i: (b, i, 0))
    return pl.pallas_call(
        functools.partial(_hgrn_kernel, alpha),
        grid=(bsz // nr, s // ts),
        in_specs=[tile, vec, vec, vec,
                  _const_spec((d, 4 * d)), _const_spec((d, d)),
                  _const_spec((1, d)), _const_spec((1, d)), _const_spec((1, HEAD)),
                  _const_spec((1, d)), _const_spec((1, d))],
        out_specs=tile,
        out_shape=jax.ShapeDtypeStruct((bsz, s, d), F32),
        scratch_shapes=[pltpu.VMEM((nr, ts, 4 * d), F32),
                        pltpu.VMEM((nr, nh, HEAD, HEAD), F32),
                        pltpu.VMEM((nr, ts, d), F32),
                        pltpu.VMEM((nr, ts, d), F32), pltpu.VMEM((nr, ts, d), F32),
                        pltpu.VMEM((nr, ts, d), BF16), pltpu.VMEM((nr, ts, d), BF16),
                        pltpu.VMEM((nr, ts, d), BF16),
                        pltpu.VMEM((nr, ts // HGRN_CHUNK, d), F32)],
        compiler_params=_cparams(("arbitrary", "arbitrary")),
        name="hgrn_layer",
    )(x, sh, sc, g, w_grouped, w_out.astype(BF16),
      jnp.log(lb).reshape(1, d), (1.0 - lb).reshape(1, d), norm_w.reshape(1, HEAD),
      ln_g.reshape(1, d), ln_b.reshape(1, d))


def _attn_in_kernel(x_ref, sh_ref, sc_ref, w_ref, cos_ref, sin_ref, q_ref, k_ref, v_ref):
    d = x_ref.shape[2]
    x = x_ref[0]
    h = x * (1.0 + sc_ref[0]) + sh_ref[0]
    qkv = jnp.dot(h.astype(BF16), w_ref[...], preferred_element_type=F32)
    cos = cos_ref[...]
    sin = sin_ref[...]
    lane = lax.broadcasted_iota(I32, (1, LANES), 1)
    first_half = (lane % ATTN_D) < (ATTN_D // 2)
    scale = ATTN_D ** -0.5 * math.log2(math.e)
    for j in range(d // LANES):
        sl = slice(j * LANES, (j + 1) * LANES)
        for src, dst, mul in ((0, q_ref, scale), (d, k_ref, 1.0)):
            t = qkv[:, src + j * LANES:src + (j + 1) * LANES]
            partner = jnp.where(first_half, pltpu.roll(t, LANES - ATTN_D // 2, 1),
                                pltpu.roll(t, ATTN_D // 2, 1))
            dst[0, :, sl] = ((t * cos + partner * sin) * mul).astype(BF16)
    v_ref[0] = qkv[:, 2 * d:].astype(BF16)


def _flash_kernel(out_scale, lam_ref, q_ref, k_ref, v_ref, w_ref, o_ref, *scratch):
    tq = q_ref.shape[1]
    rows = 2 * tq
    qi = pl.program_id(2)
    nrep = tq // LANES
    rc = FLASH_ROW_CHUNK
    n_heads = q_ref.shape[2] // HEAD
    per_head = len(scratch) // n_heads
    heads = [(hd,) + tuple(scratch[hd * per_head:(hd + 1) * per_head]) for hd in range(n_heads)]
    lane = lax.broadcasted_iota(I32, (1, LANES), 1)

    for hd, m_ref, l_ref, acc_ref, _, _, _, qq_ref in heads:
        q = q_ref[0, :, hd * HEAD:(hd + 1) * HEAD]
        zero = jnp.zeros_like(q)
        qq_ref[0:tq, :] = jnp.where(lane < ATTN_D, q, zero)
        qq_ref[tq:rows, :] = jnp.where(lane >= ATTN_D, q, zero)
        m_ref[...] = jnp.full_like(m_ref, -jnp.inf)
        l_ref[...] = jnp.zeros_like(l_ref)
        acc_ref[...] = jnp.zeros_like(acc_ref)

    def scores(head, j):
        hd, _, _, _, _, s_ref, _, qq_ref = head
        kb = k_ref[0, pl.ds(pl.multiple_of(j * tq, tq), tq), hd * HEAD:(hd + 1) * HEAD]
        s_ref[...] = lax.dot_general(qq_ref[...], kb, NT_DIMS, preferred_element_type=F32)

    def softmax(head, masked):
        _, m_ref, l_ref, _, a_ref, s_ref, p_ref, _ = head
        for r0 in range(0, rows, rc):
            rs = slice(r0, r0 + rc)
            s = s_ref[rs, :]
            if masked:
                row = lax.broadcasted_iota(I32, (rc, tq), 0) + (r0 % tq)
                col = lax.broadcasted_iota(I32, (rc, tq), 1)
                s = jnp.where(row >= col, s, -jnp.inf)
            m_old = m_ref[rs, :]
            m_new = jnp.maximum(m_old, jnp.max(s, axis=-1, keepdims=True))
            a = jnp.exp2(m_old - m_new)
            p = jnp.exp2(s - jnp.concatenate([m_new] * nrep, axis=1))
            psum = p[:, 0:LANES]
            for r in range(1, nrep):
                psum = psum + p[:, r * LANES:(r + 1) * LANES]
            l_ref[rs, :] = a * l_ref[rs, :] + psum
            m_ref[rs, :] = m_new
            a_ref[rs, :] = a
            p_ref[rs, :] = p.astype(BF16)

    def values(head, j):
        hd, _, _, acc_ref, a_ref, _, p_ref, _ = head
        vb = v_ref[0, pl.ds(pl.multiple_of(j * tq, tq), tq), hd * HEAD:(hd + 1) * HEAD]
        acc_ref[...] = a_ref[...] * acc_ref[...] + jnp.dot(p_ref[...], vb, preferred_element_type=F32)

    nh_step = len(heads)
    n_blk = qi + 1

    def finalize(head):
        hd, _, l_ref, acc_ref, _, _, _, _ = head
        o_all = acc_ref[...] / jnp.sum(l_ref[...], axis=-1, keepdims=True)
        o = o_all[:tq] - lam_ref[0] * o_all[tq:]
        ms = jnp.mean(o * o, axis=-1, keepdims=True)
        o_ref[0, :, hd * HEAD:(hd + 1) * HEAD] = (
            o * lax.rsqrt(ms + NORM_EPS) * w_ref[...] * out_scale).astype(BF16)

    def emit(base, k_lo, k_hi, n_items, masked_from):
        for k in range(k_lo, k_hi + 1):
            if k + 1 < n_items:
                scores(heads[(k + 1) % nh_step], base + (k + 1) // nh_step)
            if 0 <= k < n_items:
                softmax(heads[k % nh_step], k >= masked_from)
            if 0 <= k - 1 < n_items:
                values(heads[(k - 1) % nh_step], base + (k - 1) // nh_step)
                if k - 1 >= masked_from:
                    finalize(heads[(k - 1) % nh_step])

    @pl.when(n_blk == 1)
    def _():
        emit(0, -1, nh_step, nh_step, 0)

    @pl.when(n_blk >= 2)
    def _():
        unbounded = 1 << 30
        emit(0, -1, 0, unbounded, unbounded)

        def body(t, carry):
            emit(t, 1, nh_step, unbounded, unbounded)
            return carry

        lax.fori_loop(0, n_blk - 2, body, 0)
        emit(n_blk - 2, 1, 2 * nh_step, 2 * nh_step, nh_step)


def _resid_ln_kernel(alpha, x_ref, g_ref, o_in_ref, w_ref, lng_ref, lnb_ref, o_ref):
    y = jnp.dot(o_in_ref[0], w_ref[...], preferred_element_type=F32)
    r = alpha * x_ref[0] + g_ref[0] * y
    o_ref[0] = _layer_norm(r, lng_ref[...], lnb_ref[...])


def _attn_layer(x, sh, sc, g, w_in, w_out, lam_params, subln_w, lambda_init, ln_g, ln_b, alpha):
    bsz, s, d = x.shape
    nh = d // HEAD
    ts = min(512, s)
    tq = min(256, s)
    half = ATTN_D // 2
    inv_freq = ROPE_THETA ** (-jnp.arange(half, dtype=F32) / half)
    ang = jnp.arange(s, dtype=F32)[:, None] * inv_freq[None, :]
    cos_t = jnp.tile(jnp.cos(ang), (1, LANES // half))
    sin_h = jnp.sin(ang)
    sin_t = jnp.tile(jnp.concatenate([-sin_h, sin_h], axis=1), (1, LANES // ATTN_D))

    vec = pl.BlockSpec((1, 1, d), lambda b, i: (b, 0, 0))
    tile = pl.BlockSpec((1, ts, d), lambda b, i: (b, i, 0))
    rope = pl.BlockSpec((ts, LANES), lambda b, i: (i, 0))
    q, k, v = pl.pallas_call(
        _attn_in_kernel,
        grid=(bsz, s // ts),
        in_specs=[tile, vec, vec, _const_spec((d, 3 * d)), rope, rope],
        out_specs=[tile, tile, tile],
        out_shape=[jax.ShapeDtypeStruct((bsz, s, d), BF16)] * 3,
        compiler_params=_cparams(("arbitrary", "arbitrary")),
        name="attn_in",
    )(x, sh, sc, w_in.astype(BF16), cos_t, sin_t)

    lp = lam_params.astype(F32)
    lam = (jnp.exp(jnp.sum(lp[0] * lp[1])) - jnp.exp(jnp.sum(lp[2] * lp[3])) + lambda_init).reshape(1)
    heads_per_step = math.gcd(FLASH_HEADS, nh)
    hw = heads_per_step * HEAD
    qspec = pl.BlockSpec((1, tq, hw), lambda b, h, i: (b, i, h))
    kvspec = pl.BlockSpec((1, s, hw), lambda b, h, i: (b, 0, h))
    head_scratch = [pltpu.VMEM((2 * tq, LANES), F32), pltpu.VMEM((2 * tq, LANES), F32),
                    pltpu.VMEM((2 * tq, HEAD), F32), pltpu.VMEM((2 * tq, LANES), F32),
                    pltpu.VMEM((2 * tq, tq), F32), pltpu.VMEM((2 * tq, tq), BF16),
                    pltpu.VMEM((2 * tq, HEAD), BF16)]
    o = pl.pallas_call(
        functools.partial(_flash_kernel, 1.0 - lambda_init),
        grid=(bsz, nh // heads_per_step, s // tq),
        in_specs=[pl.BlockSpec(memory_space=pltpu.SMEM), qspec, kvspec, kvspec,
                  pl.BlockSpec((1, HEAD), lambda b, h, i: (0, 0))],
        out_specs=qspec,
        out_shape=jax.ShapeDtypeStruct((bsz, s, d), BF16),
        scratch_shapes=head_scratch * heads_per_step,
        compiler_params=_cparams(("arbitrary", "arbitrary", "arbitrary")),
        name="diff_flash",
    )(lam, q, k, v, subln_w.reshape(1, HEAD))

    return pl.pallas_call(
        functools.partial(_resid_ln_kernel, alpha),
        grid=(bsz, s // ts),
        in_specs=[tile, vec, tile, _const_spec((d, d)), _const_spec((1, d)), _const_spec((1, d))],
        out_specs=tile,
        out_shape=jax.ShapeDtypeStruct((bsz, s, d), F32),
        compiler_params=_cparams(("arbitrary", "arbitrary")),
        name="attn_out",
    )(x, g, o, w_out.astype(BF16), ln_g.reshape(1, d), ln_b.reshape(1, d))


def _pack_row_parts(v):
    q = v.shape[1] // 4
    return tuple(lax.bitcast_convert_type(_pack_bf16_pair(v[:, p * q:(p + 1) * q], v[:, (2 + p) * q:(3 + p) * q]), I32)
                 for p in range(2))


def _unpack_row_parts(part_a, part_b):
    lo_a, hi_a = _unpack_bf16_pair(lax.bitcast_convert_type(part_a, U32))
    lo_b, hi_b = _unpack_bf16_pair(lax.bitcast_convert_type(part_b, U32))
    return lo_a, lo_b, hi_a, hi_b


def _router_kernel(x_ref, sh_ref, sc_ref, w_ref, bias_ref, tri_ref, hpa_ref, hpb_ref, mt_ref, mf_ref, cnt_ref,
                   cnt_scr):
    ts, d = x_ref.shape[1], x_ref.shape[2]
    first = (pl.program_id(0) == 0) & (pl.program_id(1) == 0)

    @pl.when(first)
    def _():
        cnt_scr[...] = jnp.zeros_like(cnt_scr)

    x = x_ref[0]
    h = x * (1.0 + sc_ref[0]) + sh_ref[0]
    hpa_ref[...], hpb_ref[...] = _pack_row_parts(h)

    h_hi = h.astype(BF16)
    h_lo = (h - h_hi.astype(F32)).astype(BF16)
    hh = jnp.dot(h_hi, w_ref[...], preferred_element_type=F32)
    lh = jnp.dot(h_lo, w_ref[:, 0:LANES], preferred_element_type=F32)
    logits = hh[:, 0:LANES] + hh[:, LANES:2 * LANES] + lh + bias_ref[...]
    lane = lax.broadcasted_iota(I32, (ts, LANES), 1)
    neg = -jnp.inf
    big = jnp.int32(LANES)
    is_g = lane < MOE_GROUPS
    gl = jnp.where(is_g, logits, neg)
    gmax = jnp.max(gl, axis=-1, keepdims=True)
    g_idx = jnp.min(jnp.where(gl == gmax, lane, big), axis=-1, keepdims=True)
    g_w = 1.0 / jnp.sum(jnp.exp(gl - gmax), axis=-1, keepdims=True)

    e_lane = lane - MOE_GROUPS
    in_grp = (e_lane >= g_idx * MOE_EPG) & (e_lane < (g_idx + 1) * MOE_EPG)
    el = jnp.where(in_grp, logits, neg)
    l1 = jnp.max(el, axis=-1, keepdims=True)
    i1 = jnp.min(jnp.where(el == l1, lane, big), axis=-1, keepdims=True)
    el2 = jnp.where(lane == i1, neg, el)
    l2 = jnp.max(el2, axis=-1, keepdims=True)
    i2 = jnp.min(jnp.where(el2 == l2, lane, big), axis=-1, keepdims=True)
    t = jnp.exp(l2 - l1)
    w1 = g_w / (1.0 + t)
    w2 = g_w * t / (1.0 + t)

    oh1 = (lane == i1)
    oh2 = (lane == i2)
    both = (oh1 | oh2).astype(BF16)
    before = jnp.dot(tri_ref[...], both, preferred_element_type=F32) + cnt_scr[...]
    r1 = jnp.sum(jnp.where(oh1, before, 0.0), axis=-1, keepdims=True)
    r2 = jnp.sum(jnp.where(oh2, before, 0.0), axis=-1, keepdims=True)
    cnt_scr[...] = cnt_scr[...] + jnp.sum(both.astype(F32), axis=0, keepdims=True)
    cnt_ref[...] = cnt_scr[...]

    e1 = i1 - MOE_GROUPS
    e2 = i2 - MOE_GROUPS
    meta = jnp.where(lane == 0, e1, jnp.where(lane == 1, e2, jnp.where(
        lane == 2, r1.astype(I32), jnp.where(lane == 3, r2.astype(I32), 0))))
    mt_ref[...] = jnp.transpose(meta)[0:8, :]
    mf_ref[...] = jnp.where(lane == 0, w1, jnp.where(lane == 1, w2, 0.0))


def _dest_kernel(ps_ref, mt_ref, d1_ref, d2_ref):
    e1 = mt_ref[0:1, :]
    e2 = mt_ref[1:2, :]
    p1 = jnp.zeros_like(e1)
    p2 = jnp.zeros_like(e2)
    for e in range(MOE_EXPERTS):
        p1 = jnp.where(e1 == e, ps_ref[e], p1)
        p2 = jnp.where(e2 == e, ps_ref[e], p2)
    d1_ref[...] = p1 + mt_ref[2:3, :]
    d2_ref[...] = p2 + mt_ref[3:4, :]


def _sc_mesh():
    return plsc.VectorSubcoreMesh(core_axis_name="c", subcore_axis_name="s")


def _sc_scatter_rows(src, idx_a, idx_b, n_rows):
    m, w = src.shape

    @functools.partial(pl.kernel, out_type=jax.ShapeDtypeStruct((n_rows, w), src.dtype),
                       mesh=_sc_mesh(), scratch_types=[], name="moe_sc_scatter")
    def scatter(x_hbm, ia_hbm, ib_hbm, o_hbm):
        def body(x_vmem, ia_vmem, ib_vmem):
            pltpu.sync_copy(x_vmem, o_hbm.at[ia_vmem.at[0]])
            pltpu.sync_copy(x_vmem, o_hbm.at[ib_vmem.at[0]])

        pltpu.emit_pipeline(
            body,
            grid=(m // SC_WINDOW,),
            in_specs=[pl.BlockSpec((SC_WINDOW, w), lambda i: (i, 0)),
                      pl.BlockSpec((1, SC_WINDOW), lambda i: (0, i)),
                      pl.BlockSpec((1, SC_WINDOW), lambda i: (0, i))],
            out_specs=[],
            core_axis_name=("c", "s"),
            dimension_semantics=(pltpu.PARALLEL,),
        )(x_hbm, ia_hbm, ib_hbm)

    return scatter(src, idx_a, idx_b)


def _sc_gather_rows(table, idx):
    m = idx.shape[1]
    w = table.shape[1]

    @functools.partial(pl.kernel, out_type=jax.ShapeDtypeStruct((m, w), table.dtype),
                       mesh=_sc_mesh(), scratch_types=[], name="moe_sc_gather")
    def gather(t_hbm, i_hbm, o_hbm):
        def body(i_vmem, o_vmem):
            pltpu.sync_copy(t_hbm.at[i_vmem.at[0]], o_vmem)

        pltpu.emit_pipeline(
            body,
            grid=(m // SC_WINDOW,),
            in_specs=[pl.BlockSpec((1, SC_WINDOW), lambda i: (0, i))],
            out_specs=[pl.BlockSpec((SC_WINDOW, w), lambda i: (i, 0))],
            core_axis_name=("c", "s"),
            dimension_semantics=(pltpu.PARALLEL,),
        )(i_hbm, o_hbm)

    return gather(table, idx)


def _expert_kernel(be_ref, nv_ref, xa_ref, xb_ref, w1_ref, w3_ref, w2_ref, ya_ref, yb_ref,
                   w1_scr, w3_scr, w2_scr):
    i = pl.program_id(0)
    blk = xa_ref.shape[0]
    n_valid = nv_ref[i]
    new_expert = (i == 0) | (be_ref[i] != be_ref[jnp.maximum(i - 1, 0)])

    @pl.when((n_valid > 0) & new_expert)
    def _():
        w1_scr[...] = w1_ref[0].astype(BF16)
        w3_scr[...] = w3_ref[0].astype(BF16)
        w2_scr[...] = w2_ref[0].astype(BF16)

    @pl.when(n_valid > 0)
    def _():
        valid = lax.broadcasted_iota(I32, (blk, 1), 0) < n_valid
        xq = _unpack_row_parts(xa_ref[...], xb_ref[...])
        xin = jnp.concatenate([jnp.where(valid, q, 0.0).astype(BF16) for q in xq], axis=1)
        a = jnp.dot(xin, w1_scr[...], preferred_element_type=F32)
        b = jnp.dot(xin, w3_scr[...], preferred_element_type=F32)
        hid = (_silu(a) * b).astype(BF16)
        y = jnp.dot(hid, w2_scr[...], preferred_element_type=F32)
        ya_ref[...], yb_ref[...] = _pack_row_parts(y)

    @pl.when(n_valid <= 0)
    def _():
        ya_ref[...] = jnp.zeros_like(ya_ref)
        yb_ref[...] = jnp.zeros_like(yb_ref)


def _combine_kernel(alpha, x_ref, g_ref, mf_ref, lng_ref, lnb_ref, y1a_ref, y1b_ref, y2a_ref, y2b_ref, o_ref):
    d = x_ref.shape[2]
    q = d // 4
    w1 = mf_ref[:, 0:1]
    w2 = mf_ref[:, 1:2]
    y1 = _unpack_row_parts(y1a_ref[...], y1b_ref[...])
    y2 = _unpack_row_parts(y2a_ref[...], y2b_ref[...])
    r = [alpha * x_ref[0, :, p * q:(p + 1) * q] + g_ref[0, :, p * q:(p + 1) * q] * (w1 * y1[p] + w2 * y2[p])
         for p in range(4)]
    mu = sum(jnp.sum(rp, axis=-1, keepdims=True) for rp in r) / d
    dev = [rp - mu for rp in r]
    var = sum(jnp.sum(dp * dp, axis=-1, keepdims=True) for dp in dev) / d
    inv = lax.rsqrt(var + NORM_EPS)
    for p in range(4):
        sl = slice(p * q, (p + 1) * q)
        o_ref[0, :, sl] = dev[p] * inv * lng_ref[:, sl] + lnb_ref[:, sl]


def _moe_layer(x, sh, sc, g, wg, bg, we, be, w1, w3, w2, layer, ln_g, ln_b, alpha):
    bsz, s, d = x.shape
    n = bsz * s
    quarter = d // 4
    e_num, blk = MOE_EXPERTS, MOE_BLOCK
    ff = w1.shape[-1]
    ts = min(512, s)
    nt_b = s // ts
    nt = n // ts
    n_rows = n * 2 + e_num * blk
    nb = n_rows // blk

    wcat = jnp.zeros((d, LANES), F32).at[:, :MOE_GROUPS].set(wg).at[:, MOE_GROUPS:MOE_GROUPS + e_num].set(we)
    bcat = jnp.zeros((1, LANES), F32).at[0, :MOE_GROUPS].set(bg).at[0, MOE_GROUPS:MOE_GROUPS + e_num].set(be)
    wcat_hi = wcat.astype(BF16)
    wcat_hl = jnp.concatenate([wcat_hi, (wcat - wcat_hi.astype(F32)).astype(BF16)], axis=1)
    tri = jnp.tri(ts, k=-1, dtype=BF16)

    vec = pl.BlockSpec((1, 1, d), lambda b, i: (b, 0, 0))
    tile = pl.BlockSpec((1, ts, d), lambda b, i: (b, i, 0))
    flat = lambda w: pl.BlockSpec((ts, w), lambda b, i: (b * nt_b + i, 0))
    hpa, hpb, mt, mf, cnt = pl.pallas_call(
        _router_kernel,
        grid=(bsz, nt_b),
        in_specs=[tile, vec, vec, _const_spec((d, 2 * LANES)), _const_spec((1, LANES)), _const_spec((ts, ts))],
        out_specs=[flat(quarter), flat(quarter), pl.BlockSpec((8, ts), lambda b, i: (0, b * nt_b + i)),
                   flat(LANES), _const_spec((1, LANES))],
        out_shape=[jax.ShapeDtypeStruct((n, quarter), I32), jax.ShapeDtypeStruct((n, quarter), I32),
                   jax.ShapeDtypeStruct((8, n), I32), jax.ShapeDtypeStruct((n, LANES), F32),
                   jax.ShapeDtypeStruct((1, LANES), F32)],
        scratch_shapes=[pltpu.VMEM((1, LANES), F32)],
        compiler_params=_cparams(("arbitrary", "arbitrary")),
        name="moe_router",
    )(x, sh, sc, wcat_hl, bcat, tri)

    counts = cnt[0, MOE_GROUPS:MOE_GROUPS + e_num].astype(I32)
    padded = ((counts + blk - 1) // blk) * blk
    pad_end = jnp.cumsum(padded)
    pad_start = pad_end - padded
    blk_start = jnp.arange(nb, dtype=I32) * blk
    blk_expert = jnp.minimum(jnp.sum((pad_end[None, :] <= blk_start[:, None]).astype(I32), axis=1), e_num - 1)
    blk_valid = jnp.clip(pad_start[blk_expert] + counts[blk_expert] - blk_start, 0, blk).astype(I32)

    td = min(8192, n)
    dest1, dest2 = pl.pallas_call(
        _dest_kernel,
        grid_spec=pltpu.PrefetchScalarGridSpec(
            num_scalar_prefetch=1,
            grid=(n // td,),
            in_specs=[pl.BlockSpec((8, td), lambda i, ps_r: (0, i))],
            out_specs=[pl.BlockSpec((1, td), lambda i, ps_r: (0, i))] * 2,
        ),
        out_shape=[jax.ShapeDtypeStruct((1, n), I32)] * 2,
        compiler_params=_cparams(("arbitrary",)),
        name="moe_dest",
    )(pad_start.astype(I32), mt)

    xa = _sc_scatter_rows(hpa, dest1, dest2, n_rows)
    xb = _sc_scatter_rows(hpb, dest1, dest2, n_rows)

    rows_spec = pl.BlockSpec((blk, quarter), lambda i, be_r, nv_r: (i, 0))
    ya, yb = pl.pallas_call(
        _expert_kernel,
        grid_spec=pltpu.PrefetchScalarGridSpec(
            num_scalar_prefetch=2,
            grid=(nb,),
            in_specs=[rows_spec, rows_spec,
                      pl.BlockSpec((1, d, ff), lambda i, be_r, nv_r: (layer * e_num + be_r[i], 0, 0)),
                      pl.BlockSpec((1, d, ff), lambda i, be_r, nv_r: (layer * e_num + be_r[i], 0, 0)),
                      pl.BlockSpec((1, ff, d), lambda i, be_r, nv_r: (layer * e_num + be_r[i], 0, 0))],
            out_specs=[rows_spec, rows_spec],
            scratch_shapes=[pltpu.VMEM((d, ff), BF16), pltpu.VMEM((d, ff), BF16), pltpu.VMEM((ff, d), BF16)],
        ),
        out_shape=[jax.ShapeDtypeStruct((n_rows, quarter), I32)] * 2,
        compiler_params=_cparams(("arbitrary",)),
        name="moe_experts",
    )(blk_expert, blk_valid, xa, xb, w1, w3, w2)

    dest12 = jnp.concatenate([dest1, dest2], axis=1)
    ga = _sc_gather_rows(ya, dest12)
    gb = _sc_gather_rows(yb, dest12)

    first = pl.BlockSpec((ts, quarter), lambda b, i: (b * nt_b + i, 0))
    second = pl.BlockSpec((ts, quarter), lambda b, i: (nt + b * nt_b + i, 0))
    return pl.pallas_call(
        functools.partial(_combine_kernel, alpha),
        grid=(bsz, nt_b),
        in_specs=[tile, vec, flat(LANES), _const_spec((1, d)), _const_spec((1, d)),
                  first, first, second, second],
        out_specs=tile,
        out_shape=jax.ShapeDtypeStruct((bsz, s, d), F32),
        compiler_params=_cparams(("arbitrary", "arbitrary")),
        name="moe_combine",
    )(x, g, mf, ln_g.reshape(1, d), ln_b.reshape(1, d), ga, gb, ga, gb)


def kernel(x, c, ada_w, ada_b, ln_g, ln_b, hgrn_w_in, hgrn_w_out, hgrn_lb, hgrn_norm_w, attn_w_in, attn_w_out,
           attn_lambda, attn_subln_w, router_g_w, router_g_b, router_e_w, router_e_b, moe_w1, moe_w3, moe_w2):
    depth = ada_w.shape[0]
    bsz, s, d = x.shape
    alpha = (2 * depth) ** 0.25
    lb_all = jnp.cumsum(jax.nn.softmax(hgrn_lb.astype(F32), axis=0), axis=0)
    lb_all = lb_all - lb_all[0:1]
    mod = _ada_mod(c, ada_w, ada_b).reshape(depth, bsz, 6, 1, d)
    w1_all = moe_w1.reshape((-1,) + moe_w1.shape[2:])
    w3_all = moe_w3.reshape((-1,) + moe_w3.shape[2:])
    w2_all = moe_w2.reshape((-1,) + moe_w2.shape[2:])
    for i in range(depth):
        sh1, sc1, g1, sh2, sc2, g2 = (mod[i, :, m] for m in range(6))
        j = i // 2
        if i % 2 == 0:
            x = _hgrn_layer(x, sh1, sc1, g1, hgrn_w_in[j], hgrn_w_out[j], lb_all[j], hgrn_norm_w[j],
                            ln_g[i, 0], ln_b[i, 0], alpha)
        else:
            lambda_init = 0.8 - 0.6 * math.exp(-0.3 * i)
            x = _attn_layer(x, sh1, sc1, g1, attn_w_in[j], attn_w_out[j], attn_lambda[j], attn_subln_w[j],
                            lambda_init, ln_g[i, 0], ln_b[i, 0], alpha)
        x = _moe_layer(x, sh2, sc2, g2, router_g_w[i], router_g_b[i], router_e_w[i], router_e_b[i],
                       w1_all, w3_all, w2_all, i, ln_g[i, 1], ln_b[i, 1], alpha)
    return x
```

```python
import functools
import math

import jax
import jax.numpy as jnp
from jax import lax
from jax.experimental import pallas as pl
from jax.experimental.pallas import tpu as pltpu
from jax.experimental.pallas import tpu_sc as plsc

F32 = jnp.float32
BF16 = jnp.bfloat16
I32 = jnp.int32
U32 = jnp.uint32
HIGHEST = lax.Precision.HIGHEST

LANES = 128
HEAD = 128
HGRN_CHUNK = 32
HGRN_ROWS = 2
HGRN_GROUP = 1
HGRN_SAFE_SPAN = 80.0
ATTN_D = 64
FLASH_HEADS = 8
FLASH_ROW_CHUNK = 32
ROPE_THETA = 10000.0
MOE_GROUPS = 4
MOE_EPG = 8
MOE_EXPERTS = MOE_GROUPS * MOE_EPG
MOE_COMBINE_PARTS = 2
MOE_BLOCK = 512
SC_WINDOW = 128
NORM_EPS = 1e-5
VMEM_LIMIT = 56 * 1024 * 1024

NT_DIMS = (((1,), (1,)), ((), ()))
TN_DIMS = (((0,), (0,)), ((), ()))


def _cparams(sem):
    return pltpu.CompilerParams(dimension_semantics=sem, vmem_limit_bytes=VMEM_LIMIT)


def _const_spec(shape):
    nd = len(shape)
    return pl.BlockSpec(shape, lambda *_: (0,) * nd)


def _layer_norm(r, g, b):
    mu = jnp.mean(r, axis=-1, keepdims=True)
    d = r - mu
    var = jnp.mean(d * d, axis=-1, keepdims=True)
    return d * lax.rsqrt(var + NORM_EPS) * g + b


def _silu(x):
    return x * (1.0 / (1.0 + jnp.exp(-x)))


def _pack_bf16_pair(lo, hi):
    lo_b = lax.bitcast_convert_type(lo.astype(BF16).astype(F32), U32)
    hi_b = lax.bitcast_convert_type(hi.astype(BF16).astype(F32), U32)
    return (hi_b & jnp.uint32(0xFFFF0000)) | (lo_b >> 16)


def _unpack_bf16_pair(u):
    lo = lax.bitcast_convert_type(u << 16, F32)
    hi = lax.bitcast_convert_type(u & jnp.uint32(0xFFFF0000), F32)
    return lo, hi


def _ada_kernel(c_ref, w_ref, b_ref, o_ref):
    c = c_ref[...]
    o_ref[0] = jnp.dot(_silu(c), w_ref[0], precision=HIGHEST, preferred_element_type=F32) + b_ref[0]


def _ada_mod(c, ada_w, ada_b):
    depth, d, n6 = ada_w.shape
    bsz = c.shape[0]
    tn = d
    return pl.pallas_call(
        _ada_kernel,
        grid=(depth, n6 // tn),
        in_specs=[
            pl.BlockSpec((bsz, d), lambda i, j: (0, 0)),
            pl.BlockSpec((1, d, tn), lambda i, j: (i, 0, j)),
            pl.BlockSpec((1, 1, tn), lambda i, j: (i, 0, j)),
        ],
        out_specs=pl.BlockSpec((1, bsz, tn), lambda i, j: (i, 0, j)),
        out_shape=jax.ShapeDtypeStruct((depth, bsz, n6), F32),
        compiler_params=_cparams(("arbitrary", "arbitrary")),
        name="ada_mod",
    )(c, ada_w, ada_b.reshape(depth, 1, n6))


def _hgrn_kernel(alpha, x_ref, sh_ref, sc_ref, g_ref, win_ref, wout_ref, loglb_ref, oml_ref,
                 nw_ref, lng_ref, lnb_ref, o_ref,
                 proj_ref, st_ref, ocat_ref, b_ref, kk_ref, qi_ref, ks_ref, vb_ref, dec_ref):
    nrow, ts, d = x_ref.shape
    nh = d // HEAD
    c = HGRN_CHUNK
    nc = ts // c
    gw = min(HGRN_GROUP * HEAD, d)
    ng = d // gw
    hpg = gw // HEAD

    @pl.when(pl.program_id(1) == 0)
    def _():
        st_ref[...] = jnp.zeros_like(st_ref)

    hb = [(x_ref[r] * (1.0 + sc_ref[r]) + sh_ref[r]).astype(BF16) for r in range(nrow)]
    pos = lax.broadcasted_iota(I32, (ts, 1), 0) % c
    row = lax.broadcasted_iota(I32, (ts, ts), 0)
    col = lax.broadcasted_iota(I32, (ts, ts), 1)
    keep = (row >= col) & (row // c == col // c)
    span = jnp.float32(0.0)

    def project(r, g):
        ps = slice(g * 4 * gw, (g + 1) * 4 * gw)
        proj_ref[r, :, ps] = jnp.dot(hb[r], win_ref[:, ps], preferred_element_type=F32)

    for r in range(nrow):
        project(r, 0)
    for g, r in [(g, r) for g in range(ng) for r in range(nrow)]:
        if g + 1 < ng:
            project(r, g + 1)
        cs = slice(g * gw, (g + 1) * gw)
        p0 = g * 4 * gw
        q2 = proj_ref[r, :, p0:p0 + gw]
        z = proj_ref[r, :, p0 + gw:p0 + 2 * gw]

        ls = jnp.minimum(z, 0.0) - jnp.log(1.0 + jnp.exp(-jnp.abs(z)))
        lsn = ls - z
        cc = loglb_ref[:, cs] + lsn
        log_f = jnp.maximum(ls, cc) + jnp.log(1.0 + jnp.exp(-jnp.abs(ls - cc)))
        kk = oml_ref[:, cs] * jnp.exp(lsn)

        b = log_f
        step = 1
        while step < c:
            b = b + jnp.where(pos >= step, pltpu.roll(b, step, 0), 0.0)
            step *= 2
        b_ref[r, :, cs] = b
        kk_ref[r, :, cs] = kk

        b3 = b.reshape(nc, c, gw)
        b_last = b3[:, c - 1:c, :]
        b_mid = b3[:, c // 2 - 1:c // 2, :]
        q3 = q2.reshape(nc, c, gw)
        k3 = kk.reshape(nc, c, gw)
        qi_ref[r, :, cs] = (q3 * jnp.exp(b3)).astype(BF16).reshape(ts, gw)
        ks_ref[r, :, cs] = (k3 * jnp.exp(b_last - b3)).astype(BF16).reshape(ts, gw)
        q_intra = (q3 * jnp.exp(b3 - b_mid)).astype(BF16).reshape(ts, gw)
        k_intra = (k3 * jnp.exp(b_mid - b3)).astype(BF16).reshape(ts, gw)
        dec_ref[r, :, cs] = jnp.exp(b_last).reshape(nc, gw)
        span = jnp.maximum(span, jnp.max(-b_last))
        v2 = proj_ref[r, :, p0 + 2 * gw:p0 + 3 * gw].astype(BF16)
        vb_ref[r, :, cs] = v2

        for hh in range(hpg):
            sl = slice(hh * HEAD, (hh + 1) * HEAD)
            sc = lax.dot_general(q_intra[:, sl], k_intra[:, sl], NT_DIMS, preferred_element_type=F32)
            p = jnp.where(keep, sc, 0.0).astype(BF16)
            ocat_ref[r, :, g * gw + hh * HEAD:g * gw + (hh + 1) * HEAD] = jnp.dot(
                p, v2[:, sl], preferred_element_type=F32)

    @pl.when(span > HGRN_SAFE_SPAN)
    def _():
        tpos = lax.broadcasted_iota(I32, (ts, 1), 0)
        for r in range(nrow):
            b = b_ref[r]
            kk = kk_ref[r]
            q2 = jnp.concatenate([proj_ref[r, :, g * 4 * gw:g * 4 * gw + gw] for g in range(ng)], axis=1)
            diag = q2 * kk
            scores = [jnp.where(row == col, jnp.sum(diag[:, hd * HEAD:(hd + 1) * HEAD], axis=-1, keepdims=True), 0.0)
                      for hd in range(nh)]
            size = 2
            while size <= c:
                ref_row = (row // size) * size + (size // 2 - 1)
                pick = jnp.where(col == ref_row, 1.0, 0.0)
                b_at = jnp.dot(pick, b, precision=HIGHEST, preferred_element_type=F32)
                right = (tpos % size) >= (size // 2)
                qa = jnp.where(right, q2 * jnp.exp(b - b_at), 0.0).astype(BF16)
                ka = jnp.where(right, 0.0, kk * jnp.exp(b_at - b)).astype(BF16)
                same = (row // size) == (col // size)
                for hd in range(nh):
                    sl = slice(hd * HEAD, (hd + 1) * HEAD)
                    sc = lax.dot_general(qa[:, sl], ka[:, sl], NT_DIMS, preferred_element_type=F32)
                    scores[hd] = scores[hd] + jnp.where(same, sc, 0.0)
                size *= 2
            for hd in range(nh):
                sl = slice(hd * HEAD, (hd + 1) * HEAD)
                ocat_ref[r, :, sl] = jnp.dot(scores[hd].astype(BF16), vb_ref[r, :, sl],
                                             preferred_element_type=F32)

    for ci in range(nc):
        rs = slice(ci * c, (ci + 1) * c)
        for r, hd in [(r, hd) for hd in range(nh) for r in range(nrow)]:
            sl = slice(hd * HEAD, (hd + 1) * HEAD)
            st = st_ref[r, hd]
            o_inter = lax.dot_general(qi_ref[r, rs, sl], st.astype(BF16), NT_DIMS,
                                      preferred_element_type=F32)
            ocat_ref[r, rs, sl] += o_inter
            upd = lax.dot_general(vb_ref[r, rs, sl], ks_ref[r, rs, sl], TN_DIMS,
                                  preferred_element_type=F32)
            st_ref[r, hd] = st * dec_ref[r, ci:ci + 1, sl] + upd

    for r in range(nrow):
        for hd in range(nh):
            sl = slice(hd * HEAD, (hd + 1) * HEAD)
            g, hh = divmod(hd, hpg)
            oh = ocat_ref[r, :, sl]
            ms = jnp.mean(oh * oh, axis=-1, keepdims=True)
            gate = proj_ref[r, :, g * 4 * gw + 3 * gw + hh * HEAD:g * 4 * gw + 3 * gw + (hh + 1) * HEAD]
            ocat_ref[r, :, sl] = oh * lax.rsqrt(ms + NORM_EPS) * nw_ref[...] * _silu(gate)
        y = jnp.dot(ocat_ref[r].astype(BF16), wout_ref[...], preferred_element_type=F32)
        res = alpha * x_ref[r] + g_ref[r] * y
        o_ref[r] = _layer_norm(res, lng_ref[...], lnb_ref[...])


def _hgrn_layer(x, sh, sc, g, w_in, w_out, lb, norm_w, ln_g, ln_b, alpha):
    bsz, s, d = x.shape
    ts = min(256, s)
    nh = d // HEAD
    gw = min(HGRN_GROUP * HEAD, d)
    ng = d // gw
    w_grouped = w_in.reshape(d, 4, ng, gw).transpose(0, 2, 1, 3).reshape(d, 4 * d).astype(BF16)
    nr = math.gcd(HGRN_ROWS, bsz)
    vec = pl.BlockSpec((nr, 1, d), lambda b, i: (b, 0, 0))
    tile = pl.BlockSpec((nr, ts, d), lambda b, i: (b, i, 0))
    return pl.pallas_call(
        functools.partial(_hgrn_kernel, alpha),
        grid=(bsz // nr, s // ts),
        in_specs=[tile, vec, vec, vec,
                  _const_spec((d, 4 * d)), _const_spec((d, d)),
                  _const_spec((1, d)), _const_spec((1, d)), _const_spec((1, HEAD)),
                  _const_spec((1, d)), _const_spec((1, d))],
        out_specs=tile,
        out_shape=jax.ShapeDtypeStruct((bsz, s, d), F32),
        scratch_shapes=[pltpu.VMEM((nr, ts, 4 * d), F32),
                        pltpu.VMEM((nr, nh, HEAD, HEAD), F32),
                        pltpu.VMEM((nr, ts, d), F32),
                        pltpu.VMEM((nr, ts, d), F32), pltpu.VMEM((nr, ts, d), F32),
                        pltpu.VMEM((nr, ts, d), BF16), pltpu.VMEM((nr, ts, d), BF16),
                        pltpu.VMEM((nr, ts, d), BF16),
                        pltpu.VMEM((nr, ts // HGRN_CHUNK, d), F32)],
        compiler_params=_cparams(("arbitrary", "arbitrary")),
        name="hgrn_layer",
    )(x, sh, sc, g, w_grouped, w_out.astype(BF16),
      jnp.log(lb).reshape(1, d), (1.0 - lb).reshape(1, d), norm_w.reshape(1, HEAD),
      ln_g.reshape(1, d), ln_b.reshape(1, d))


def _attn_in_kernel(x_ref, sh_ref, sc_ref, w_ref, cos_ref, sin_ref, q_ref, k_ref, v_ref):
    d = x_ref.shape[2]
    x = x_ref[0]
    h = x * (1.0 + sc_ref[0]) + sh_ref[0]
    qkv = jnp.dot(h.astype(BF16), w_ref[...], preferred_element_type=F32)
    cos = cos_ref[...]
    sin = sin_ref[...]
    lane = lax.broadcasted_iota(I32, (1, LANES), 1)
    first_half = (lane % ATTN_D) < (ATTN_D // 2)
    scale = ATTN_D ** -0.5 * math.log2(math.e)
    for j in range(d // LANES):
        sl = slice(j * LANES, (j + 1) * LANES)
        for src, dst, mul in ((0, q_ref, scale), (d, k_ref, 1.0)):
            t = qkv[:, src + j * LANES:src + (j + 1) * LANES]
            partner = jnp.where(first_half, pltpu.roll(t, LANES - ATTN_D // 2, 1),
                                pltpu.roll(t, ATTN_D // 2, 1))
            dst[0, :, sl] = ((t * cos + partner * sin) * mul).astype(BF16)
    v_ref[0] = qkv[:, 2 * d:].astype(BF16)


def _flash_kernel(out_scale, lam_ref, q_ref, k_ref, v_ref, w_ref, o_ref, *scratch):
    tq = q_ref.shape[1]
    rows = 2 * tq
    qi = pl.program_id(2)
    nrep = tq // LANES
    rc = FLASH_ROW_CHUNK
    n_heads = q_ref.shape[2] // HEAD
    per_head = len(scratch) // n_heads
    heads = [(hd,) + tuple(scratch[hd * per_head:(hd + 1) * per_head]) for hd in range(n_heads)]
    lane = lax.broadcasted_iota(I32, (1, LANES), 1)

    for hd, m_ref, l_ref, acc_ref, _, _, _, qq_ref in heads:
        q = q_ref[0, :, hd * HEAD:(hd + 1) * HEAD]
        zero = jnp.zeros_like(q)
        qq_ref[0:tq, :] = jnp.where(lane < ATTN_D, q, zero)
        qq_ref[tq:rows, :] = jnp.where(lane >= ATTN_D, q, zero)
        m_ref[...] = jnp.full_like(m_ref, -jnp.inf)
        l_ref[...] = jnp.zeros_like(l_ref)
        acc_ref[...] = jnp.zeros_like(acc_ref)

    def scores(head, j):
        hd, _, _, _, _, s_ref, _, qq_ref = head
        kb = k_ref[0, pl.ds(pl.multiple_of(j * tq, tq), tq), hd * HEAD:(hd + 1) * HEAD]
        s_ref[...] = lax.dot_general(qq_ref[...], kb, NT_DIMS, preferred_element_type=F32)

    def softmax(head, masked):
        _, m_ref, l_ref, _, a_ref, s_ref, p_ref, _ = head
        for r0 in range(0, rows, rc):
            rs = slice(r0, r0 + rc)
            s = s_ref[rs, :]
            if masked:
                row = lax.broadcasted_iota(I32, (rc, tq), 0) + (r0 % tq)
                col = lax.broadcasted_iota(I32, (rc, tq), 1)
                s = jnp.where(row >= col, s, -jnp.inf)
            m_old = m_ref[rs, :]
            m_new = jnp.maximum(m_old, jnp.max(s, axis=-1, keepdims=True))
            a = jnp.exp2(m_old - m_new)
            p = jnp.exp2(s - jnp.concatenate([m_new] * nrep, axis=1))
            psum = p[:, 0:LANES]
            for r in range(1, nrep):
                psum = psum + p[:, r * LANES:(r + 1) * LANES]
            l_ref[rs, :] = a * l_ref[rs, :] + psum
            m_ref[rs, :] = m_new
            a_ref[rs, :] = a
            p_ref[rs, :] = p.astype(BF16)

    def values(head, j):
        hd, _, _, acc_ref, a_ref, _, p_ref, _ = head
        vb = v_ref[0, pl.ds(pl.multiple_of(j * tq, tq), tq), hd * HEAD:(hd + 1) * HEAD]
        acc_ref[...] = a_ref[...] * acc_ref[...] + jnp.dot(p_ref[...], vb, preferred_element_type=F32)

    nh_step = len(heads)
    n_blk = qi + 1

    def finalize(head):
        hd, _, l_ref, acc_ref, _, _, _, _ = head
        o_all = acc_ref[...] / jnp.sum(l_ref[...], axis=-1, keepdims=True)
        o = o_all[:tq] - lam_ref[0] * o_all[tq:]
        ms = jnp.mean(o * o, axis=-1, keepdims=True)
        o_ref[0, :, hd * HEAD:(hd + 1) * HEAD] = (
            o * lax.rsqrt(ms + NORM_EPS) * w_ref[...] * out_scale).astype(BF16)

    def emit(base, k_lo, k_hi, n_items, masked_from):
        for k in range(k_lo, k_hi + 1):
            if k + 1 < n_items:
                scores(heads[(k + 1) % nh_step], base + (k + 1) // nh_step)
            if 0 <= k < n_items:
                softmax(heads[k % nh_step], k >= masked_from)
            if 0 <= k - 1 < n_items:
                values(heads[(k - 1) % nh_step], base + (k - 1) // nh_step)
                if k - 1 >= masked_from:
                    finalize(heads[(k - 1) % nh_step])

    @pl.when(n_blk == 1)
    def _():
        emit(0, -1, nh_step, nh_step, 0)

    @pl.when(n_blk >= 2)
    def _():
        unbounded = 1 << 30
        emit(0, -1, 0, unbounded, unbounded)

        def body(t, carry):
            emit(t, 1, nh_step, unbounded, unbounded)
            return carry

        lax.fori_loop(0, n_blk - 2, body, 0)
        emit(n_blk - 2, 1, 2 * nh_step, 2 * nh_step, nh_step)


def _resid_ln_kernel(alpha, x_ref, g_ref, o_in_ref, w_ref, lng_ref, lnb_ref, o_ref):
    y = jnp.dot(o_in_ref[0], w_ref[...], preferred_element_type=F32)
    r = alpha * x_ref[0] + g_ref[0] * y
    o_ref[0] = _layer_norm(r, lng_ref[...], lnb_ref[...])


def _attn_layer(x, sh, sc, g, w_in, w_out, lam_params, subln_w, lambda_init, ln_g, ln_b, alpha):
    bsz, s, d = x.shape
    nh = d // HEAD
    ts = min(512, s)
    tq = min(256, s)
    half = ATTN_D // 2
    inv_freq = ROPE_THETA ** (-jnp.arange(half, dtype=F32) / half)
    ang = jnp.arange(s, dtype=F32)[:, None] * inv_freq[None, :]
    cos_t = jnp.tile(jnp.cos(ang), (1, LANES // half))
    sin_h = jnp.sin(ang)
    sin_t = jnp.tile(jnp.concatenate([-sin_h, sin_h], axis=1), (1, LANES // ATTN_D))

    vec = pl.BlockSpec((1, 1, d), lambda b, i: (b, 0, 0))
    tile = pl.BlockSpec((1, ts, d), lambda b, i: (b, i, 0))
    rope = pl.BlockSpec((ts, LANES), lambda b, i: (i, 0))
    q, k, v = pl.pallas_call(
        _attn_in_kernel,
        grid=(bsz, s // ts),
        in_specs=[tile, vec, vec, _const_spec((d, 3 * d)), rope, rope],
        out_specs=[tile, tile, tile],
        out_shape=[jax.ShapeDtypeStruct((bsz, s, d), BF16)] * 3,
        compiler_params=_cparams(("arbitrary", "arbitrary")),
        name="attn_in",
    )(x, sh, sc, w_in.astype(BF16), cos_t, sin_t)

    lp = lam_params.astype(F32)
    lam = (jnp.exp(jnp.sum(lp[0] * lp[1])) - jnp.exp(jnp.sum(lp[2] * lp[3])) + lambda_init).reshape(1)
    heads_per_step = math.gcd(FLASH_HEADS, nh)
    hw = heads_per_step * HEAD
    qspec = pl.BlockSpec((1, tq, hw), lambda b, h, i: (b, i, h))
    kvspec = pl.BlockSpec((1, s, hw), lambda b, h, i: (b, 0, h))
    head_scratch = [pltpu.VMEM((2 * tq, LANES), F32), pltpu.VMEM((2 * tq, LANES), F32),
                    pltpu.VMEM((2 * tq, HEAD), F32), pltpu.VMEM((2 * tq, LANES), F32),
                    pltpu.VMEM((2 * tq, tq), F32), pltpu.VMEM((2 * tq, tq), BF16),
                    pltpu.VMEM((2 * tq, HEAD), BF16)]
    o = pl.pallas_call(
        functools.partial(_flash_kernel, 1.0 - lambda_init),
        grid=(bsz, nh // heads_per_step, s // tq),
        in_specs=[pl.BlockSpec(memory_space=pltpu.SMEM), qspec, kvspec, kvspec,
                  pl.BlockSpec((1, HEAD), lambda b, h, i: (0, 0))],
        out_specs=qspec,
        out_shape=jax.ShapeDtypeStruct((bsz, s, d), BF16),
        scratch_shapes=head_scratch * heads_per_step,
        compiler_params=_cparams(("arbitrary", "arbitrary", "arbitrary")),
        name="diff_flash",
    )(lam, q, k, v, subln_w.reshape(1, HEAD))

    return pl.pallas_call(
        functools.partial(_resid_ln_kernel, alpha),
        grid=(bsz, s // ts),
        in_specs=[tile, vec, tile, _const_spec((d, d)), _const_spec((1, d)), _const_spec((1, d))],
        out_specs=tile,
        out_shape=jax.ShapeDtypeStruct((bsz, s, d), F32),
        compiler_params=_cparams(("arbitrary", "arbitrary")),
        name="attn_out",
    )(x, g, o, w_out.astype(BF16), ln_g.reshape(1, d), ln_b.reshape(1, d))


def _pack_row_parts(v):
    q = v.shape[1] // 4
    return tuple(lax.bitcast_convert_type(_pack_bf16_pair(v[:, p * q:(p + 1) * q], v[:, (2 + p) * q:(3 + p) * q]), I32)
                 for p in range(2))


def _unpack_row_parts(part_a, part_b):
    lo_a, hi_a = _unpack_bf16_pair(lax.bitcast_convert_type(part_a, U32))
    lo_b, hi_b = _unpack_bf16_pair(lax.bitcast_convert_type(part_b, U32))
    return lo_a, lo_b, hi_a, hi_b


def _router_kernel(x_ref, sh_ref, sc_ref, w_ref, bias_ref, tri_ref, hpa_ref, hpb_ref, mt_ref, mf_ref, cnt_ref,
                   cnt_scr):
    ts, d = x_ref.shape[1], x_ref.shape[2]
    first = (pl.program_id(0) == 0) & (pl.program_id(1) == 0)

    @pl.when(first)
    def _():
        cnt_scr[...] = jnp.zeros_like(cnt_scr)

    x = x_ref[0]
    h = x * (1.0 + sc_ref[0]) + sh_ref[0]
    hpa_ref[...], hpb_ref[...] = _pack_row_parts(h)

    h_hi = h.astype(BF16)
    h_lo = (h - h_hi.astype(F32)).astype(BF16)
    hh = jnp.dot(h_hi, w_ref[...], preferred_element_type=F32)
    lh = jnp.dot(h_lo, w_ref[:, 0:LANES], preferred_element_type=F32)
    logits = hh[:, 0:LANES] + hh[:, LANES:2 * LANES] + lh + bias_ref[...]
    lane = lax.broadcasted_iota(I32, (ts, LANES), 1)
    neg = -jnp.inf
    big = jnp.int32(LANES)
    is_g = lane < MOE_GROUPS
    gl = jnp.where(is_g, logits, neg)
    gmax = jnp.max(gl, axis=-1, keepdims=True)
    g_idx = jnp.min(jnp.where(gl == gmax, lane, big), axis=-1, keepdims=True)
    g_w = 1.0 / jnp.sum(jnp.exp(gl - gmax), axis=-1, keepdims=True)

    e_lane = lane - MOE_GROUPS
    in_grp = (e_lane >= g_idx * MOE_EPG) & (e_lane < (g_idx + 1) * MOE_EPG)
    el = jnp.where(in_grp, logits, neg)
    l1 = jnp.max(el, axis=-1, keepdims=True)
    i1 = jnp.min(jnp.where(el == l1, lane, big), axis=-1, keepdims=True)
    el2 = jnp.where(lane == i1, neg, el)
    l2 = jnp.max(el2, axis=-1, keepdims=True)
    i2 = jnp.min(jnp.where(el2 == l2, lane, big), axis=-1, keepdims=True)
    t = jnp.exp(l2 - l1)
    w1 = g_w / (1.0 + t)
    w2 = g_w * t / (1.0 + t)

    oh1 = (lane == i1)
    oh2 = (lane == i2)
    both = (oh1 | oh2).astype(BF16)
    before = jnp.dot(tri_ref[...], both, preferred_element_type=F32) + cnt_scr[...]
    r1 = jnp.sum(jnp.where(oh1, before, 0.0), axis=-1, keepdims=True)
    r2 = jnp.sum(jnp.where(oh2, before, 0.0), axis=-1, keepdims=True)
    cnt_scr[...] = cnt_scr[...] + jnp.sum(both.astype(F32), axis=0, keepdims=True)
    cnt_ref[...] = cnt_scr[...]

    e1 = i1 - MOE_GROUPS
    e2 = i2 - MOE_GROUPS
    meta = jnp.where(lane == 0, e1, jnp.where(lane == 1, e2, jnp.where(
        lane == 2, r1.astype(I32), jnp.where(lane == 3, r2.astype(I32), 0))))
    mt_ref[...] = jnp.transpose(meta)[0:8, :]
    mf_ref[...] = jnp.where(lane == 0, w1, jnp.where(lane == 1, w2, 0.0))


def _dest_kernel(ps_ref, mt_ref, d1_ref, d2_ref):
    e1 = mt_ref[0:1, :]
    e2 = mt_ref[1:2, :]
    p1 = jnp.zeros_like(e1)
    p2 = jnp.zeros_like(e2)
    for e in range(MOE_EXPERTS):
        p1 = jnp.where(e1 == e, ps_ref[e], p1)
        p2 = jnp.where(e2 == e, ps_ref[e], p2)
    d1_ref[...] = p1 + mt_ref[2:3, :]
    d2_ref[...] = p2 + mt_ref[3:4, :]


def _sc_mesh():
    return plsc.VectorSubcoreMesh(core_axis_name="c", subcore_axis_name="s")


def _sc_scatter_rows(src, idx_a, idx_b, n_rows):
    m, w = src.shape

    @functools.partial(pl.kernel, out_type=jax.ShapeDtypeStruct((n_rows, w), src.dtype),
                       mesh=_sc_mesh(), scratch_types=[], name="moe_sc_scatter")
    def scatter(x_hbm, ia_hbm, ib_hbm, o_hbm):
        def body(x_vmem, ia_vmem, ib_vmem):
            pltpu.sync_copy(x_vmem, o_hbm.at[ia_vmem.at[0]])
            pltpu.sync_copy(x_vmem, o_hbm.at[ib_vmem.at[0]])

        pltpu.emit_pipeline(
            body,
            grid=(m // SC_WINDOW,),
            in_specs=[pl.BlockSpec((SC_WINDOW, w), lambda i: (i, 0)),
                      pl.BlockSpec((1, SC_WINDOW), lambda i: (0, i)),
                      pl.BlockSpec((1, SC_WINDOW), lambda i: (0, i))],
            out_specs=[],
            core_axis_name=("c", "s"),
            dimension_semantics=(pltpu.PARALLEL,),
        )(x_hbm, ia_hbm, ib_hbm)

    return scatter(src, idx_a, idx_b)


def _sc_gather_rows(table, idx):
    m = idx.shape[1]
    w = table.shape[1]

    @functools.partial(pl.kernel, out_type=jax.ShapeDtypeStruct((m, w), table.dtype),
                       mesh=_sc_mesh(), scratch_types=[], name="moe_sc_gather")
    def gather(t_hbm, i_hbm, o_hbm):
        def body(i_vmem, o_vmem):
            pltpu.sync_copy(t_hbm.at[i_vmem.at[0]], o_vmem)

        pltpu.emit_pipeline(
            body,
            grid=(m // SC_WINDOW,),
            in_specs=[pl.BlockSpec((1, SC_WINDOW), lambda i: (0, i))],
            out_specs=[pl.BlockSpec((SC_WINDOW, w), lambda i: (i, 0))],
            core_axis_name=("c", "s"),
            dimension_semantics=(pltpu.PARALLEL,),
        )(i_hbm, o_hbm)

    return gather(table, idx)


def _expert_kernel(be_ref, nv_ref, xa_ref, xb_ref, w1_ref, w3_ref, w2_ref, ya_ref, yb_ref,
                   w1_scr, w3_scr, w2_scr):
    i = pl.program_id(0)
    blk = xa_ref.shape[0]
    n_valid = nv_ref[i]
    new_expert = (i == 0) | (be_ref[i] != be_ref[jnp.maximum(i - 1, 0)])

    @pl.when((n_valid > 0) & new_expert)
    def _():
        w1_scr[...] = w1_ref[0].astype(BF16)
        w3_scr[...] = w3_ref[0].astype(BF16)
        w2_scr[...] = w2_ref[0].astype(BF16)

    @pl.when(n_valid > 0)
    def _():
        valid = lax.broadcasted_iota(I32, (blk, 1), 0) < n_valid
        xq = _unpack_row_parts(xa_ref[...], xb_ref[...])
        xin = jnp.concatenate([jnp.where(valid, q, 0.0).astype(BF16) for q in xq], axis=1)
        a = jnp.dot(xin, w1_scr[...], preferred_element_type=F32)
        b = jnp.dot(xin, w3_scr[...], preferred_element_type=F32)
        hid = (_silu(a) * b).astype(BF16)
        y = jnp.dot(hid, w2_scr[...], preferred_element_type=F32)
        ya_ref[...], yb_ref[...] = _pack_row_parts(y)

    @pl.when(n_valid <= 0)
    def _():
        ya_ref[...] = jnp.zeros_like(ya_ref)
        yb_ref[...] = jnp.zeros_like(yb_ref)


def _combine_kernel(alpha, x_ref, g_ref, mf_ref, lng_ref, lnb_ref, y1a_ref, y1b_ref, y2a_ref, y2b_ref, *rest):
    o_ref = rest[-1]
    d = x_ref.shape[2]
    q = d // 4
    w1 = mf_ref[:, 0:1]
    w2 = mf_ref[:, 1:2]
    y1 = _unpack_row_parts(y1a_ref[...], y1b_ref[...])
    y2 = _unpack_row_parts(y2a_ref[...], y2b_ref[...])
    r = [alpha * x_ref[0, :, p * q:(p + 1) * q] + g_ref[0, :, p * q:(p + 1) * q] * (w1 * y1[p] + w2 * y2[p])
         for p in range(4)]
    mu = sum(jnp.sum(rp, axis=-1, keepdims=True) for rp in r) / d
    dev = [rp - mu for rp in r]
    var = sum(jnp.sum(dp * dp, axis=-1, keepdims=True) for dp in dev) / d
    inv = lax.rsqrt(var + NORM_EPS)
    for p in range(4):
        sl = slice(p * q, (p + 1) * q)
        o_ref[0, :, sl] = dev[p] * inv * lng_ref[:, sl] + lnb_ref[:, sl]


def _moe_layer(x, sh, sc, g, wg, bg, we, be, w1, w3, w2, layer, ln_g, ln_b, alpha):
    bsz, s, d = x.shape
    n = bsz * s
    quarter = d // 4
    e_num, blk = MOE_EXPERTS, MOE_BLOCK
    ff = w1.shape[-1]
    ts = min(512, s)
    nt_b = s // ts
    nt = n // ts
    n_rows = n * 2 + e_num * blk
    nb = n_rows // blk

    wcat = jnp.zeros((d, LANES), F32).at[:, :MOE_GROUPS].set(wg).at[:, MOE_GROUPS:MOE_GROUPS + e_num].set(we)
    bcat = jnp.zeros((1, LANES), F32).at[0, :MOE_GROUPS].set(bg).at[0, MOE_GROUPS:MOE_GROUPS + e_num].set(be)
    wcat_hi = wcat.astype(BF16)
    wcat_hl = jnp.concatenate([wcat_hi, (wcat - wcat_hi.astype(F32)).astype(BF16)], axis=1)
    tri = jnp.tri(ts, k=-1, dtype=BF16)

    vec = pl.BlockSpec((1, 1, d), lambda b, i: (b, 0, 0))
    tile = pl.BlockSpec((1, ts, d), lambda b, i: (b, i, 0))
    flat = lambda w: pl.BlockSpec((ts, w), lambda b, i: (b * nt_b + i, 0))
    hpa, hpb, mt, mf, cnt = pl.pallas_call(
        _router_kernel,
        grid=(bsz, nt_b),
        in_specs=[tile, vec, vec, _const_spec((d, 2 * LANES)), _const_spec((1, LANES)), _const_spec((ts, ts))],
        out_specs=[flat(quarter), flat(quarter), pl.BlockSpec((8, ts), lambda b, i: (0, b * nt_b + i)),
                   flat(LANES), _const_spec((1, LANES))],
        out_shape=[jax.ShapeDtypeStruct((n, quarter), I32), jax.ShapeDtypeStruct((n, quarter), I32),
                   jax.ShapeDtypeStruct((8, n), I32), jax.ShapeDtypeStruct((n, LANES), F32),
                   jax.ShapeDtypeStruct((1, LANES), F32)],
        scratch_shapes=[pltpu.VMEM((1, LANES), F32)],
        compiler_params=_cparams(("arbitrary", "arbitrary")),
        name="moe_router",
    )(x, sh, sc, wcat_hl, bcat, tri)

    counts = cnt[0, MOE_GROUPS:MOE_GROUPS + e_num].astype(I32)
    padded = ((counts + blk - 1) // blk) * blk
    pad_end = jnp.cumsum(padded)
    pad_start = pad_end - padded
    blk_start = jnp.arange(nb, dtype=I32) * blk
    blk_expert = jnp.minimum(jnp.sum((pad_end[None, :] <= blk_start[:, None]).astype(I32), axis=1), e_num - 1)
    blk_valid = jnp.clip(pad_start[blk_expert] + counts[blk_expert] - blk_start, 0, blk).astype(I32)

    td = min(8192, n)
    dest1, dest2 = pl.pallas_call(
        _dest_kernel,
        grid_spec=pltpu.PrefetchScalarGridSpec(
            num_scalar_prefetch=1,
            grid=(n // td,),
            in_specs=[pl.BlockSpec((8, td), lambda i, ps_r: (0, i))],
            out_specs=[pl.BlockSpec((1, td), lambda i, ps_r: (0, i))] * 2,
        ),
        out_shape=[jax.ShapeDtypeStruct((1, n), I32)] * 2,
        compiler_params=_cparams(("arbitrary",)),
        name="moe_dest",
    )(pad_start.astype(I32), mt)

    xa = _sc_scatter_rows(hpa, dest1, dest2, n_rows)
    xb = _sc_scatter_rows(hpb, dest1, dest2, n_rows)

    rows_spec = pl.BlockSpec((blk, quarter), lambda i, be_r, nv_r: (i, 0))
    ya, yb = pl.pallas_call(
        _expert_kernel,
        grid_spec=pltpu.PrefetchScalarGridSpec(
            num_scalar_prefetch=2,
            grid=(nb,),
            in_specs=[rows_spec, rows_spec,
                      pl.BlockSpec((1, d, ff), lambda i, be_r, nv_r: (layer * e_num + be_r[i], 0, 0)),
                      pl.BlockSpec((1, d, ff), lambda i, be_r, nv_r: (layer * e_num + be_r[i], 0, 0)),
                      pl.BlockSpec((1, ff, d), lambda i, be_r, nv_r: (layer * e_num + be_r[i], 0, 0))],
            out_specs=[rows_spec, rows_spec],
            scratch_shapes=[pltpu.VMEM((d, ff), BF16), pltpu.VMEM((d, ff), BF16), pltpu.VMEM((ff, d), BF16)],
        ),
        out_shape=[jax.ShapeDtypeStruct((n_rows, quarter), I32)] * 2,
        compiler_params=_cparams(("arbitrary",)),
        name="moe_experts",
    )(blk_expert, blk_valid, xa, xb, w1, w3, w2)

    parts = MOE_COMBINE_PARTS if bsz % MOE_COMBINE_PARTS == 0 else 1
    pb = bsz // parts
    pn = n // parts
    pt = nt // parts
    out = None
    for p in range(parts):
        tok = slice(p * pn, (p + 1) * pn)
        dest12 = jnp.concatenate([dest1[:, tok], dest2[:, tok]], axis=1)
        ga = _sc_gather_rows(ya, dest12)
        gb = _sc_gather_rows(yb, dest12)
        first = pl.BlockSpec((ts, quarter), lambda b, i: (b * nt_b + i, 0))
        second = pl.BlockSpec((ts, quarter), lambda b, i: (pt + b * nt_b + i, 0))
        tile_p = pl.BlockSpec((1, ts, d), lambda b, i, p=p: (p * pb + b, i, 0))
        vec_p = pl.BlockSpec((1, 1, d), lambda b, i, p=p: (p * pb + b, 0, 0))
        mf_p = pl.BlockSpec((ts, LANES), lambda b, i, p=p: ((p * pb + b) * nt_b + i, 0))
        in_specs = [tile_p, vec_p, mf_p, _const_spec((1, d)), _const_spec((1, d)), first, first, second, second]
        args = [x, g, mf, ln_g.reshape(1, d), ln_b.reshape(1, d), ga, gb, ga, gb]
        aliases = {}
        if out is not None:
            in_specs.append(pl.BlockSpec(memory_space=pl.ANY))
            args.append(out)
            aliases = {len(args) - 1: 0}
        out = pl.pallas_call(
            functools.partial(_combine_kernel, alpha),
            grid=(pb, nt_b),
            in_specs=in_specs,
            out_specs=tile_p,
            out_shape=jax.ShapeDtypeStruct((bsz, s, d), F32),
            input_output_aliases=aliases,
            compiler_params=_cparams(("arbitrary", "arbitrary")),
            name="moe_combine",
        )(*args)
    return out


def kernel(x, c, ada_w, ada_b, ln_g, ln_b, hgrn_w_in, hgrn_w_out, hgrn_lb, hgrn_norm_w, attn_w_in, attn_w_out,
           attn_lambda, attn_subln_w, router_g_w, router_g_b, router_e_w, router_e_b, moe_w1, moe_w3, moe_w2):
    depth = ada_w.shape[0]
    bsz, s, d = x.shape
    alpha = (2 * depth) ** 0.25
    lb_all = jnp.cumsum(jax.nn.softmax(hgrn_lb.astype(F32), axis=0), axis=0)
    lb_all = lb_all - lb_all[0:1]
    mod = _ada_mod(c, ada_w, ada_b).reshape(depth, bsz, 6, 1, d)
    w1_all = moe_w1.reshape((-1,) + moe_w1.shape[2:])
    w3_all = moe_w3.reshape((-1,) + moe_w3.shape[2:])
    w2_all = moe_w2.reshape((-1,) + moe_w2.shape[2:])
    for i in range(depth):
        sh1, sc1, g1, sh2, sc2, g2 = (mod[i, :, m] for m in range(6))
        j = i // 2
        if i % 2 == 0:
            x = _hgrn_layer(x, sh1, sc1, g1, hgrn_w_in[j], hgrn_w_out[j], lb_all[j], hgrn_norm_w[j],
                            ln_g[i, 0], ln_b[i, 0], alpha)
        else:
            lambda_init = 0.8 - 0.6 * math.exp(-0.3 * i)
            x = _attn_layer(x, sh1, sc1, g1, attn_w_in[j], attn_w_out[j], attn_lambda[j], attn_subln_w[j],
                            lambda_init, ln_g[i, 0], ln_b[i, 0], alpha)
        x = _moe_layer(x, sh2, sc2, g2, router_g_w[i], router_g_b[i], router_e_w[i], router_e_b[i],
                       w1_all, w3_all, w2_all, i, ln_g[i, 1], ln_b[i, 1], alpha)
    return x
```
